```python
import math
import jax
import jax.numpy as jnp
from jax import lax
import numpy as np

D_MODEL = 1024
BATCH = 8
SEQ = 4096
DEPTH = 2

HEAD_DIM = 64
Q_BLOCK = 128
NORM_EPS = 1e-6

REL_BUCKETS = 32
REL_MAX_DIST = 2048

A_HEADS = 8
A_IDX_HEADS = 4
A_IDX_DIM = 64
A_TOPK_MAX = 256

B_HEADS = 8
B_KV_HEADS = 2
B_GROUP = B_HEADS // B_KV_HEADS
B_CMP_LEN = 32
B_CMP_STRIDE = 16
B_SEL_LEN = 64
B_SEL_TOPK_MAX = 16
B_WINDOW = 512
B_Q_BLOCK = 64

C_GROUPS = ((128, 1), (512, 4), (2048, 16))
C_HEADS_PER_GROUP = 4
C_HEADS = C_HEADS_PER_GROUP * len(C_GROUPS)
C_PAD = 2048

REL_HEADS = A_HEADS + B_HEADS + C_HEADS

D_FF = ((8 * D_MODEL + 3 * 256 - 1) // (3 * 256)) * 256

IN_SIZES = (
    A_HEADS * HEAD_DIM,
    HEAD_DIM,
    HEAD_DIM,
    A_IDX_HEADS * A_IDX_DIM,
    A_IDX_DIM,
    A_IDX_HEADS,
    B_HEADS * HEAD_DIM,
    6 * B_KV_HEADS * HEAD_DIM,
    3 * B_HEADS,
    3 * C_HEADS * HEAD_DIM,
    3 * D_MODEL,
)
IN_TOTAL = sum(IN_SIZES)

kernel_name = "hybrid_dsa_nsa_dilated_gated_block"


def _rms(x, g):
    xf = x.astype(jnp.float32)
    y = xf * lax.rsqrt(jnp.mean(xf * xf, axis=-1, keepdims=True) + NORM_EPS)
    return (y * g.astype(jnp.float32)).astype(x.dtype)


def _split_last(y, sizes):
    out, off = [], 0
    for n in sizes:
        out.append(y[..., off:off + n])
        off += n
    return out


def _rel_bucket(dist):
    n = jnp.maximum(dist, 0)
    exact = REL_BUCKETS // 2
    nf = jnp.maximum(n, 1).astype(jnp.float32)
    large = exact + (jnp.log(nf / exact) / math.log(REL_MAX_DIST / exact)
                     * (REL_BUCKETS - exact)).astype(jnp.int32)
    return jnp.where(n < exact, n, jnp.minimum(large, REL_BUCKETS - 1))


def _masked_softmax(logits, mask):
    s = jnp.where(mask, logits, -jnp.inf)
    m = jnp.max(s, axis=-1, keepdims=True)
    m = jnp.where(jnp.isfinite(m), m, 0.0)
    e = jnp.exp(s - m)
    den = jnp.sum(e, axis=-1, keepdims=True)
    p = e / jnp.maximum(den, 1e-30)
    lse = (m + jnp.log(den))[..., 0]
    return p, lse


def _sweep(fn, seq, block):
    out = lax.map(fn, jnp.arange(seq // block))
    out = jnp.moveaxis(out, 0, 1)
    return out.reshape((out.shape[0], seq) + out.shape[3:])


_gather_rows = jax.vmap(lambda a, i: a[i])


def _dsa_attention(q, k, v, iq, ik, iw, rel_tab):
    seq = q.shape[1]
    topk = min(A_TOPK_MAX, seq // 4)
    scale = HEAD_DIM ** -0.5
    keys = jnp.arange(seq)

    def block(i):
        q0 = i * Q_BLOCK
        t = q0 + jnp.arange(Q_BLOCK)
        qb = lax.dynamic_slice_in_dim(q, q0, Q_BLOCK, axis=1)
        iqb = lax.dynamic_slice_in_dim(iq, q0, Q_BLOCK, axis=1)
        iwb = lax.dynamic_slice_in_dim(iw, q0, Q_BLOCK, axis=1).astype(jnp.float32)
        rel = jax.nn.relu(jnp.einsum("bqhd,bsd->bqhs", iqb, ik, preferred_element_type=jnp.float32))
        score = jnp.einsum("bqhs,bqh->bqs", rel, iwb)
        score = jnp.where(keys[None, None, :] <= t[None, :, None], score, -jnp.inf)
        _, idx = lax.top_k(score, topk)
        kg = _gather_rows(k, idx)
        vg = _gather_rows(v, idx)
        dist = t[None, :, None] - idx
        bias = jnp.moveaxis(rel_tab[_rel_bucket(dist)], -1, 1)
        logits = jnp.einsum("bqhd,bqkd->bhqk", qb, kg, preferred_element_type=jnp.float32) * scale + bias
        p, _ = _masked_softmax(logits, (dist >= 0)[:, None])
        return jnp.einsum("bhqk,bqkd->bqhd", p.astype(vg.dtype), vg).astype(q.dtype)

    o = _sweep(block, seq, Q_BLOCK)
    return o.reshape(o.shape[:2] + (A_HEADS * HEAD_DIM,))


def _nsa_attention(q, kc, vc, ks, vs, kw, vw, gates, cmp_pos, cmp_w, k_gain, rel_tab):
    bsz, seq = q.shape[:2]
    G, Hg, Dh = B_KV_HEADS, B_GROUP, HEAD_DIM
    scale = Dh ** -0.5
    n_cmp = (seq - B_CMP_LEN) // B_CMP_STRIDE + 1
    n_sel = seq // B_SEL_LEN
    n_top = min(B_SEL_TOPK_MAX, n_sel)

    cidx = jnp.arange(n_cmp)[:, None] * B_CMP_STRIDE + jnp.arange(B_CMP_LEN)[None, :]

    def compress(src, pos, w):
        blk = src[:, cidx] + pos[:, None, :]
        return jnp.einsum("bnlgd,lde->bnge", blk, w)

    kcmp = _rms(compress(kc, cmp_pos[0], cmp_w[0]), k_gain)
    vcmp = compress(vc, cmp_pos[1], cmp_w[1])
    cstart = cidx[:, 0]
    cend = cidx[:, -1]
    sstart = jnp.arange(n_sel) * B_SEL_LEN
    overlap = ((cstart[:, None] < sstart[None, :] + B_SEL_LEN)
               & (cstart[:, None] + B_CMP_LEN > sstart[None, :])).astype(jnp.float32)
    ksb = jnp.moveaxis(ks.reshape(bsz, n_sel, B_SEL_LEN, G, Dh), 3, 1)
    vsb = jnp.moveaxis(vs.reshape(bsz, n_sel, B_SEL_LEN, G, Dh), 3, 1)
    wpad = ((0, 0), (B_WINDOW, 0), (0, 0), (0, 0))
    kw_pad = jnp.pad(kw, wpad)
    vw_pad = jnp.pad(vw, wpad)
    tab = rel_tab.reshape(REL_BUCKETS, G, Hg)
    sel_blocks = jnp.arange(n_sel)
    gather_blocks = jax.vmap(jax.vmap(lambda a, i: a[i]))
    bias_per_group = jax.vmap(lambda tb, bk: tb[bk], in_axes=(1, 1), out_axes=1)

    def block(i):
        q0 = i * B_Q_BLOCK
        t = q0 + jnp.arange(B_Q_BLOCK)
        qb = lax.dynamic_slice_in_dim(q, q0, B_Q_BLOCK, axis=1).reshape(bsz, B_Q_BLOCK, G, Hg, Dh)
        gb = lax.dynamic_slice_in_dim(gates, q0, B_Q_BLOCK, axis=1).reshape(bsz, B_Q_BLOCK, G, Hg, 3)
        lc = jnp.einsum("bqghd,bngd->bghqn", qb, kcmp, preferred_element_type=jnp.float32) * scale
        pc, _ = _masked_softmax(lc, cend[None, :] <= t[:, None])
        oc = jnp.einsum("bghqn,bngd->bqghd", pc.astype(vcmp.dtype), vcmp)
        imp = jnp.einsum("bghqn,nm->bgqm", pc, overlap)
        jt = (t // B_SEL_LEN)[:, None]
        forced = (sel_blocks[None] == 0) | (sel_blocks[None] == jt) | (sel_blocks[None] == jt - 1)
        imp = jnp.where(forced, jnp.inf, jnp.where(sel_blocks[None] <= jt, imp, -jnp.inf))
        _, sel = lax.top_k(imp, n_top)
        kg = gather_blocks(ksb, sel).reshape(bsz, G, B_Q_BLOCK, n_top * B_SEL_LEN, Dh)
        vg = gather_blocks(vsb, sel).reshape(bsz, G, B_Q_BLOCK, n_top * B_SEL_LEN, Dh)
        pos = (sel[..., None] * B_SEL_LEN + jnp.arange(B_SEL_LEN)).reshape(bsz, G, B_Q_BLOCK, -1)
        dist = t[None, None, :, None] - pos
        bias_s = jnp.moveaxis(bias_per_group(tab, _rel_bucket(dist)), -1, 2)
        ls = jnp.einsum("bqghd,bgqkd->bghqk", qb, kg, preferred_element_type=jnp.float32) * scale + bias_s
        ps, _ = _masked_softmax(ls, (dist >= 0)[:, :, None])
        osel = jnp.einsum("bghqk,bgqkd->bqghd", ps.astype(vg.dtype), vg)
        kwb = lax.dynamic_slice_in_dim(kw_pad, q0, B_Q_BLOCK + B_WINDOW, axis=1)
        vwb = lax.dynamic_slice_in_dim(vw_pad, q0, B_Q_BLOCK + B_WINDOW, axis=1)
        kpos = q0 - B_WINDOW + jnp.arange(B_Q_BLOCK + B_WINDOW)
        dw = t[:, None] - kpos[None, :]
        mw = (dw >= 0) & (dw < B_WINDOW) & (kpos[None, :] >= 0)
        bias_w = jnp.transpose(tab[_rel_bucket(dw)], (2, 3, 0, 1))
        lw = jnp.einsum("bqghd,bkgd->bghqk", qb, kwb, preferred_element_type=jnp.float32) * scale + bias_w
        pw, _ = _masked_softmax(lw, mw)
        ow = jnp.einsum("bghqk,bkgd->bqghd", pw.astype(vwb.dtype), vwb)
        o = gb[..., 0:1] * oc + gb[..., 1:2] * osel + gb[..., 2:3] * ow
        return o.reshape(bsz, B_Q_BLOCK, G * Hg * Dh).astype(q.dtype)

    return _sweep(block, seq, B_Q_BLOCK)


def _dilated_attention(q, k, v, rel_tab):
    bsz, seq = q.shape[:2]
    scale = HEAD_DIM ** -0.5
    pad = ((0, 0), (C_PAD, 0), (0, 0), (0, 0))
    hsl = [slice(g * C_HEADS_PER_GROUP, (g + 1) * C_HEADS_PER_GROUP) for g in range(len(C_GROUPS))]
    kps = [jnp.pad(k[:, :, hs], pad) for hs in hsl]
    vps = [jnp.pad(v[:, :, hs], pad) for hs in hsl]

    def block(i):
        q0 = i * Q_BLOCK
        t = q0 + jnp.arange(Q_BLOCK)
        qb = lax.dynamic_slice_in_dim(q, q0, Q_BLOCK, axis=1)
        outs, lses = [], []
        for gi, (win, dil) in enumerate(C_GROUPS):
            steps = jnp.arange(win // dil + 1) * dil
            pos = t[:, None] - steps[None, :]
            kg = jnp.take(kps[gi], pos + C_PAD, axis=1)
            vg = jnp.take(vps[gi], pos + C_PAD, axis=1)
            bias = rel_tab[_rel_bucket(steps), hsl[gi]].T[None, :, None, :]
            logits = jnp.einsum("bqhd,bqkhd->bhqk", qb[:, :, hsl[gi]], kg,
                                preferred_element_type=jnp.float32) * scale + bias
            p, lse = _masked_softmax(logits, (pos >= 0)[None, None])
            outs.append(jnp.einsum("bhqk,bqkhd->bqhd", p.astype(vg.dtype), vg))
            lses.append(lse)
        w = jax.nn.softmax(jnp.stack(lses), axis=0)
        w = jnp.transpose(w, (0, 1, 3, 2))[..., None]
        o = w[0] * outs[0]
        for gi in range(1, len(C_GROUPS)):
            o = o + w[gi] * outs[gi]
        return o.reshape(bsz, Q_BLOCK, C_HEADS_PER_GROUP * HEAD_DIM).astype(q.dtype)

    return _sweep(block, seq, Q_BLOCK)


def setup_inputs(seed: int = 0) -> dict:
    key = jax.random.key(seed)
    ks = jax.random.split(key, 14)

    def nrm(k, shape, scale):
        return jax.random.normal(k, shape, jnp.float32) * scale

    return {
        "x": nrm(ks[0], (BATCH, SEQ, D_MODEL), 1.0),
        "norm1_g": 1.0 + nrm(ks[1], (DEPTH, D_MODEL), 0.1),
        "norm2_g": 1.0 + nrm(ks[2], (DEPTH, D_MODEL), 0.1),
        "w_in": nrm(ks[3], (DEPTH, D_MODEL, IN_TOTAL), D_MODEL ** -0.5),
        "qk_norm_g": 1.0 + nrm(ks[4], (DEPTH, 6, HEAD_DIM), 0.1),
        "nsa_cmp_pos": nrm(ks[5], (DEPTH, 2, B_CMP_LEN, HEAD_DIM), 0.5),
        "nsa_cmp_w": nrm(ks[6], (DEPTH, 2, B_CMP_LEN, HEAD_DIM, HEAD_DIM), (B_CMP_LEN * HEAD_DIM) ** -0.5),
        "w_branch_a": nrm(ks[7], (DEPTH, A_HEADS * HEAD_DIM, D_MODEL), (A_HEADS * HEAD_DIM) ** -0.5),
        "w_branch_b": nrm(ks[8], (DEPTH, B_HEADS * HEAD_DIM, D_MODEL), (B_HEADS * HEAD_DIM) ** -0.5),
        "w_branch_c": nrm(ks[9], (DEPTH, C_HEADS_PER_GROUP * HEAD_DIM, D_MODEL), (C_HEADS_PER_GROUP * HEAD_DIM) ** -0.5),
        "w_out": nrm(ks[10], (DEPTH, D_MODEL, D_MODEL), D_MODEL ** -0.5),
        "w_ffn_in": nrm(ks[11], (DEPTH, D_MODEL, 2 * D_FF), D_MODEL ** -0.5),
        "w_ffn_out": nrm(ks[12], (DEPTH, D_FF, D_MODEL), D_FF ** -0.5),
        "rel_bias": nrm(ks[13], (REL_BUCKETS, REL_HEADS), 0.5),
    }


def reference(x, norm1_g, norm2_g, w_in, qk_norm_g, nsa_cmp_pos, nsa_cmp_w, w_branch_a, w_branch_b,
              w_branch_c, w_out, w_ffn_in, w_ffn_out, rel_bias):
    bsz, seq, _ = x.shape
    rel_a = rel_bias[:, :A_HEADS]
    rel_b = rel_bias[:, A_HEADS:A_HEADS + B_HEADS]
    rel_c = rel_bias[:, A_HEADS + B_HEADS:]
    for layer in range(DEPTH):
        h = _rms(x, norm1_g[layer])
        (a_q, a_k, a_v, i_q, i_k, i_w, b_q, b_kv, b_g, c_qkv, mix_g) = _split_last(h @ w_in[layer], IN_SIZES)
        qk = qk_norm_g[layer]
        a_q = _rms(a_q.reshape(bsz, seq, A_HEADS, HEAD_DIM), qk[0])
        a_k = _rms(a_k, qk[1])
        y_a = _dsa_attention(a_q, a_k, a_v, i_q.reshape(bsz, seq, A_IDX_HEADS, A_IDX_DIM), i_k, i_w,
                             rel_a) @ w_branch_a[layer]
        b_q = _rms(b_q.reshape(bsz, seq, B_HEADS, HEAD_DIM), qk[2])
        b_kv = b_kv.reshape(bsz, seq, 6, B_KV_HEADS, HEAD_DIM)
        b_gates = jax.nn.sigmoid(b_g).reshape(bsz, seq, B_HEADS, 3)
        y_b = _nsa_attention(b_q, b_kv[:, :, 0], b_kv[:, :, 1], _rms(b_kv[:, :, 2], qk[3]), b_kv[:, :, 3],
                             _rms(b_kv[:, :, 4], qk[3]), b_kv[:, :, 5], b_gates, nsa_cmp_pos[layer],
                             nsa_cmp_w[layer], qk[3], rel_b) @ w_branch_b[layer]
        c_qkv = c_qkv.reshape(bsz, seq, 3, C_HEADS, HEAD_DIM)
        y_c = _dilated_attention(_rms(c_qkv[:, :, 0], qk[4]), _rms(c_qkv[:, :, 1], qk[5]), c_qkv[:, :, 2],
                                 rel_c) @ w_branch_c[layer]
        g = jax.nn.sigmoid(mix_g).reshape(bsz, seq, 3, D_MODEL)
        x = x + (g[:, :, 0] * y_a + g[:, :, 1] * y_b + g[:, :, 2] * y_c) @ w_out[layer]
        h = _rms(x, norm2_g[layer])
        gate, up = jnp.split(h @ w_ffn_in[layer], 2, axis=-1)
        x = x + (jax.nn.silu(gate) * up) @ w_ffn_out[layer]
    return x
```

```python
import functools
import math

import numpy as np
import jax
import jax.numpy as jnp
from jax import lax
from jax.experimental import pallas as pl
from jax.experimental.pallas import tpu as pltpu

F32 = jnp.float32
BF16 = jnp.bfloat16
I32 = jnp.int32

D_MODEL = 1024
HEAD_DIM = 64
NORM_EPS = 1e-6
REL_BUCKETS = 32
REL_MAX_DIST = 2048

A_HEADS = 8
A_IDX_HEADS = 4
A_TOPK_MAX = 256
B_HEADS = 8
B_KV_HEADS = 2
B_GROUP = B_HEADS // B_KV_HEADS
B_CMP_LEN = 32
B_CMP_STRIDE = 16
B_SEL_LEN = 64
B_SEL_TOPK_MAX = 16
B_WINDOW = 512
C_GROUPS = ((128, 1), (512, 4), (2048, 16))
C_HEADS_PER_GROUP = 4
C_HEADS = C_HEADS_PER_GROUP * len(C_GROUPS)
D_FF = ((8 * D_MODEL + 3 * 256 - 1) // (3 * 256)) * 256

_O_AQ = 0
_O_AK = _O_AQ + A_HEADS * HEAD_DIM
_O_AV = _O_AK + HEAD_DIM
_O_IQ = _O_AV + HEAD_DIM
_O_IK = _O_IQ + A_IDX_HEADS * HEAD_DIM
_O_IW = _O_IK + HEAD_DIM
_O_BQ = _O_IW + A_IDX_HEADS
_O_BKV = _O_BQ + B_HEADS * HEAD_DIM
_O_BG = _O_BKV + 6 * B_KV_HEADS * HEAD_DIM
_O_CQ = _O_BG + 3 * B_HEADS
_O_CK = _O_CQ + C_HEADS * HEAD_DIM
_O_CV = _O_CK + C_HEADS * HEAD_DIM
_O_MIX = _O_CV + C_HEADS * HEAD_DIM
_O_END = _O_MIX + 3 * D_MODEL

TQ = 128
TK = 128
LANES = 128
VMEM_LIMIT = 56 * 1024 * 1024
INT_MIN = -2 ** 31
NEG_INIT = -1e30
SCALE = HEAD_DIM ** -0.5


def _params(sem):
    return pltpu.CompilerParams(dimension_semantics=sem, vmem_limit_bytes=VMEM_LIMIT)


def _dot_t(a, b):
    return lax.dot_general(a, b, (((1,), (1,)), ((), ())), preferred_element_type=F32)


def _dot(a, b):
    return jnp.dot(a, b, preferred_element_type=F32)


def _split_dot(a, b_bf16):
    hi = a.astype(BF16)
    lo = (a - hi.astype(F32)).astype(BF16)
    return _dot(hi, b_bf16) + _dot(lo, b_bf16)


def _bucket_table(n_max):
    n = np.arange(n_max, dtype=np.int64)
    exact = REL_BUCKETS // 2
    nf = np.maximum(n, 1).astype(np.float32)
    large = exact + (np.log(nf / np.float32(exact)) / np.float32(math.log(REL_MAX_DIST / exact))
                     * np.float32(REL_BUCKETS - exact)).astype(np.int32)
    return np.where(n < exact, n, np.minimum(large, REL_BUCKETS - 1)).astype(np.int32)


def _num_bias_tiles(seq):
    bucket = _bucket_table(seq + TQ)
    first_sat = int(np.min(np.nonzero(bucket == REL_BUCKETS - 1)[0]))
    assert np.all(bucket[first_sat:] == REL_BUCKETS - 1)
    nd = -(-(first_sat + TK - 1) // TQ) + 1
    return min(nd, seq // TQ)


def _toeplitz_bias(rel_cols, seq):
    nd = _num_bias_tiles(seq)
    bucket = _bucket_table(seq + TQ)
    d = np.arange(nd)[:, None, None] * TQ + np.arange(TQ)[None, :, None] - np.arange(TK)[None, None, :]
    idx = bucket[np.clip(d, 0, None)]
    t = rel_cols.astype(F32)[idx]
    h = rel_cols.shape[1]
    return jnp.transpose(t, (0, 3, 1, 2)).reshape(nd * h, TQ, TK), nd


def _dilated_bias(rel_cols, dil):
    bucket = _bucket_table(2 * TK * dil + 1)
    du = np.arange(TQ)[:, None] + TK - np.arange(2 * TK)[None, :]
    idx = bucket[np.clip(du, 0, None) * dil]
    return jnp.transpose(rel_cols.astype(F32)[idx], (2, 0, 1))


def _proj_kernel(kinds, *refs):
    n = len(kinds)
    x_ref, g_ref, gs_ref = refs[0], refs[1], refs[2]
    w_refs = refs[3:3 + 2 * n:2]
    aux_refs = refs[4:4 + 2 * n:2]
    out_refs = refs[3 + 2 * n:]
    x = x_ref[...]
    ms = jnp.mean(x * x, axis=-1, keepdims=True)
    h = (x * lax.rsqrt(ms + NORM_EPS) * g_ref[...]).astype(BF16)
    for kind, w_ref, aux_ref, o_ref in zip(kinds, w_refs, aux_refs, out_refs):
        width = w_ref.shape[1]
        cw = 256 if width % 256 == 0 else LANES
        for c0 in range(0, width, cw):
            y = _dot(h, w_ref[:, c0:c0 + cw])
            if kind == "norm":
                gsum = _split_dot(y * y, gs_ref[:cw, :cw])
                r = lax.rsqrt(gsum * (1.0 / HEAD_DIM) + NORM_EPS)
                mask = aux_ref[0:1, c0:c0 + cw]
                fac = mask * (r * aux_ref[1:2, c0:c0 + cw]) + (1.0 - mask)
                y = y * fac + aux_ref[2:3, c0:c0 + cw]
            elif kind == "sigmoid":
                y = 1.0 / (1.0 + jnp.exp(-y))
            o_ref[:, c0:c0 + cw] = y.astype(o_ref.dtype)


def _proj(x2d, gain, pieces, tm):
    m, d = x2d.shape
    kinds = tuple(p[0] for p in pieces)
    gs = (np.arange(256)[:, None] // HEAD_DIM == np.arange(256)[None, :] // HEAD_DIM)
    gs = jnp.asarray(gs, BF16)
    in_specs = [pl.BlockSpec((tm, d), lambda i: (i, 0)),
                pl.BlockSpec((1, d), lambda i: (0, 0)),
                pl.BlockSpec((256, 256), lambda i: (0, 0))]
    args = [x2d, gain.reshape(1, d).astype(F32), gs]
    out_specs, out_shapes = [], []
    for _, w, aux, dt in pieces:
        nw = w.shape[1]
        in_specs += [pl.BlockSpec((d, nw), lambda i: (0, 0)), pl.BlockSpec((3, nw), lambda i: (0, 0))]
        args += [w, aux]
        out_specs.append(pl.BlockSpec((tm, nw), lambda i: (i, 0)))
        out_shapes.append(jax.ShapeDtypeStruct((m, nw), dt))
    return pl.pallas_call(
        functools.partial(_proj_kernel, kinds),
        grid=(m // tm,),
        in_specs=in_specs, out_specs=out_specs, out_shape=out_shapes,
        compiler_params=_params(("parallel",)),
        name="proj",
    )(*args)


def _aux(width, mask=None, gain=None, add=None):
    z = jnp.zeros((width,), F32)
    return jnp.stack([z if mask is None else mask, z if gain is None else gain, z if add is None else add])


def _seg(*parts):
    ref = next(p for p in parts if not isinstance(p, int))
    return jnp.concatenate([jnp.zeros(ref.shape[:-1] + (p,), ref.dtype) if isinstance(p, int) else p
                            for p in parts], axis=-1)


def _proj_pieces(w_in, qk):
    w = w_in.astype(BF16)
    hd = HEAD_DIM
    ones, zeros = jnp.ones((hd,), F32), jnp.zeros((hd,), F32)
    cat = jnp.concatenate

    def cols(a, b):
        return w[:, a:b]

    pieces_ab = [
        ("norm", cols(_O_AQ, _O_AK), _aux(512, jnp.ones((512,), F32), jnp.tile(qk[0], A_HEADS) * SCALE), BF16),
        ("norm", _seg(cols(_O_AK, _O_AV), hd, cols(_O_AV, _O_IQ), hd),
         _aux(256, cat([ones, zeros, zeros, zeros]), cat([qk[1], zeros, zeros, zeros]),
              cat([zeros, zeros, zeros, ones])), BF16),
        ("plain", cols(_O_IQ, _O_IK), _aux(256), BF16),
        ("plain", _seg(cols(_O_IK, _O_BQ), LANES - hd - A_IDX_HEADS), _aux(LANES), F32),
        ("norm", cols(_O_BQ, _O_BKV), _aux(512, jnp.ones((512,), F32), jnp.tile(qk[2], B_HEADS) * SCALE), BF16),
        ("plain", cols(_O_BKV, _O_BKV + 256), _aux(256), F32),
    ]
    o = _O_BKV + 256
    ks0, ks1, vs0, vs1, kw0, kw1, vw0, vw1 = [cols(o + i * hd, o + (i + 1) * hd) for i in range(8)]
    kmask = cat([ones, zeros] * 4 + [zeros, zeros] * 4)
    kgain = cat([qk[3], zeros] * 4 + [zeros, zeros] * 4)
    vadd = cat([zeros, zeros] * 4 + [zeros, ones] * 4)
    pieces_ab += [
        ("norm", _seg(ks0, hd, ks1, hd, kw0, hd, kw1, hd, vs0, hd, vs1, hd, vw0, hd, vw1, hd),
         _aux(1024, kmask, kgain, vadd), BF16),
        ("sigmoid", _seg(cols(_O_BG, _O_CQ), LANES - 3 * B_HEADS), _aux(LANES), F32),
    ]
    pieces_c = [
        ("norm", cols(_O_CQ, _O_CK), _aux(768, jnp.ones((768,), F32), jnp.tile(qk[4], C_HEADS) * SCALE), BF16),
        ("norm", cols(_O_CK, _O_CV), _aux(768, jnp.ones((768,), F32), jnp.tile(qk[5], C_HEADS)), BF16),
        ("plain", cols(_O_CV, _O_MIX), _aux(768), BF16),
    ]
    pieces_g = [("sigmoid", cols(_O_MIX, _O_END), _aux(3 * D_MODEL), F32)]
    return pieces_ab, pieces_c, pieces_g


def _online_update(m, acc, s, v_aug):
    m_new = jnp.maximum(m, jnp.max(s, axis=-1, keepdims=True))
    p = jnp.exp(s - m_new)
    acc = jnp.exp(m - m_new) * acc + _dot(p.astype(BF16), v_aug)
    return m_new, acc


def _finish(acc):
    return acc[:, :HEAD_DIM] / jnp.maximum(acc[:, HEAD_DIM:HEAD_DIM + 1], 1e-30)


def _stack_heads(q, h0, nh):
    return jnp.concatenate([q[:, (h0 + h) * HEAD_DIM:(h0 + h + 1) * HEAD_DIM] for h in range(nh)], axis=0)


def _dsa_kernel(topk, nd, pbits, iq_ref, ikwq_ref, ikw_ref, aq_ref, akv_ref, bias_ref, o_ref, keys_ref):
    i = pl.program_id(1)
    nk = i + 1
    row = lax.broadcasted_iota(I32, (TQ, TK), 0)
    col = lax.broadcasted_iota(I32, (TQ, TK), 1)
    tpos = i * TQ + row

    iq = iq_ref[...]
    iw = ikwq_ref[...]
    wb = [jnp.broadcast_to(iw[:, HEAD_DIM + h:HEAD_DIM + h + 1], (TQ, TK)) for h in range(A_IDX_HEADS)]

    def score_tile(j, c):
        r0 = pl.multiple_of(j * TK, TK)
        ik_t = ikw_ref[pl.ds(r0, TK), 0:HEAD_DIM].astype(BF16)
        acc = jnp.zeros((TQ, TK), F32)
        for h in range(A_IDX_HEADS):
            d = _dot_t(iq[:, h * HEAD_DIM:(h + 1) * HEAD_DIM], ik_t)
            acc = acc + jnp.maximum(d, 0.0) * wb[h]
        acc = jnp.where(acc == 0.0, 0.0, acc)
        bits = lax.bitcast_convert_type(acc, I32)
        key = jnp.where(bits < 0, bits ^ 0x7FFFFFFF, bits)
        keys_ref[j] = jnp.where(j * TK + col <= tpos, key, INT_MIN)
        return c

    lax.fori_loop(0, nk, score_tile, 0)

    def count(pred):
        def body(j, acc):
            return acc + jnp.where(pred(j, keys_ref[j]), 1.0, 0.0)
        acc = lax.fori_loop(0, nk, body, jnp.zeros((TQ, TK), F32))
        return jnp.sum(acc, axis=-1, keepdims=True)

    kf = float(topk)

    def thr_step(b, t):
        cand = t ^ jnp.left_shift(jnp.int32(1), 31 - b)
        cb = jnp.broadcast_to(cand, (TQ, TK))
        cnt = count(lambda j, k: k >= cb)
        return jnp.where(cnt >= kf, cand, t)

    thr = lax.fori_loop(0, 32, thr_step, jnp.full((TQ, 1), INT_MIN, I32))
    tb = jnp.broadcast_to(thr, (TQ, TK))

    need = kf - count(lambda j, k: k > tb)

    def tie_step(b, p):
        cand = p | jnp.left_shift(jnp.int32(1), pbits - 1 - b)
        cb = jnp.broadcast_to(cand, (TQ, TK))
        cnt = count(lambda j, k: (k == tb) & (j * TK + col < cb))
        return jnp.where(cnt < need, cand, p)

    plim = lax.fori_loop(0, pbits, tie_step, jnp.zeros((TQ, 1), I32))
    pb = jnp.broadcast_to(plim, (TQ, TK))

    qs = _stack_heads(aq_ref[...], 0, A_HEADS)

    def att_tile(j, carry):
        m, acc = carry
        r0 = pl.multiple_of(j * TK, TK)
        kt = akv_ref[pl.ds(r0, TK), 0:HEAD_DIM]
        va = akv_ref[pl.ds(r0, TK), LANES:2 * LANES]
        key = keys_ref[j]
        spos = j * TK + col
        sel = ((key > tb) | ((key == tb) & (spos <= pb))) & (spos <= tpos)
        d = jnp.minimum(i - j, nd - 1)
        bias = bias_ref[pl.ds(d * A_HEADS, A_HEADS)]
        s = _dot_t(qs, kt).reshape(A_HEADS, TQ, TK) + bias
        s = jnp.where(sel[None], s, -jnp.inf).reshape(A_HEADS * TQ, TK)
        return _online_update(m, acc, s, va)

    m0 = jnp.full((A_HEADS * TQ, 1), NEG_INIT, F32)
    a0 = jnp.zeros((A_HEADS * TQ, LANES), F32)
    _, acc = lax.fori_loop(0, nk, att_tile, (m0, a0))
    out = _finish(acc)
    o_ref[...] = jnp.concatenate([out[h * TQ:(h + 1) * TQ] for h in range(A_HEADS)], axis=1).astype(o_ref.dtype)


def _dsa(iq, ikw, aq, akv, bias, nd, bsz, seq):
    nq = seq // TQ
    topk = min(A_TOPK_MAX, seq // 4)
    pbits = max(1, (seq - 1).bit_length())
    return pl.pallas_call(
        functools.partial(_dsa_kernel, topk, nd, pbits),
        grid=(bsz, nq),
        in_specs=[
            pl.BlockSpec((TQ, 256), lambda b, i: (b * nq + i, 0)),
            pl.BlockSpec((TQ, LANES), lambda b, i: (b * nq + i, 0)),
            pl.BlockSpec((seq, LANES), lambda b, i: (b, 0)),
            pl.BlockSpec((TQ, 512), lambda b, i: (b * nq + i, 0)),
            pl.BlockSpec((seq, 256), lambda b, i: (b, 0)),
            pl.BlockSpec(bias.shape, lambda b, i: (0, 0, 0)),
        ],
        out_specs=pl.BlockSpec((TQ, 512), lambda b, i: (b * nq + i, 0)),
        out_shape=jax.ShapeDtypeStruct((bsz * seq, 512), BF16),
        scratch_shapes=[pltpu.VMEM((seq // TK, TQ, TK), I32)],
        compiler_params=_params(("parallel", "arbitrary")),
        name="dsa",
    )(iq, ikw, ikw, aq, akv, bias)


def _nsa_cmp_kernel(x_ref, pos_ref, wlo_ref, whi_ref, gain_ref, k_ref, v_ref):
    x = x_ref[...]
    lo = _dot((x + pos_ref[0:1, :]).astype(BF16), wlo_ref[...])
    hi = _dot((x + pos_ref[1:2, :]).astype(BF16), whi_ref[...])
    nrow = x.shape[0]
    pre = lo + pltpu.roll(hi, nrow - 1, 0)
    ks = []
    for g in range(B_KV_HEADS):
        kg = pre[:, g * HEAD_DIM:(g + 1) * HEAD_DIM]
        ms = jnp.mean(kg * kg, axis=-1, keepdims=True)
        ks.append(kg * lax.rsqrt(ms + NORM_EPS) * gain_ref[...])
    k_ref[...] = jnp.concatenate(ks, axis=1).astype(k_ref.dtype)
    v_ref[...] = pre[:, LANES:2 * LANES].astype(v_ref.dtype)


def _nsa_cmp(bcmp, cmp_pos, cmp_w, k_gain, bsz, seq):
    nch = seq // B_CMP_STRIDE
    half = B_CMP_LEN // 2
    width = half * 256
    x = bcmp.reshape(bsz * nch, width)

    def wmat(l0):
        w = jnp.zeros((half, 4, HEAD_DIM, 4, HEAD_DIM), F32)
        for j in range(4):
            w = w.at[:, j, :, j, :].set(cmp_w[j // 2, l0:l0 + half])
        return w.reshape(width, 256).astype(BF16)

    def prow(l0):
        p = jnp.stack([cmp_pos[0, l0:l0 + half], cmp_pos[0, l0:l0 + half],
                       cmp_pos[1, l0:l0 + half], cmp_pos[1, l0:l0 + half]], axis=1)
        return p.reshape(width)

    pos = jnp.stack([prow(0), prow(half)]).astype(F32)
    return pl.pallas_call(
        _nsa_cmp_kernel,
        grid=(bsz,),
        in_specs=[
            pl.BlockSpec((nch, width), lambda b: (b, 0)),
            pl.BlockSpec((2, width), lambda b: (0, 0)),
            pl.BlockSpec((width, 256), lambda b: (0, 0)),
            pl.BlockSpec((width, 256), lambda b: (0, 0)),
            pl.BlockSpec((1, HEAD_DIM), lambda b: (0, 0)),
        ],
        out_specs=[pl.BlockSpec((nch, LANES), lambda b: (b, 0)), pl.BlockSpec((nch, LANES), lambda b: (b, 0))],
        out_shape=[jax.ShapeDtypeStruct((bsz * nch, LANES), BF16)] * 2,
        compiler_params=_params(("parallel",)),
        name="nsa_cmp",
    )(x, pos, wmat(0), wmat(half), k_gain.reshape(1, HEAD_DIM).astype(F32))


def _nsa_kernel(seq, nd, q_ref, g_ref, kc_ref, vc_ref, kv_ref, bias_ref, o_ref, imp_ref, sel_ref):
    i = pl.program_id(1)
    ncp = seq // B_CMP_STRIDE
    ns = seq // B_SEL_LEN
    n_top = min(B_SEL_TOPK_MAX, ns)
    hg = B_GROUP
    row = lax.broadcasted_iota(I32, (TQ, TK), 0)
    col = lax.broadcasted_iota(I32, (TQ, TK), 1)
    tpos = i * TQ + row
    q = q_ref[...]

    n_idx = lax.broadcasted_iota(I32, (ncp, TQ), 0)
    t_c = i * TQ + lax.broadcasted_iota(I32, (ncp, TQ), 1)
    cmask = n_idx * B_CMP_STRIDE + (B_CMP_LEN - 1) <= t_c
    om = lax.broadcasted_iota(I32, (ns, ncp), 0) * B_SEL_LEN
    on = lax.broadcasted_iota(I32, (ns, ncp), 1) * B_CMP_STRIDE
    ovt = jnp.where((on < om + B_SEL_LEN) & (on + B_CMP_LEN > om), 1.0, 0.0).astype(BF16)
    m_idx = lax.broadcasted_iota(I32, (ns, TQ), 0)
    jt = (i * TQ + lax.broadcasted_iota(I32, (ns, TQ), 1)) // B_SEL_LEN
    forced = (m_idx == 0) | (m_idx == jt) | (m_idx == jt - 1)

    oc = []
    for g in range(B_KV_HEADS):
        kc_g = kc_ref[:, g * HEAD_DIM:(g + 1) * HEAD_DIM]
        vc_g = vc_ref[:, g * HEAD_DIM:(g + 1) * HEAD_DIM]
        psum = jnp.zeros((ncp, TQ), F32)
        for hh in range(hg):
            h = g * hg + hh
            st = jnp.where(cmask, _dot_t(kc_g, q[:, h * HEAD_DIM:(h + 1) * HEAD_DIM]), -jnp.inf)
            mx = jnp.max(st, axis=0, keepdims=True)
            mx = jnp.where(mx == -jnp.inf, 0.0, mx)
            e = jnp.exp(st - mx)
            pt = e / jnp.maximum(jnp.sum(e, axis=0, keepdims=True), 1e-30)
            psum = psum + pt
            oc.append(_dot(pt.T.astype(BF16), vc_g))
        hi = psum.astype(BF16)
        lo = (psum - hi.astype(F32)).astype(BF16)
        imp = _dot(ovt, hi) + _dot(ovt, lo)
        imp = jnp.where(forced, jnp.inf, jnp.where(m_idx <= jt, imp, -jnp.inf))
        imp_ref[g] = imp

        def rank_step(mp, rank, g=g, imp=imp):
            vp = jnp.broadcast_to(imp_ref[g, pl.ds(mp, 1), :], (ns, TQ))
            before = (vp > imp) | ((vp == imp) & (mp < m_idx))
            return rank + jnp.where(before, 1.0, 0.0)

        rank = lax.fori_loop(0, ns, rank_step, jnp.zeros((ns, TQ), F32))
        selt = jnp.where(rank < float(n_top), 1.0, 0.0)
        selt = jnp.concatenate([selt, jnp.zeros((LANES - ns, TQ), F32)], axis=0)
        sel_ref[g] = selt.T.astype(BF16)

    qs = [_stack_heads(q, g * hg, hg) for g in range(B_KV_HEADS)]
    e_m = lax.broadcasted_iota(I32, (LANES, TK), 0)
    e_c = lax.broadcasted_iota(I32, (LANES, TK), 1) // B_SEL_LEN
    blocks_per_tile = TK // B_SEL_LEN

    def branch_tile(j, carry, koff, voff, mask_fn):
        r0 = pl.multiple_of(j * TK, TK)
        spos = j * TK + col
        d = jnp.minimum(i - j, nd - 1)
        out = []
        for g in range(B_KV_HEADS):
            m, acc = carry[2 * g], carry[2 * g + 1]
            kt = kv_ref[pl.ds(r0, TK), koff + g * LANES:koff + g * LANES + HEAD_DIM]
            va = kv_ref[pl.ds(r0, TK), voff + g * LANES:voff + (g + 1) * LANES]
            bias = bias_ref[pl.ds(d * B_HEADS + g * hg, hg)]
            s = _dot_t(qs[g], kt).reshape(hg, TQ, TK) + bias
            s = jnp.where(mask_fn(g, j, spos)[None], s, -jnp.inf).reshape(hg * TQ, TK)
            out += list(_online_update(m, acc, s, va))
        return tuple(out)

    def sel_mask(g, j, spos):
        expand = jnp.where(e_m == j * blocks_per_tile + e_c, 1.0, 0.0).astype(BF16)
        return (_dot(sel_ref[g], expand) > 0.5) & (spos <= tpos)

    def win_mask(g, j, spos):
        return (spos <= tpos) & (tpos - spos < B_WINDOW)

    init = (jnp.full((hg * TQ, 1), NEG_INIT, F32), jnp.zeros((hg * TQ, LANES), F32)) * B_KV_HEADS
    sel_c = lax.fori_loop(0, i + 1, lambda j, c: branch_tile(j, c, 0, 4 * LANES, sel_mask), init)
    jlo = jnp.maximum(i - B_WINDOW // TK, 0)
    win_c = lax.fori_loop(jlo, i + 1, lambda j, c: branch_tile(j, c, 2 * LANES, 6 * LANES, win_mask), init)

    gates = g_ref[...]
    outs = []
    for g in range(B_KV_HEADS):
        osel = _finish(sel_c[2 * g + 1])
        owin = _finish(win_c[2 * g + 1])
        for hh in range(hg):
            h = g * hg + hh
            rs = slice(hh * TQ, (hh + 1) * TQ)
            outs.append(gates[:, 3 * h:3 * h + 1] * oc[h] + gates[:, 3 * h + 1:3 * h + 2] * osel[rs]
                        + gates[:, 3 * h + 2:3 * h + 3] * owin[rs])
    o_ref[...] = jnp.concatenate(outs, axis=1).astype(o_ref.dtype)


def _nsa(bq, bg, kcmp, vcmp, bsw, bias, nd, bsz, seq):
    nq = seq // TQ
    ncp = seq // B_CMP_STRIDE
    ns = seq // B_SEL_LEN
    return pl.pallas_call(
        functools.partial(_nsa_kernel, seq, nd),
        grid=(bsz, nq),
        in_specs=[
            pl.BlockSpec((TQ, 512), lambda b, i: (b * nq + i, 0)),
            pl.BlockSpec((TQ, LANES), lambda b, i: (b * nq + i, 0)),
            pl.BlockSpec((ncp, LANES), lambda b, i: (b, 0)),
            pl.BlockSpec((ncp, LANES), lambda b, i: (b, 0)),
            pl.BlockSpec((seq, 1024), lambda b, i: (b, 0)),
            pl.BlockSpec(bias.shape, lambda b, i: (0, 0, 0)),
        ],
        out_specs=pl.BlockSpec((TQ, 512), lambda b, i: (b * nq + i, 0)),
        out_shape=jax.ShapeDtypeStruct((bsz * seq, 512), BF16),
        scratch_shapes=[pltpu.VMEM((B_KV_HEADS, ns, TQ), F32), pltpu.VMEM((B_KV_HEADS, TQ, LANES), BF16)],
        compiler_params=_params(("parallel", "arbitrary")),
        name="nsa",
    )(bq, bg, kcmp, vcmp, bsw, bias)


def _dil_kernel(q_ref, kp_ref, kc_ref, vp_ref, vc_ref, bias_ref, o_ref, lse_ref):
    i = pl.program_id(2)
    q = q_ref[...]
    k2 = jnp.concatenate([kp_ref[...], kc_ref[...]], axis=0)
    v2 = jnp.concatenate([vp_ref[...], vc_ref[...]], axis=0)
    row = lax.broadcasted_iota(I32, (TQ, 2 * TK), 0)
    col = lax.broadcasted_iota(I32, (TQ, 2 * TK), 1)
    du = row + TK - col
    valid = (du >= 0) & (du <= TK) & ((col >= TK) | (i > 0))
    for hh in range(C_HEADS_PER_GROUP):
        hs = slice(hh * HEAD_DIM, (hh + 1) * HEAD_DIM)
        s = jnp.where(valid, _dot_t(q[:, hs], k2[:, hs]) + bias_ref[hh], -jnp.inf)
        m = jnp.max(s, axis=-1, keepdims=True)
        e = jnp.exp(s - m)
        den = jnp.sum(e, axis=-1, keepdims=True)
        o_ref[:, hs] = _dot(e.astype(BF16), v2[:, hs]) / den
        lse_ref[:, hs] = jnp.broadcast_to(m + jnp.log(den), (TQ, HEAD_DIM))


def _dilated_group(cq, ck, cv, bias, g, dil, bsz, seq):
    ln = seq // dil
    nq = ln // TQ
    width = C_HEADS * HEAD_DIM
    gw = C_HEADS_PER_GROUP * HEAD_DIM
    ncb = width // gw
    views = [a.reshape(bsz * ln, dil * width) for a in (cq, ck, cv)]

    def cur(b, r, i):
        return (b * nq + i, r * ncb + g)

    def prev(b, r, i):
        return (b * nq + jnp.maximum(i - 1, 0), r * ncb + g)

    blk = (TQ, gw)
    o, lse = pl.pallas_call(
        _dil_kernel,
        grid=(bsz, dil, nq),
        in_specs=[pl.BlockSpec(blk, cur), pl.BlockSpec(blk, prev), pl.BlockSpec(blk, cur),
                  pl.BlockSpec(blk, prev), pl.BlockSpec(blk, cur),
                  pl.BlockSpec(bias.shape, lambda b, r, i: (0, 0, 0))],
        out_specs=[pl.BlockSpec(blk, lambda b, r, i: (b * nq + i, r))] * 2,
        out_shape=[jax.ShapeDtypeStruct((bsz * ln, dil * gw), F32)] * 2,
        compiler_params=_params(("parallel", "parallel", "arbitrary")),
        name=f"dilated_d{dil}",
    )(views[0], views[1], views[1], views[2], views[2], bias)
    return o.reshape(bsz * seq, gw), lse.reshape(bsz * seq, gw)


def _merge_kernel(x_ref, ya_ref, yb_ref, o0_ref, l0_ref, o1_ref, l1_ref, o2_ref, l2_ref, g_ref,
                  wa_ref, wb_ref, wc_ref, wo_ref, out_ref):
    l0, l1, l2 = l0_ref[...], l1_ref[...], l2_ref[...]
    mx = jnp.maximum(jnp.maximum(l0, l1), l2)
    e0, e1, e2 = jnp.exp(l0 - mx), jnp.exp(l1 - mx), jnp.exp(l2 - mx)
    yc = (e0 * o0_ref[...] + e1 * o1_ref[...] + e2 * o2_ref[...]) / (e0 + e1 + e2)
    ya = _dot(ya_ref[...], wa_ref[...])
    yb = _dot(yb_ref[...], wb_ref[...])
    yc = _dot(yc.astype(BF16), wc_ref[...])
    d = D_MODEL
    z = g_ref[:, 0:d] * ya + g_ref[:, d:2 * d] * yb + g_ref[:, 2 * d:3 * d] * yc
    out_ref[...] = x_ref[...] + _dot(z.astype(BF16), wo_ref[...])


def _merge(x2d, ya, yb, c_outs, mixg, wa, wb, wc, wo, tm):
    m = x2d.shape[0]

    def rows(w):
        return pl.BlockSpec((tm, w), lambda i: (i, 0))

    def full(a):
        return pl.BlockSpec(a.shape, lambda i: (0, 0))

    c_flat = [a for pair in c_outs for a in pair]
    return pl.pallas_call(
        _merge_kernel,
        grid=(m // tm,),
        in_specs=[rows(D_MODEL), rows(512), rows(512)] + [rows(256)] * 6 + [rows(3 * D_MODEL)]
                 + [full(wa), full(wb), full(wc), full(wo)],
        out_specs=rows(D_MODEL),
        out_shape=jax.ShapeDtypeStruct((m, D_MODEL), F32),
        compiler_params=_params(("parallel",)),
        name="merge",
    )(x2d, ya, yb, *c_flat, mixg, wa, wb, wc, wo)


def _ffn_kernel(x_ref, g_ref, wg_ref, wu_ref, wd_ref, out_ref):
    x = x_ref[...]
    ms = jnp.mean(x * x, axis=-1, keepdims=True)
    h = (x * lax.rsqrt(ms + NORM_EPS) * g_ref[...]).astype(BF16)
    gate = _dot(h, wg_ref[...])
    up = _dot(h, wu_ref[...])
    act = gate / (1.0 + jnp.exp(-gate)) * up
    out_ref[...] = x + _dot(act.astype(BF16), wd_ref[...])


def _ffn(x2d, gain, w_in, w_out, tm):
    m = x2d.shape[0]
    wg = w_in[:, :D_FF].astype(BF16)
    wu = w_in[:, D_FF:].astype(BF16)
    wd = w_out.astype(BF16)

    def full(a):
        return pl.BlockSpec(a.shape, lambda i: (0, 0))

    return pl.pallas_call(
        _ffn_kernel,
        grid=(m // tm,),
        in_specs=[pl.BlockSpec((tm, D_MODEL), lambda i: (i, 0)), pl.BlockSpec((1, D_MODEL), lambda i: (0, 0)),
                  full(wg), full(wu), full(wd)],
        out_specs=pl.BlockSpec((tm, D_MODEL), lambda i: (i, 0)),
        out_shape=jax.ShapeDtypeStruct((m, D_MODEL), F32),
        compiler_params=_params(("parallel",)),
        name="ffn",
    )(x2d, gain.reshape(1, D_MODEL).astype(F32), wg, wu, wd)


def _layer(x2d, bsz, seq, norm1_g, norm2_g, w_in, qk, cmp_pos, cmp_w, w_a, w_b, w_c, w_out, w_ffn_in, w_ffn_out,
           bias_a, bias_b, nd, bias_c):
    pieces_ab, pieces_c, pieces_g = _proj_pieces(w_in, qk)
    aq, akv, iq, ikw, bq, bcmp, bsw, bg = _proj(x2d, norm1_g, pieces_ab, 512)
    cq, ck, cv = _proj(x2d, norm1_g, pieces_c, 512)
    (mixg,) = _proj(x2d, norm1_g, pieces_g, 512)

    ya = _dsa(iq, ikw, aq, akv, bias_a, nd, bsz, seq)
    kcmp, vcmp = _nsa_cmp(bcmp, cmp_pos, cmp_w, qk[3], bsz, seq)
    yb = _nsa(bq, bg, kcmp, vcmp, bsw, bias_b, nd, bsz, seq)
    c_outs = [_dilated_group(cq, ck, cv, bias_c[g], g, dil, bsz, seq) for g, (_, dil) in enumerate(C_GROUPS)]

    x1 = _merge(x2d, ya, yb, c_outs, mixg, w_a.astype(BF16), w_b.astype(BF16), w_c.astype(BF16),
                w_out.astype(BF16), 256)
    return _ffn(x1, norm2_g, w_ffn_in, w_ffn_out, 256)


def kernel(x, norm1_g, norm2_g, w_in, qk_norm_g, nsa_cmp_pos, nsa_cmp_w, w_branch_a, w_branch_b, w_branch_c, w_out, w_ffn_in, w_ffn_out, rel_bias):
    bsz, seq, d = x.shape
    assert d == D_MODEL and seq % (TQ * max(dil for _, dil in C_GROUPS)) == 0
    for win, dil in C_GROUPS:
        assert win == TK * dil
    bias_a, nd = _toeplitz_bias(rel_bias[:, :A_HEADS], seq)
    bias_b, _ = _toeplitz_bias(rel_bias[:, A_HEADS:A_HEADS + B_HEADS], seq)
    rel_c = rel_bias[:, A_HEADS + B_HEADS:]
    bias_c = [_dilated_bias(rel_c[:, g * C_HEADS_PER_GROUP:(g + 1) * C_HEADS_PER_GROUP], dil)
              for g, (_, dil) in enumerate(C_GROUPS)]
    x2d = x.reshape(bsz * seq, d)
    for layer in range(norm1_g.shape[0]):
        x2d = _layer(x2d, bsz, seq, norm1_g[layer], norm2_g[layer], w_in[layer], qk_norm_g[layer],
                     nsa_cmp_pos[layer], nsa_cmp_w[layer], w_branch_a[layer], w_branch_b[layer],
                     w_branch_c[layer], w_out[layer], w_ffn_in[layer], w_ffn_out[layer],
                     bias_a, bias_b, nd, bias_c)
    return x2d.reshape(bsz, seq, d)
```

```python
import functools
import math

import numpy as np
import jax
import jax.numpy as jnp
from jax import lax
from jax.experimental import pallas as pl
from jax.experimental.pallas import tpu as pltpu

F32 = jnp.float32
BF16 = jnp.bfloat16
I32 = jnp.int32

D_MODEL = 1024
HEAD_DIM = 64
NORM_EPS = 1e-6
REL_BUCKETS = 32
REL_MAX_DIST = 2048

A_HEADS = 8
A_IDX_HEADS = 4
A_TOPK_MAX = 256
B_HEADS = 8
B_KV_HEADS = 2
B_GROUP = B_HEADS // B_KV_HEADS
B_CMP_LEN = 32
B_CMP_STRIDE = 16
B_SEL_LEN = 64
B_SEL_TOPK_MAX = 16
B_WINDOW = 512
C_GROUPS = ((128, 1), (512, 4), (2048, 16))
C_HEADS_PER_GROUP = 4
C_HEADS = C_HEADS_PER_GROUP * len(C_GROUPS)
D_FF = ((8 * D_MODEL + 3 * 256 - 1) // (3 * 256)) * 256

_O_AQ = 0
_O_AK = _O_AQ + A_HEADS * HEAD_DIM
_O_AV = _O_AK + HEAD_DIM
_O_IQ = _O_AV + HEAD_DIM
_O_IK = _O_IQ + A_IDX_HEADS * HEAD_DIM
_O_IW = _O_IK + HEAD_DIM
_O_BQ = _O_IW + A_IDX_HEADS
_O_BKV = _O_BQ + B_HEADS * HEAD_DIM
_O_BG = _O_BKV + 6 * B_KV_HEADS * HEAD_DIM
_O_CQ = _O_BG + 3 * B_HEADS
_O_CK = _O_CQ + C_HEADS * HEAD_DIM
_O_CV = _O_CK + C_HEADS * HEAD_DIM
_O_MIX = _O_CV + C_HEADS * HEAD_DIM
_O_END = _O_MIX + 3 * D_MODEL

TQ = 128
TK = 128
KB_TILES = 4
LANES = 128
VMEM_LIMIT = 56 * 1024 * 1024
INT_MIN = -2 ** 31
NEG_INIT = -1e30
SCALE = HEAD_DIM ** -0.5


def _params(sem):
    return pltpu.CompilerParams(dimension_semantics=sem, vmem_limit_bytes=VMEM_LIMIT)


def _dot_t(a, b):
    return lax.dot_general(a, b, (((1,), (1,)), ((), ())), preferred_element_type=F32)


def _dot(a, b):
    return jnp.dot(a, b, preferred_element_type=F32)


def _split_dot(a, b_bf16):
    hi = a.astype(BF16)
    lo = (a - hi.astype(F32)).astype(BF16)
    return _dot(hi, b_bf16) + _dot(lo, b_bf16)


def _bucket_table(n_max):
    n = np.arange(n_max, dtype=np.int64)
    exact = REL_BUCKETS // 2
    nf = np.maximum(n, 1).astype(np.float32)
    large = exact + (np.log(nf / np.float32(exact)) / np.float32(math.log(REL_MAX_DIST / exact))
                     * np.float32(REL_BUCKETS - exact)).astype(np.int32)
    return np.where(n < exact, n, np.minimum(large, REL_BUCKETS - 1)).astype(np.int32)


def _num_bias_tiles(seq):
    bucket = _bucket_table(seq + TQ)
    first_sat = int(np.min(np.nonzero(bucket == REL_BUCKETS - 1)[0]))
    assert np.all(bucket[first_sat:] == REL_BUCKETS - 1)
    nd = -(-(first_sat + TK - 1) // TQ) + 1
    return min(nd, seq // TQ)


def _bias_kernel(nh, idx_ref, rel_ref, o_ref):
    idx = idx_ref[0]
    acc = [jnp.zeros(idx.shape, F32) for _ in range(nh)]
    for b in range(REL_BUCKETS):
        hit = idx == b
        for h in range(nh):
            acc[h] = jnp.where(hit, rel_ref[b, h], acc[h])
    for h in range(nh):
        o_ref[h] = acc[h]


def _bias_tiles(rel_cols, idx):
    n, r, c = idx.shape
    nh = rel_cols.shape[1]
    return pl.pallas_call(
        functools.partial(_bias_kernel, nh),
        grid=(n,),
        in_specs=[pl.BlockSpec((1, r, c), lambda k: (k, 0, 0)),
                  pl.BlockSpec(memory_space=pltpu.SMEM)],
        out_specs=pl.BlockSpec((nh, r, c), lambda k: (k, 0, 0)),
        out_shape=jax.ShapeDtypeStruct((n * nh, r, c), F32),
        compiler_params=_params(("parallel",)),
        name="bias_tiles",
    )(jnp.asarray(idx, I32), rel_cols.astype(F32))


def _toeplitz_bias(rel_cols, seq):
    nd = _num_bias_tiles(seq)
    bucket = _bucket_table(seq + TQ)
    d = np.arange(nd)[:, None, None] * TQ + np.arange(TQ)[None, :, None] - np.arange(TK)[None, None, :]
    return _bias_tiles(rel_cols, bucket[np.clip(d, 0, None)]), nd


def _dilated_bias(rel_cols, dil):
    bucket = _bucket_table(2 * TK * dil + 1)
    du = np.arange(TQ)[:, None] + TK - np.arange(2 * TK)[None, :]
    return _bias_tiles(rel_cols, bucket[np.clip(du, 0, None) * dil][None])


def _proj_kernel(kinds, *refs):
    n = len(kinds)
    x_ref, g_ref, gs_ref = refs[0], refs[1], refs[2]
    w_refs = refs[3:3 + 2 * n:2]
    aux_refs = refs[4:4 + 2 * n:2]
    out_refs = refs[3 + 2 * n:]
    x = x_ref[...]
    ms = jnp.mean(x * x, axis=-1, keepdims=True)
    h = (x * lax.rsqrt(ms + NORM_EPS) * g_ref[...]).astype(BF16)
    for kind, w_ref, aux_ref, o_ref in zip(kinds, w_refs, aux_refs, out_refs):
        width = w_ref.shape[1]
        cw = 256 if width % 256 == 0 else LANES
        for c0 in range(0, width, cw):
            y = _dot(h, w_ref[:, c0:c0 + cw])
            if kind == "norm":
                gsum = _split_dot(y * y, gs_ref[:cw, :cw])
                r = lax.rsqrt(gsum * (1.0 / HEAD_DIM) + NORM_EPS)
                mask = aux_ref[0:1, c0:c0 + cw]
                fac = mask * (r * aux_ref[1:2, c0:c0 + cw]) + (1.0 - mask)
                y = y * fac + aux_ref[2:3, c0:c0 + cw]
            elif kind == "sigmoid":
                y = 1.0 / (1.0 + jnp.exp(-y))
            o_ref[:, c0:c0 + cw] = y.astype(o_ref.dtype)


def _proj(x2d, gain, pieces, tm):
    m, d = x2d.shape
    kinds = tuple(p[0] for p in pieces)
    gs = (np.arange(256)[:, None] // HEAD_DIM == np.arange(256)[None, :] // HEAD_DIM)
    gs = jnp.asarray(gs, BF16)
    in_specs = [pl.BlockSpec((tm, d), lambda i: (i, 0)),
                pl.BlockSpec((1, d), lambda i: (0, 0)),
                pl.BlockSpec((256, 256), lambda i: (0, 0))]
    args = [x2d, gain.reshape(1, d).astype(F32), gs]
    out_specs, out_shapes = [], []
    for _, w, aux, dt in pieces:
        nw = w.shape[1]
        in_specs += [pl.BlockSpec((d, nw), lambda i: (0, 0)), pl.BlockSpec((3, nw), lambda i: (0, 0))]
        args += [w, aux]
        out_specs.append(pl.BlockSpec((tm, nw), lambda i: (i, 0)))
        out_shapes.append(jax.ShapeDtypeStruct((m, nw), dt))
    return pl.pallas_call(
        functools.partial(_proj_kernel, kinds),
        grid=(m // tm,),
        in_specs=in_specs, out_specs=out_specs, out_shape=out_shapes,
        compiler_params=_params(("parallel",)),
        name="proj",
    )(*args)


def _aux(width, mask=None, gain=None, add=None):
    z = jnp.zeros((width,), F32)
    return jnp.stack([z if mask is None else mask, z if gain is None else gain, z if add is None else add])


def _seg(*parts):
    ref = next(p for p in parts if not isinstance(p, int))
    return jnp.concatenate([jnp.zeros(ref.shape[:-1] + (p,), ref.dtype) if isinstance(p, int) else p
                            for p in parts], axis=-1)


def _proj_pieces(w_in, qk):
    w = w_in.astype(BF16)
    hd = HEAD_DIM
    ones, zeros = jnp.ones((hd,), F32), jnp.zeros((hd,), F32)
    cat = jnp.concatenate

    def cols(a, b):
        return w[:, a:b]

    pieces_ab = [
        ("norm", cols(_O_AQ, _O_AK), _aux(512, jnp.ones((512,), F32), jnp.tile(qk[0], A_HEADS) * SCALE), BF16),
        ("norm", _seg(cols(_O_AK, _O_AV), hd, cols(_O_AV, _O_IQ), hd),
         _aux(256, cat([ones, zeros, zeros, zeros]), cat([qk[1], zeros, zeros, zeros]),
              cat([zeros, zeros, zeros, ones])), BF16),
        ("plain", cols(_O_IQ, _O_IK), _aux(256), BF16),
        ("plain", _seg(cols(_O_IK, _O_IW), LANES - hd), _aux(LANES), BF16),
        ("plain", _seg(cols(_O_IW, _O_BQ), LANES - A_IDX_HEADS), _aux(LANES), F32),
        ("norm", cols(_O_BQ, _O_BKV), _aux(512, jnp.ones((512,), F32), jnp.tile(qk[2], B_HEADS) * SCALE), BF16),
        ("plain", cols(_O_BKV, _O_BKV + 256), _aux(256), F32),
    ]
    o = _O_BKV + 256
    ks0, ks1, vs0, vs1, kw0, kw1, vw0, vw1 = [cols(o + i * hd, o + (i + 1) * hd) for i in range(8)]
    kmask = cat([ones, zeros] * 4 + [zeros, zeros] * 4)
    kgain = cat([qk[3], zeros] * 4 + [zeros, zeros] * 4)
    vadd = cat([zeros, zeros] * 4 + [zeros, ones] * 4)
    pieces_ab += [
        ("norm", _seg(ks0, hd, ks1, hd, kw0, hd, kw1, hd, vs0, hd, vs1, hd, vw0, hd, vw1, hd),
         _aux(1024, kmask, kgain, vadd), BF16),
        ("sigmoid", _seg(cols(_O_BG, _O_CQ), LANES - 3 * B_HEADS), _aux(LANES), F32),
    ]
    pieces_c = [
        ("norm", cols(_O_CQ, _O_CK), _aux(768, jnp.ones((768,), F32), jnp.tile(qk[4], C_HEADS) * SCALE), BF16),
        ("norm", cols(_O_CK, _O_CV), _aux(768, jnp.ones((768,), F32), jnp.tile(qk[5], C_HEADS)), BF16),
        ("plain", cols(_O_CV, _O_MIX), _aux(768), BF16),
    ]
    pieces_g = [("sigmoid", cols(_O_MIX, _O_END), _aux(3 * D_MODEL), F32)]
    return pieces_ab, pieces_c, pieces_g


def _flash_init(m_ref, acc_ref):
    m_ref[...] = jnp.full(m_ref.shape, NEG_INIT, F32)
    acc_ref[...] = jnp.zeros(acc_ref.shape, F32)


def _flash_block(q_ref, heads, group_of, kt, v_aug, bias_fn, masks, m_ref, acc_ref):
    heads = list(heads)
    s_all = [_dot_t(q_ref[h], kt[group_of(h)]) for h in heads]
    m_old = [m_ref[h] for h in heads]
    ps, alphas = [], []
    for k, h in enumerate(heads):
        s, mk = s_all[k], masks[group_of(h)]
        sc = [jnp.where(mk[c], s[:, c * TK:(c + 1) * TK] + bias_fn(h, c), -jnp.inf) for c in range(len(mk))]
        m_new = jnp.maximum(m_old[k], jnp.max(functools.reduce(jnp.maximum, sc), axis=-1, keepdims=True))
        ps.append(jnp.concatenate([jnp.exp(x - m_new).astype(BF16) for x in sc], axis=1))
        alphas.append(jnp.exp(m_old[k] - m_new))
        m_ref[h] = m_new
    for k, h in enumerate(heads):
        acc_ref[h] = alphas[k] * acc_ref[h] + _dot(ps[k], v_aug[group_of(h)])


def _flash_out(acc):
    return acc[:, :HEAD_DIM] / jnp.maximum(acc[:, HEAD_DIM:HEAD_DIM + 1], 1e-30)


def _split_heads(q_ref, qs_ref, nh):
    for h in range(nh):
        qs_ref[h] = q_ref[:, h * HEAD_DIM:(h + 1) * HEAD_DIM]


def _dsa_kernel(topk, nd, pbits, iq_ref, iw_ref, ik_ref, aq_ref, akv_ref, bias_ref, o_ref,
                keys_ref, qs_ref, m_ref, acc_ref):
    i = pl.program_id(1)
    nk = i + 1
    row = lax.broadcasted_iota(I32, (TQ, TK), 0)
    col = lax.broadcasted_iota(I32, (TQ, TK), 1)
    tpos = i * TQ + row

    nkb = (nk + KB_TILES - 1) // KB_TILES
    kb = KB_TILES * TK

    iw = iw_ref[...]
    wb = [jnp.broadcast_to(iw[:, h:h + 1], (TQ, kb)) for h in range(A_IDX_HEADS)]
    iqs = [iq_ref[:, h * HEAD_DIM:(h + 1) * HEAD_DIM] for h in range(A_IDX_HEADS)]

    def score_block(jb, c):
        r0 = pl.multiple_of(jb * kb, kb)
        ik_t = ik_ref[pl.ds(r0, kb), 0:HEAD_DIM]
        acc = jnp.zeros((TQ, kb), F32)
        for h in range(A_IDX_HEADS):
            acc = acc + jnp.maximum(_dot_t(iqs[h], ik_t), 0.0) * wb[h]
        acc = jnp.where(acc == 0.0, 0.0, acc)
        bits = lax.bitcast_convert_type(acc, I32)
        key = jnp.where(bits < 0, bits ^ 0x7FFFFFFF, bits)
        for t in range(KB_TILES):
            j = jb * KB_TILES + t
            keys_ref[j] = jnp.where(j * TK + col <= tpos, key[:, t * TK:(t + 1) * TK], INT_MIN)
        return c

    lax.fori_loop(0, nkb, score_block, 0)

    def count(pred):
        def body(j, acc):
            return acc + jnp.where(pred(j, keys_ref[j]), 1.0, 0.0)
        acc = lax.fori_loop(0, nk, body, jnp.zeros((TQ, TK), F32))
        return jnp.broadcast_to(jnp.sum(acc, axis=-1, keepdims=True), (TQ, TK))

    kf = float(topk)

    def thr_step(b, t):
        cand = t ^ jnp.left_shift(jnp.int32(1), 31 - b)
        return jnp.where(count(lambda j, k: k >= cand) >= kf, cand, t)

    thr = lax.fori_loop(0, 32, thr_step, jnp.full((TQ, TK), INT_MIN, I32))

    n_ge = count(lambda j, k: k >= thr)

    @pl.when(jnp.max(n_ge) > kf)
    def _():
        need = kf - count(lambda j, k: k > thr)

        def tie_step(b, p):
            cand = p | jnp.left_shift(jnp.int32(1), pbits - 1 - b)
            cnt = count(lambda j, k: (k == thr) & (j * TK + col < cand))
            return jnp.where(cnt < need, cand, p)

        plim = lax.fori_loop(0, pbits, tie_step, jnp.zeros((TQ, TK), I32))
        lose = (n_ge > kf) & (thr > INT_MIN)

        def demote(j, c):
            k = keys_ref[j]
            keys_ref[j] = jnp.where(lose & (k == thr) & (j * TK + col > plim), k - 1, k)
            return c

        lax.fori_loop(0, nk, demote, 0)

    sel_thr = jnp.maximum(thr, INT_MIN + 1)

    _split_heads(aq_ref, qs_ref, A_HEADS)
    _flash_init(m_ref, acc_ref)

    def att_block(jb, c):
        r0 = pl.multiple_of(jb * kb, kb)
        kt = akv_ref[pl.ds(r0, kb), 0:HEAD_DIM]
        va = akv_ref[pl.ds(r0, kb), LANES:2 * LANES]
        j0 = jb * KB_TILES
        masks = [keys_ref[j0 + t] >= sel_thr for t in range(KB_TILES)]
        d = [jnp.clip(i - j0 - t, 0, nd - 1) * A_HEADS for t in range(KB_TILES)]
        _flash_block(qs_ref, range(A_HEADS), lambda h: 0, [kt], [va], lambda h, t: bias_ref[d[t] + h], [masks],
                     m_ref, acc_ref)
        return c

    lax.fori_loop(0, nkb, att_block, 0)
    o_ref[...] = jnp.concatenate([_flash_out(acc_ref[h]) for h in range(A_HEADS)], axis=1).astype(o_ref.dtype)


def _dsa(iq, iw, ik, aq, akv, bias, nd, bsz, seq):
    nq = seq // TQ
    topk = min(A_TOPK_MAX, seq // 4)
    pbits = max(1, (seq - 1).bit_length())
    return pl.pallas_call(
        functools.partial(_dsa_kernel, topk, nd, pbits),
        grid=(bsz, nq),
        in_specs=[
            pl.BlockSpec((TQ, 256), lambda b, i: (b * nq + i, 0)),
            pl.BlockSpec((TQ, LANES), lambda b, i: (b * nq + i, 0)),
            pl.BlockSpec((seq, LANES), lambda b, i: (b, 0)),
            pl.BlockSpec((TQ, 512), lambda b, i: (b * nq + i, 0)),
            pl.BlockSpec((seq, 256), lambda b, i: (b, 0)),
            pl.BlockSpec(bias.shape, lambda b, i: (0, 0, 0)),
        ],
        out_specs=pl.BlockSpec((TQ, 512), lambda b, i: (b * nq + i, 0)),
        out_shape=jax.ShapeDtypeStruct((bsz * seq, 512), BF16),
        scratch_shapes=[pltpu.VMEM((seq // TK, TQ, TK), I32),
                        pltpu.VMEM((A_HEADS, TQ, HEAD_DIM), BF16),
                        pltpu.VMEM((A_HEADS, TQ, LANES), F32),
                        pltpu.VMEM((A_HEADS, TQ, LANES), F32)],
        compiler_params=_params(("parallel", "arbitrary")),
        name="dsa",
    )(iq, iw, ik, aq, akv, bias)


def _nsa_cmp_kernel(x_ref, pos_ref, wlo_ref, whi_ref, gain_ref, k_ref, v_ref):
    x = x_ref[...]
    lo = _dot((x + pos_ref[0:1, :]).astype(BF16), wlo_ref[...])
    hi = _dot((x + pos_ref[1:2, :]).astype(BF16), whi_ref[...])
    nrow = x.shape[0]
    pre = lo + pltpu.roll(hi, nrow - 1, 0)
    ks = []
    for g in range(B_KV_HEADS):
        kg = pre[:, g * HEAD_DIM:(g + 1) * HEAD_DIM]
        ms = jnp.mean(kg * kg, axis=-1, keepdims=True)
        ks.append(kg * lax.rsqrt(ms + NORM_EPS) * gain_ref[...])
    k_ref[...] = jnp.concatenate(ks, axis=1).astype(k_ref.dtype)
    v_ref[...] = pre[:, LANES:2 * LANES].astype(v_ref.dtype)


def _nsa_cmp(bcmp, cmp_pos, cmp_w, k_gain, bsz, seq):
    nch = seq // B_CMP_STRIDE
    half = B_CMP_LEN // 2
    width = half * 256
    x = bcmp.reshape(bsz * nch, width)

    def wmat(l0):
        w = jnp.zeros((half, 4, HEAD_DIM, 4, HEAD_DIM), F32)
        for j in range(4):
            w = w.at[:, j, :, j, :].set(cmp_w[j // 2, l0:l0 + half])
        return w.reshape(width, 256).astype(BF16)

    def prow(l0):
        p = jnp.stack([cmp_pos[0, l0:l0 + half], cmp_pos[0, l0:l0 + half],
                       cmp_pos[1, l0:l0 + half], cmp_pos[1, l0:l0 + half]], axis=1)
        return p.reshape(width)

    pos = jnp.stack([prow(0), prow(half)]).astype(F32)
    return pl.pallas_call(
        _nsa_cmp_kernel,
        grid=(bsz,),
        in_specs=[
            pl.BlockSpec((nch, width), lambda b: (b, 0)),
            pl.BlockSpec((2, width), lambda b: (0, 0)),
            pl.BlockSpec((width, 256), lambda b: (0, 0)),
            pl.BlockSpec((width, 256), lambda b: (0, 0)),
            pl.BlockSpec((1, HEAD_DIM), lambda b: (0, 0)),
        ],
        out_specs=[pl.BlockSpec((nch, LANES), lambda b: (b, 0)), pl.BlockSpec((nch, LANES), lambda b: (b, 0))],
        out_shape=[jax.ShapeDtypeStruct((bsz * nch, LANES), BF16)] * 2,
        compiler_params=_params(("parallel",)),
        name="nsa_cmp",
    )(x, pos, wmat(0), wmat(half), k_gain.reshape(1, HEAD_DIM).astype(F32))


def _nsa_kernel(seq, nd, q_ref, g_ref, kc_ref, vc_ref, kv_ref, bias_ref, o_ref,
                imp_ref, sel_ref, qs_ref, m_ref, acc_ref):
    i = pl.program_id(1)
    ncp = seq // B_CMP_STRIDE
    ns = seq // B_SEL_LEN
    n_top = min(B_SEL_TOPK_MAX, ns)
    hg = B_GROUP
    row = lax.broadcasted_iota(I32, (TQ, TK), 0)
    col = lax.broadcasted_iota(I32, (TQ, TK), 1)
    tpos = i * TQ + row
    _split_heads(q_ref, qs_ref, B_HEADS)

    n_idx = lax.broadcasted_iota(I32, (ncp, TQ), 0)
    t_c = i * TQ + lax.broadcasted_iota(I32, (ncp, TQ), 1)
    cmask = n_idx * B_CMP_STRIDE + (B_CMP_LEN - 1) <= t_c
    om = lax.broadcasted_iota(I32, (ns, ncp), 0) * B_SEL_LEN
    on = lax.broadcasted_iota(I32, (ns, ncp), 1) * B_CMP_STRIDE
    ovt = jnp.where((on < om + B_SEL_LEN) & (on + B_CMP_LEN > om), 1.0, 0.0).astype(BF16)
    m_idx = lax.broadcasted_iota(I32, (ns, TQ), 0)
    jt = (i * TQ + lax.broadcasted_iota(I32, (ns, TQ), 1)) // B_SEL_LEN
    forced = (m_idx == 0) | (m_idx == jt) | (m_idx == jt - 1)

    kc = [kc_ref[:, g * HEAD_DIM:(g + 1) * HEAD_DIM] for g in range(B_KV_HEADS)]
    vc = [vc_ref[:, g * HEAD_DIM:(g + 1) * HEAD_DIM] for g in range(B_KV_HEADS)]
    st_all = [_dot_t(kc[h // hg], qs_ref[h]) for h in range(B_HEADS)]
    pts = []
    for h in range(B_HEADS):
        st = jnp.where(cmask, st_all[h], -jnp.inf)
        mx = jnp.max(st, axis=0, keepdims=True)
        mx = jnp.where(mx == -jnp.inf, 0.0, mx)
        e = jnp.exp(st - mx)
        pts.append(e / jnp.maximum(jnp.sum(e, axis=0, keepdims=True), 1e-30))
    oc = [_dot(pts[h].T.astype(BF16), vc[h // hg]) for h in range(B_HEADS)]

    for g in range(B_KV_HEADS):
        psum = functools.reduce(lambda a, b: a + b, pts[g * hg:(g + 1) * hg])
        hi = psum.astype(BF16)
        lo = (psum - hi.astype(F32)).astype(BF16)
        imp = _dot(ovt, hi) + _dot(ovt, lo)
        imp = jnp.where(forced, jnp.inf, jnp.where(m_idx <= jt, imp, -jnp.inf))
        imp_ref[g] = imp

        def rank_step(mp, rank, g=g, imp=imp):
            vp = jnp.broadcast_to(imp_ref[g, pl.ds(mp, 1), :], (ns, TQ))
            before = (vp > imp) | ((vp == imp) & (mp < m_idx))
            return rank + jnp.where(before, 1.0, 0.0)

        rank = lax.fori_loop(0, ns, rank_step, jnp.zeros((ns, TQ), F32), unroll=8)
        selt = jnp.where((rank < float(n_top)) & (m_idx <= jt), 1.0, 0.0)
        selt = jnp.concatenate([selt, jnp.zeros((LANES - ns, TQ), F32)], axis=0)
        sel_ref[g] = selt.T.astype(BF16)

    def branch_block(j0, ntiles, koff, voff, mask_fn):
        r0 = pl.multiple_of(j0 * TK, TK)
        d = [jnp.clip(i - j0 - t, 0, nd - 1) * B_HEADS for t in range(ntiles)]
        rows = pl.ds(r0, ntiles * TK)
        kt = [kv_ref[rows, koff + g * LANES:koff + g * LANES + HEAD_DIM] for g in range(B_KV_HEADS)]
        va = [kv_ref[rows, voff + g * LANES:voff + (g + 1) * LANES] for g in range(B_KV_HEADS)]
        masks = [mask_fn(g, j0) for g in range(B_KV_HEADS)]
        _flash_block(qs_ref, range(B_HEADS), lambda h: h // hg, kt, va, lambda h, t: bias_ref[d[t] + h],
                     masks, m_ref, acc_ref)

    kb = KB_TILES * TK
    e_m = lax.broadcasted_iota(I32, (LANES, kb), 0)
    e_c = lax.broadcasted_iota(I32, (LANES, kb), 1) // B_SEL_LEN

    def sel_masks(g, j0):
        expand = jnp.where(e_m == j0 * (TK // B_SEL_LEN) + e_c, 1.0, 0.0).astype(BF16)
        chosen = _dot(sel_ref[g], expand) > 0.5
        return [chosen[:, t * TK:(t + 1) * TK] & ((j0 + t) * TK + col <= tpos) for t in range(KB_TILES)]

    def sel_body(jb, c):
        branch_block(jb * KB_TILES, KB_TILES, 0, 4 * LANES, sel_masks)
        return c

    gates = g_ref[...]

    def gate(h, br):
        return gates[:, 3 * h + br:3 * h + br + 1]

    _flash_init(m_ref, acc_ref)
    lax.fori_loop(0, (i + KB_TILES) // KB_TILES, sel_body, 0)
    part = [gate(h, 0) * oc[h] + gate(h, 1) * _flash_out(acc_ref[h]) for h in range(B_HEADS)]

    wt = B_WINDOW // TK + 1
    w0 = jnp.maximum(i + 1 - wt, 0)

    def win_masks(g, j0):
        out = []
        for t in range(wt):
            dist = tpos - ((j0 + t) * TK + col)
            out.append((dist >= 0) & (dist < B_WINDOW))
        return out

    _flash_init(m_ref, acc_ref)
    branch_block(w0, wt, 2 * LANES, 6 * LANES, win_masks)
    outs = [part[h] + gate(h, 2) * _flash_out(acc_ref[h]) for h in range(B_HEADS)]
    o_ref[...] = jnp.concatenate(outs, axis=1).astype(o_ref.dtype)


def _nsa(bq, bg, kcmp, vcmp, bsw, bias, nd, bsz, seq):
    nq = seq // TQ
    ncp = seq // B_CMP_STRIDE
    ns = seq // B_SEL_LEN
    return pl.pallas_call(
        functools.partial(_nsa_kernel, seq, nd),
        grid=(bsz, nq),
        in_specs=[
            pl.BlockSpec((TQ, 512), lambda b, i: (b * nq + i, 0)),
            pl.BlockSpec((TQ, LANES), lambda b, i: (b * nq + i, 0)),
            pl.BlockSpec((ncp, LANES), lambda b, i: (b, 0)),
            pl.BlockSpec((ncp, LANES), lambda b, i: (b, 0)),
            pl.BlockSpec((seq, 1024), lambda b, i: (b, 0)),
            pl.BlockSpec(bias.shape, lambda b, i: (0, 0, 0)),
        ],
        out_specs=pl.BlockSpec((TQ, 512), lambda b, i: (b * nq + i, 0)),
        out_shape=jax.ShapeDtypeStruct((bsz * seq, 512), BF16),
        scratch_shapes=[pltpu.VMEM((B_KV_HEADS, ns, TQ), F32), pltpu.VMEM((B_KV_HEADS, TQ, LANES), BF16),
                        pltpu.VMEM((B_HEADS, TQ, HEAD_DIM), BF16),
                        pltpu.VMEM((B_HEADS, TQ, LANES), F32),
                        pltpu.VMEM((B_HEADS, TQ, LANES), F32)],
        compiler_params=_params(("parallel", "arbitrary")),
        name="nsa",
    )(bq, bg, kcmp, vcmp, bsw, bias)


def _dil_kernel(q_ref, kp_ref, kc_ref, vp_ref, vc_ref, bias_ref, o_ref, lse_ref):
    i = pl.program_id(2)
    q = q_ref[...]
    k2 = jnp.concatenate([kp_ref[...], kc_ref[...]], axis=0)
    v2 = jnp.concatenate([vp_ref[...], vc_ref[...]], axis=0)
    row = lax.broadcasted_iota(I32, (TQ, 2 * TK), 0)
    col = lax.broadcasted_iota(I32, (TQ, 2 * TK), 1)
    du = row + TK - col
    valid = (du >= 0) & (du <= TK) & ((col >= TK) | (i > 0))
    for hh in range(C_HEADS_PER_GROUP):
        hs = slice(hh * HEAD_DIM, (hh + 1) * HEAD_DIM)
        s = jnp.where(valid, _dot_t(q[:, hs], k2[:, hs]) + bias_ref[hh], -jnp.inf)
        m = jnp.max(s, axis=-1, keepdims=True)
        e = jnp.exp(s - m)
        den = jnp.sum(e, axis=-1, keepdims=True)
        o_ref[:, hs] = _dot(e.astype(BF16), v2[:, hs]) / den
        lse_ref[:, hs] = jnp.broadcast_to(m + jnp.log(den), (TQ, HEAD_DIM))


def _dilated_group(cq, ck, cv, bias, g, dil, bsz, seq):
    ln = seq // dil
    nq = ln // TQ
    width = C_HEADS * HEAD_DIM
    gw = C_HEADS_PER_GROUP * HEAD_DIM
    ncb = width // gw
    views = [a.reshape(bsz * ln, dil * width) for a in (cq, ck, cv)]

    def cur(b, r, i):
        return (b * nq + i, r * ncb + g)

    def prev(b, r, i):
        return (b * nq + jnp.maximum(i - 1, 0), r * ncb + g)

    blk = (TQ, gw)
    o, lse = pl.pallas_call(
        _dil_kernel,
        grid=(bsz, dil, nq),
        in_specs=[pl.BlockSpec(blk, cur), pl.BlockSpec(blk, prev), pl.BlockSpec(blk, cur),
                  pl.BlockSpec(blk, prev), pl.BlockSpec(blk, cur),
                  pl.BlockSpec(bias.shape, lambda b, r, i: (0, 0, 0))],
        out_specs=[pl.BlockSpec(blk, lambda b, r, i: (b * nq + i, r))] * 2,
        out_shape=[jax.ShapeDtypeStruct((bsz * ln, dil * gw), F32)] * 2,
        compiler_params=_params(("parallel", "parallel", "arbitrary")),
        name=f"dilated_d{dil}",
    )(views[0], views[1], views[1], views[2], views[2], bias)
    return o.reshape(bsz * seq, gw), lse.reshape(bsz * seq, gw)


def _merge_kernel(x_ref, ya_ref, yb_ref, o0_ref, l0_ref, o1_ref, l1_ref, o2_ref, l2_ref, g_ref,
                  wa_ref, wb_ref, wc_ref, wo_ref, out_ref):
    l0, l1, l2 = l0_ref[...], l1_ref[...], l2_ref[...]
    mx = jnp.maximum(jnp.maximum(l0, l1), l2)
    e0, e1, e2 = jnp.exp(l0 - mx), jnp.exp(l1 - mx), jnp.exp(l2 - mx)
    yc = (e0 * o0_ref[...] + e1 * o1_ref[...] + e2 * o2_ref[...]) / (e0 + e1 + e2)
    ya = _dot(ya_ref[...], wa_ref[...])
    yb = _dot(yb_ref[...], wb_ref[...])
    yc = _dot(yc.astype(BF16), wc_ref[...])
    d = D_MODEL
    z = g_ref[:, 0:d] * ya + g_ref[:, d:2 * d] * yb + g_ref[:, 2 * d:3 * d] * yc
    out_ref[...] = x_ref[...] + _dot(z.astype(BF16), wo_ref[...])


def _merge(x2d, ya, yb, c_outs, mixg, wa, wb, wc, wo, tm):
    m = x2d.shape[0]

    def rows(w):
        return pl.BlockSpec((tm, w), lambda i: (i, 0))

    def full(a):
        return pl.BlockSpec(a.shape, lambda i: (0, 0))

    c_flat = [a for pair in c_outs for a in pair]
    return pl.pallas_call(
        _merge_kernel,
        grid=(m // tm,),
        in_specs=[rows(D_MODEL), rows(512), rows(512)] + [rows(256)] * 6 + [rows(3 * D_MODEL)]
                 + [full(wa), full(wb), full(wc), full(wo)],
        out_specs=rows(D_MODEL),
        out_shape=jax.ShapeDtypeStruct((m, D_MODEL), F32),
        compiler_params=_params(("parallel",)),
        name="merge",
    )(x2d, ya, yb, *c_flat, mixg, wa, wb, wc, wo)


def _ffn_kernel(x_ref, g_ref, wg_ref, wu_ref, wd_ref, out_ref):
    x = x_ref[...]
    ms = jnp.mean(x * x, axis=-1, keepdims=True)
    h = (x * lax.rsqrt(ms + NORM_EPS) * g_ref[...]).astype(BF16)
    gate = _dot(h, wg_ref[...])
    up = _dot(h, wu_ref[...])
    act = gate / (1.0 + jnp.exp(-gate)) * up
    out_ref[...] = x + _dot(act.astype(BF16), wd_ref[...])


def _ffn(x2d, gain, w_in, w_out, tm):
    m = x2d.shape[0]
    wg = w_in[:, :D_FF].astype(BF16)
    wu = w_in[:, D_FF:].astype(BF16)
    wd = w_out.astype(BF16)

    def full(a):
        return pl.BlockSpec(a.shape, lambda i: (0, 0))

    return pl.pallas_call(
        _ffn_kernel,
        grid=(m // tm,),
        in_specs=[pl.BlockSpec((tm, D_MODEL), lambda i: (i, 0)), pl.BlockSpec((1, D_MODEL), lambda i: (0, 0)),
                  full(wg), full(wu), full(wd)],
        out_specs=pl.BlockSpec((tm, D_MODEL), lambda i: (i, 0)),
        out_shape=jax.ShapeDtypeStruct((m, D_MODEL), F32),
        compiler_params=_params(("parallel",)),
        name="ffn",
    )(x2d, gain.reshape(1, D_MODEL).astype(F32), wg, wu, wd)


def _layer(x2d, bsz, seq, norm1_g, norm2_g, w_in, qk, cmp_pos, cmp_w, w_a, w_b, w_c, w_out, w_ffn_in, w_ffn_out,
           bias_a, bias_b, nd, bias_c):
    pieces_ab, pieces_c, pieces_g = _proj_pieces(w_in, qk)
    aq, akv, iq, ik, iw, bq, bcmp, bsw, bg = _proj(x2d, norm1_g, pieces_ab, 512)
    cq, ck, cv = _proj(x2d, norm1_g, pieces_c, 512)
    (mixg,) = _proj(x2d, norm1_g, pieces_g, 512)

    ya = _dsa(iq, iw, ik, aq, akv, bias_a, nd, bsz, seq)
    kcmp, vcmp = _nsa_cmp(bcmp, cmp_pos, cmp_w, qk[3], bsz, seq)
    yb = _nsa(bq, bg, kcmp, vcmp, bsw, bias_b, nd, bsz, seq)
    c_outs = [_dilated_group(cq, ck, cv, bias_c[g], g, dil, bsz, seq) for g, (_, dil) in enumerate(C_GROUPS)]

    x1 = _merge(x2d, ya, yb, c_outs, mixg, w_a.astype(BF16), w_b.astype(BF16), w_c.astype(BF16),
                w_out.astype(BF16), 256)
    return _ffn(x1, norm2_g, w_ffn_in, w_ffn_out, 256)


def kernel(x, norm1_g, norm2_g, w_in, qk_norm_g, nsa_cmp_pos, nsa_cmp_w, w_branch_a, w_branch_b, w_branch_c, w_out, w_ffn_in, w_ffn_out, rel_bias):
    bsz, seq, d = x.shape
    assert d == D_MODEL and seq % (TQ * max(dil for _, dil in C_GROUPS)) == 0
    assert seq % (KB_TILES * TK) == 0 and seq >= B_WINDOW + TK
    for win, dil in C_GROUPS:
        assert win == TK * dil
    bias_a, nd = _toeplitz_bias(rel_bias[:, :A_HEADS], seq)
    bias_b, _ = _toeplitz_bias(rel_bias[:, A_HEADS:A_HEADS + B_HEADS], seq)
    rel_c = rel_bias[:, A_HEADS + B_HEADS:]
    bias_c = [_dilated_bias(rel_c[:, g * C_HEADS_PER_GROUP:(g + 1) * C_HEADS_PER_GROUP], dil)
              for g, (_, dil) in enumerate(C_GROUPS)]
    x2d = x.reshape(bsz * seq, d)
    for layer in range(norm1_g.shape[0]):
        x2d = _layer(x2d, bsz, seq, norm1_g[layer], norm2_g[layer], w_in[layer], qk_norm_g[layer],
                     nsa_cmp_pos[layer], nsa_cmp_w[layer], w_branch_a[layer], w_branch_b[layer],
                     w_branch_c[layer], w_out[layer], w_ffn_in[layer], w_ffn_out[layer],
                     bias_a, bias_b, nd, bias_c)
    return x2d.reshape(bsz, seq, d)
```

```python
import functools
import math

import numpy as np
import jax
import jax.numpy as jnp
from jax import lax
from jax.experimental import pallas as pl
from jax.experimental.pallas import tpu as pltpu

F32 = jnp.float32
BF16 = jnp.bfloat16
I32 = jnp.int32

D_MODEL = 1024
HEAD_DIM = 64
NORM_EPS = 1e-6
REL_BUCKETS = 32
REL_MAX_DIST = 2048

A_HEADS = 8
A_IDX_HEADS = 4
A_TOPK_MAX = 256
B_HEADS = 8
B_KV_HEADS = 2
B_GROUP = B_HEADS // B_KV_HEADS
B_CMP_LEN = 32
B_CMP_STRIDE = 16
B_SEL_LEN = 64
B_SEL_TOPK_MAX = 16
B_WINDOW = 512
C_GROUPS = ((128, 1), (512, 4), (2048, 16))
C_HEADS_PER_GROUP = 4
C_HEADS = C_HEADS_PER_GROUP * len(C_GROUPS)
D_FF = ((8 * D_MODEL + 3 * 256 - 1) // (3 * 256)) * 256

_O_AQ = 0
_O_AK = _O_AQ + A_HEADS * HEAD_DIM
_O_AV = _O_AK + HEAD_DIM
_O_IQ = _O_AV + HEAD_DIM
_O_IK = _O_IQ + A_IDX_HEADS * HEAD_DIM
_O_IW = _O_IK + HEAD_DIM
_O_BQ = _O_IW + A_IDX_HEADS
_O_BKV = _O_BQ + B_HEADS * HEAD_DIM
_O_BG = _O_BKV + 6 * B_KV_HEADS * HEAD_DIM
_O_CQ = _O_BG + 3 * B_HEADS
_O_CK = _O_CQ + C_HEADS * HEAD_DIM
_O_CV = _O_CK + C_HEADS * HEAD_DIM
_O_MIX = _O_CV + C_HEADS * HEAD_DIM
_O_END = _O_MIX + 3 * D_MODEL

TQ = 128
TK = 128
KB_TILES = 4
A_PAIR = 2
LOG2E = 1.4426950408889634
LANES = 128
VMEM_LIMIT = 56 * 1024 * 1024
INT_MIN = -2 ** 31
NEG_INIT = -1e30
SCALE = HEAD_DIM ** -0.5


def _params(sem):
    return pltpu.CompilerParams(dimension_semantics=sem, vmem_limit_bytes=VMEM_LIMIT)


def _dot_t(a, b):
    return lax.dot_general(a, b, (((1,), (1,)), ((), ())), preferred_element_type=F32)


def _dot(a, b):
    return jnp.dot(a, b, preferred_element_type=F32)


def _split_dot(a, b_bf16):
    hi = a.astype(BF16)
    lo = (a - hi.astype(F32)).astype(BF16)
    return _dot(hi, b_bf16) + _dot(lo, b_bf16)


def _bucket_table(n_max):
    n = np.arange(n_max, dtype=np.int64)
    exact = REL_BUCKETS // 2
    nf = np.maximum(n, 1).astype(np.float32)
    large = exact + (np.log(nf / np.float32(exact)) / np.float32(math.log(REL_MAX_DIST / exact))
                     * np.float32(REL_BUCKETS - exact)).astype(np.int32)
    return np.where(n < exact, n, np.minimum(large, REL_BUCKETS - 1)).astype(np.int32)


def _num_bias_tiles(seq):
    bucket = _bucket_table(seq + TQ)
    first_sat = int(np.min(np.nonzero(bucket == REL_BUCKETS - 1)[0]))
    assert np.all(bucket[first_sat:] == REL_BUCKETS - 1)
    nd = -(-(first_sat + TK - 1) // TQ) + 1
    return min(nd, seq // TQ)


def _bias_kernel(nh, group, scale, idx_ref, rel_ref, o_ref):
    idx = idx_ref[0]
    c = idx.shape[1]
    acc = [jnp.zeros(idx.shape, F32) for _ in range(nh)]
    for b in range(REL_BUCKETS):
        hit = idx == b
        for h in range(nh):
            acc[h] = jnp.where(hit, rel_ref[b, h] * scale, acc[h])
    for h in range(nh):
        o_ref[h // group, :, (h % group) * c:(h % group + 1) * c] = acc[h]


def _bias_tiles(rel_cols, idx, group=1, scale=1.0):
    n, r, c = idx.shape
    nh = rel_cols.shape[1]
    return pl.pallas_call(
        functools.partial(_bias_kernel, nh, group, scale),
        grid=(n,),
        in_specs=[pl.BlockSpec((1, r, c), lambda k: (k, 0, 0)),
                  pl.BlockSpec(memory_space=pltpu.SMEM)],
        out_specs=pl.BlockSpec((nh // group, r, group * c), lambda k: (k, 0, 0)),
        out_shape=jax.ShapeDtypeStruct((n * nh // group, r, group * c), F32),
        compiler_params=_params(("parallel",)),
        name="bias_tiles",
    )(jnp.asarray(idx, I32), rel_cols.astype(F32))


def _toeplitz_bias(rel_cols, seq, keys_on_rows=False, group=1, scale=1.0):
    nd = _num_bias_tiles(seq)
    bucket = _bucket_table(seq + TQ)
    d = np.arange(nd)[:, None, None] * TQ + np.arange(TQ)[None, :, None] - np.arange(TK)[None, None, :]
    idx = bucket[np.clip(d, 0, None)]
    return _bias_tiles(rel_cols, idx.transpose(0, 2, 1) if keys_on_rows else idx, group, scale), nd


def _dilated_bias(rel_cols, dil):
    bucket = _bucket_table(2 * TK * dil + 1)
    du = np.arange(TQ)[:, None] + TK - np.arange(2 * TK)[None, :]
    return _bias_tiles(rel_cols, bucket[np.clip(du, 0, None) * dil][None])


def _proj_kernel(kinds, *refs):
    n = len(kinds)
    x_ref, g_ref, gs_ref = refs[0], refs[1], refs[2]
    w_refs = refs[3:3 + 2 * n:2]
    aux_refs = refs[4:4 + 2 * n:2]
    out_refs = iter(refs[3 + 2 * n:])
    x = x_ref[...]
    ms = jnp.mean(x * x, axis=-1, keepdims=True)
    h = (x * lax.rsqrt(ms + NORM_EPS) * g_ref[...]).astype(BF16)
    for kind, w_ref, aux_ref in zip(kinds, w_refs, aux_refs):
        if kind == "dsa_t":
            vt_ref, iwt_ref = next(out_refs), next(out_refs)
            yt = _dot_t(w_ref[...], h)
            rows = lax.broadcasted_iota(I32, (LANES, yt.shape[1]), 0)
            vt = (yt[0:LANES] + jnp.where(rows >= HEAD_DIM, 1.0, 0.0)).astype(vt_ref.dtype)
            kbw = vt_ref.shape[2]
            for c in range(vt_ref.shape[0]):
                vt_ref[c] = vt[:, c * kbw:(c + 1) * kbw]
            iwt_ref[...] = yt[LANES:LANES + 8]
            continue
        o_ref = next(out_refs)
        width = w_ref.shape[1]
        cw = 256 if width % 256 == 0 else LANES
        for c0 in range(0, width, cw):
            y = _dot(h, w_ref[:, c0:c0 + cw])
            if kind == "norm":
                gsum = _split_dot(y * y, gs_ref[:cw, :cw])
                r = lax.rsqrt(gsum * (1.0 / HEAD_DIM) + NORM_EPS)
                mask = aux_ref[0:1, c0:c0 + cw]
                fac = mask * (r * aux_ref[1:2, c0:c0 + cw]) + (1.0 - mask)
                y = y * fac + aux_ref[2:3, c0:c0 + cw]
            elif kind == "sigmoid":
                y = 1.0 / (1.0 + jnp.exp(-y))
            o_ref[:, c0:c0 + cw] = y.astype(o_ref.dtype)


def _proj(x2d, gain, pieces, tm):
    m, d = x2d.shape
    kinds = tuple(p[0] for p in pieces)
    gs = (np.arange(256)[:, None] // HEAD_DIM == np.arange(256)[None, :] // HEAD_DIM)
    gs = jnp.asarray(gs, BF16)
    in_specs = [pl.BlockSpec((tm, d), lambda i: (i, 0)),
                pl.BlockSpec((1, d), lambda i: (0, 0)),
                pl.BlockSpec((256, 256), lambda i: (0, 0))]
    args = [x2d, gain.reshape(1, d).astype(F32), gs]
    out_specs, out_shapes = [], []
    kb = KB_TILES * TK
    for kind, w, aux, dt in pieces:
        in_specs += [pl.BlockSpec(w.shape, lambda i: (0, 0)), pl.BlockSpec(aux.shape, lambda i: (0, 0))]
        args += [w, aux]
        if kind == "dsa_t":
            out_specs += [pl.BlockSpec((tm // kb, LANES, kb), lambda i: (i, 0, 0)),
                          pl.BlockSpec((8, tm), lambda i: (0, i))]
            out_shapes += [jax.ShapeDtypeStruct((m // kb, LANES, kb), dt), jax.ShapeDtypeStruct((8, m), F32)]
            continue
        nw = w.shape[1]
        out_specs.append(pl.BlockSpec((tm, nw), lambda i: (i, 0)))
        out_shapes.append(jax.ShapeDtypeStruct((m, nw), dt))
    return pl.pallas_call(
        functools.partial(_proj_kernel, kinds),
        grid=(m // tm,),
        in_specs=in_specs, out_specs=out_specs, out_shape=out_shapes,
        compiler_params=_params(("parallel",)),
        name="proj",
    )(*args)


def _aux(width, mask=None, gain=None, add=None):
    z = jnp.zeros((width,), F32)
    return jnp.stack([z if mask is None else mask, z if gain is None else gain, z if add is None else add])


def _seg(*parts):
    ref = next(p for p in parts if not isinstance(p, int))
    return jnp.concatenate([jnp.zeros(ref.shape[:-1] + (p,), ref.dtype) if isinstance(p, int) else p
                            for p in parts], axis=-1)


def _proj_pieces(w_in, qk):
    w = w_in.astype(BF16)
    hd = HEAD_DIM
    ones, zeros = jnp.ones((hd,), F32), jnp.zeros((hd,), F32)
    cat = jnp.concatenate

    def cols(a, b):
        return w[:, a:b]

    pieces_ab = [
        ("norm", cols(_O_AQ, _O_AK), _aux(512, jnp.ones((512,), F32), jnp.tile(qk[0], A_HEADS) * (SCALE * LOG2E)),
         BF16),
        ("norm", _seg(cols(_O_AK, _O_AV), hd), _aux(LANES, cat([ones, zeros]), cat([qk[1], zeros])), BF16),
        ("dsa_t", _seg(cols(_O_AV, _O_IQ), hd, cols(_O_IW, _O_BQ), 16 - A_IDX_HEADS).T, _aux(LANES), BF16),
        ("plain", cols(_O_IQ, _O_IK), _aux(256), BF16),
        ("plain", _seg(cols(_O_IK, _O_IW), LANES - hd), _aux(LANES), BF16),
        ("norm", cols(_O_BQ, _O_BKV), _aux(512, jnp.ones((512,), F32), jnp.tile(qk[2], B_HEADS) * SCALE), BF16),
        ("plain", cols(_O_BKV, _O_BKV + 256), _aux(256), F32),
    ]
    o = _O_BKV + 256
    ks0, ks1, vs0, vs1, kw0, kw1, vw0, vw1 = [cols(o + i * hd, o + (i + 1) * hd) for i in range(8)]
    kmask = cat([ones, zeros] * 4 + [zeros, zeros] * 4)
    kgain = cat([qk[3], zeros] * 4 + [zeros, zeros] * 4)
    vadd = cat([zeros, zeros] * 4 + [zeros, ones] * 4)
    pieces_ab += [
        ("norm", _seg(ks0, hd, ks1, hd, kw0, hd, kw1, hd, vs0, hd, vs1, hd, vw0, hd, vw1, hd),
         _aux(1024, kmask, kgain, vadd), BF16),
        ("sigmoid", _seg(cols(_O_BG, _O_CQ), LANES - 3 * B_HEADS), _aux(LANES), F32),
    ]
    pieces_c = [
        ("norm", cols(_O_CQ, _O_CK), _aux(768, jnp.ones((768,), F32), jnp.tile(qk[4], C_HEADS) * SCALE), BF16),
        ("norm", cols(_O_CK, _O_CV), _aux(768, jnp.ones((768,), F32), jnp.tile(qk[5], C_HEADS)), BF16),
        ("plain", cols(_O_CV, _O_MIX), _aux(768), BF16),
    ]
    pieces_g = [("sigmoid", cols(_O_MIX, _O_END), _aux(3 * D_MODEL), F32)]
    return pieces_ab, pieces_c, pieces_g


def _flash_init(m_ref, acc_ref):
    m_ref[...] = jnp.full(m_ref.shape, NEG_INIT, F32)
    acc_ref[...] = jnp.zeros(acc_ref.shape, F32)


def _flash_block(q_ref, heads, group_of, kt, v_aug, bias_fn, masks, m_ref, acc_ref):
    heads = list(heads)
    s_all = [_dot_t(q_ref[h], kt[group_of(h)]) for h in heads]
    m_old = [m_ref[h] for h in heads]
    ps, alphas = [], []
    for k, h in enumerate(heads):
        s, mk = s_all[k], masks[group_of(h)]
        sc = [jnp.where(mk[c], s[:, c * TK:(c + 1) * TK] + bias_fn(h, c), -jnp.inf) for c in range(len(mk))]
        m_new = jnp.maximum(m_old[k], jnp.max(functools.reduce(jnp.maximum, sc), axis=-1, keepdims=True))
        ps.append(jnp.concatenate([jnp.exp(x - m_new).astype(BF16) for x in sc], axis=1))
        alphas.append(jnp.exp(m_old[k] - m_new))
        m_ref[h] = m_new
    for k, h in enumerate(heads):
        acc_ref[h] = alphas[k] * acc_ref[h] + _dot(ps[k], v_aug[group_of(h)])


def _flash_block_t(q_ref, slots, kt, v_aug_t, bias_fn, masks, m_ref, acc_ref):
    slots = list(slots)
    s_all = [_dot_t(kt, q_ref[p]) for p in slots]
    m_old = [m_ref[p][0:1] for p in slots]
    ps, alphas = [], []
    for k, p in enumerate(slots):
        s = s_all[k]
        sc = [jnp.where(masks[c], s[c * TK:(c + 1) * TK] + bias_fn(p, c), -jnp.inf) for c in range(len(masks))]
        m_new = jnp.maximum(m_old[k], jnp.max(functools.reduce(jnp.maximum, sc), axis=0, keepdims=True))
        ps.append(jnp.concatenate([jnp.exp2(x - m_new).astype(BF16) for x in sc], axis=0))
        alphas.append(jnp.exp2(m_old[k] - m_new))
        m_ref[p] = jnp.broadcast_to(m_new, m_ref.shape[1:])
    for k, p in enumerate(slots):
        acc_ref[p] = alphas[k] * acc_ref[p] + _dot(v_aug_t, ps[k])


def _flash_out(acc):
    return acc[:, :HEAD_DIM] / jnp.maximum(acc[:, HEAD_DIM:HEAD_DIM + 1], 1e-30)


def _split_heads(q_ref, qs_ref, nh):
    for h in range(nh):
        qs_ref[h] = q_ref[:, h * HEAD_DIM:(h + 1) * HEAD_DIM]


def _dsa_kernel(topk, nd, pbits, iq_ref, iwt_ref, ik_ref, aq_ref, ak_ref, avt_ref, bias_ref, o_ref,
                keys_ref, half_ref, iqs_ref, qs_ref, m_ref, acc_ref):
    i = pl.program_id(1)
    nk = i + 1
    nkb = (nk + KB_TILES - 1) // KB_TILES
    kb = KB_TILES * TK
    krow = lax.broadcasted_iota(I32, (TK, TQ), 0)
    qpos = i * TQ + lax.broadcasted_iota(I32, (TK, TQ), 1)
    i16 = jnp.int16

    iwt = iwt_ref[...]
    for h in range(A_IDX_HEADS):
        iqs_ref[h * TQ:(h + 1) * TQ] = iq_ref[:, h * HEAD_DIM:(h + 1) * HEAD_DIM]

    def score_block(jb, c):
        r0 = pl.multiple_of(jb * kb, kb)
        d = _dot_t(ik_ref[pl.ds(r0, kb), 0:HEAD_DIM], iqs_ref[...])
        acc = jnp.zeros((kb, TQ), F32)
        for h in range(A_IDX_HEADS):
            acc = acc + jnp.maximum(d[:, h * TQ:(h + 1) * TQ], 0.0) * iwt[h:h + 1]
        acc = jnp.where(acc == 0.0, 0.0, acc)
        bits = lax.bitcast_convert_type(acc, I32)
        key = jnp.where(bits < 0, bits ^ 0x7FFFFFFF, bits)
        for t in range(KB_TILES):
            j = jb * KB_TILES + t
            kj = jnp.where(j * TK + krow <= qpos, key[t * TK:(t + 1) * TK], INT_MIN)
            keys_ref[j] = kj
            half_ref[j] = jnp.right_shift(kj, 16).astype(i16)
        return c

    lax.fori_loop(0, nkb, score_block, 0)

    kf = float(topk)

    def search_half():
        def step(b, u):
            cand = u | jnp.left_shift(jnp.int32(1), 15 - b)
            cb = jnp.broadcast_to((cand - 32768).astype(i16), (TK, TQ))

            def body(jb, acc):
                for t in range(KB_TILES):
                    hit = half_ref[jb * KB_TILES + t] >= cb
                    acc = acc + jnp.where(hit, jnp.ones((), i16), jnp.zeros((), i16))
                return acc

            acc = lax.fori_loop(0, nkb, body, jnp.zeros((TK, TQ), i16))
            cnt = jnp.sum(acc.astype(F32), axis=0, keepdims=True)
            return jnp.where(cnt >= kf, cand, u)

        return lax.fori_loop(0, 16, step, jnp.zeros((1, TQ), I32))

    t_hi = search_half() - 32768

    def low_tile(j, c):
        k = keys_ref[j]
        hi = jnp.right_shift(k, 16)
        lo = (k & 0xFFFF) - 32768
        half_ref[j] = jnp.where(hi > t_hi, 32767, jnp.where(hi < t_hi, -32768, lo)).astype(i16)
        return c

    lax.fori_loop(0, nk, low_tile, 0)
    thr = t_hi * 65536 + search_half()

    def count(pred):
        def body(j, acc):
            return acc + jnp.where(pred(j, keys_ref[j]), 1.0, 0.0)
        acc = lax.fori_loop(0, nk, body, jnp.zeros((TK, TQ), F32))
        return jnp.sum(acc, axis=0, keepdims=True)

    n_ge = count(lambda j, k: k >= thr)

    @pl.when(jnp.max(n_ge) > kf)
    def _():
        need = kf - count(lambda j, k: k > thr)

        def tie_step(b, p):
            cand = p | jnp.left_shift(jnp.int32(1), pbits - 1 - b)
            cnt = count(lambda j, k: (k == thr) & (j * TK + krow < cand))
            return jnp.where(cnt < need, cand, p)

        plim = lax.fori_loop(0, pbits, tie_step, jnp.zeros((1, TQ), I32))
        lose = (n_ge > kf) & (thr > INT_MIN)

        def demote(j, c):
            k = keys_ref[j]
            keys_ref[j] = jnp.where(lose & (k == thr) & (j * TK + krow > plim), k - 1, k)
            return c

        lax.fori_loop(0, nk, demote, 0)

    sel_thr = jnp.maximum(thr, INT_MIN + 1)

    nslot = A_HEADS // A_PAIR
    for h in range(A_HEADS):
        qs_ref[h // A_PAIR, (h % A_PAIR) * TQ:(h % A_PAIR + 1) * TQ] = aq_ref[:, h * HEAD_DIM:(h + 1) * HEAD_DIM]
    _flash_init(m_ref, acc_ref)

    def att_block(jb, c):
        r0 = pl.multiple_of(jb * kb, kb)
        kt = ak_ref[pl.ds(r0, kb), 0:HEAD_DIM]
        j0 = jb * KB_TILES
        masks = []
        for t in range(KB_TILES):
            mk = keys_ref[j0 + t] >= sel_thr
            masks.append(jnp.concatenate([mk] * A_PAIR, axis=1))
        d = [jnp.clip(i - j0 - t, 0, nd - 1) * nslot for t in range(KB_TILES)]
        _flash_block_t(qs_ref, range(nslot), kt, avt_ref[jb], lambda p, t: bias_ref[d[t] + p], masks,
                       m_ref, acc_ref)
        return c

    lax.fori_loop(0, nkb, att_block, 0)
    outs = []
    for h in range(A_HEADS):
        acc = acc_ref[h // A_PAIR][:, (h % A_PAIR) * TQ:(h % A_PAIR + 1) * TQ]
        outs.append(acc[:HEAD_DIM] / jnp.maximum(acc[HEAD_DIM:HEAD_DIM + 1], 1e-30))
    o_ref[...] = jnp.concatenate(outs, axis=0).T.astype(o_ref.dtype)


def _dsa(iq, iwt, ik, aq, ak, avt, bias, nd, bsz, seq):
    nq = seq // TQ
    kb = KB_TILES * TK
    topk = min(A_TOPK_MAX, seq // 4)
    pbits = max(1, (seq - 1).bit_length())
    return pl.pallas_call(
        functools.partial(_dsa_kernel, topk, nd, pbits),
        grid=(bsz, nq),
        in_specs=[
            pl.BlockSpec((TQ, 256), lambda b, i: (b * nq + i, 0)),
            pl.BlockSpec((8, TQ), lambda b, i: (0, b * nq + i)),
            pl.BlockSpec((seq, LANES), lambda b, i: (b, 0)),
            pl.BlockSpec((TQ, 512), lambda b, i: (b * nq + i, 0)),
            pl.BlockSpec((seq, LANES), lambda b, i: (b, 0)),
            pl.BlockSpec((seq // kb, LANES, kb), lambda b, i: (b, 0, 0)),
            pl.BlockSpec(bias.shape, lambda b, i: (0, 0, 0)),
        ],
        out_specs=pl.BlockSpec((TQ, 512), lambda b, i: (b * nq + i, 0)),
        out_shape=jax.ShapeDtypeStruct((bsz * seq, 512), BF16),
        scratch_shapes=[pltpu.VMEM((seq // TK, TK, TQ), I32),
                        pltpu.VMEM((seq // TK, TK, TQ), jnp.int16),
                        pltpu.VMEM((A_IDX_HEADS * TQ, HEAD_DIM), BF16),
                        pltpu.VMEM((A_HEADS // A_PAIR, A_PAIR * TQ, HEAD_DIM), BF16),
                        pltpu.VMEM((A_HEADS // A_PAIR, 8, A_PAIR * TQ), F32),
                        pltpu.VMEM((A_HEADS // A_PAIR, LANES, A_PAIR * TQ), F32)],
        compiler_params=_params(("parallel", "arbitrary")),
        name="dsa",
    )(iq, iwt, ik, aq, ak, avt, bias)


def _nsa_cmp_kernel(x_ref, pos_ref, wlo_ref, whi_ref, gain_ref, k_ref, v_ref):
    x = x_ref[...]
    lo = _dot((x + pos_ref[0:1, :]).astype(BF16), wlo_ref[...])
    hi = _dot((x + pos_ref[1:2, :]).astype(BF16), whi_ref[...])
    nrow = x.shape[0]
    pre = lo + pltpu.roll(hi, nrow - 1, 0)
    ks = []
    for g in range(B_KV_HEADS):
        kg = pre[:, g * HEAD_DIM:(g + 1) * HEAD_DIM]
        ms = jnp.mean(kg * kg, axis=-1, keepdims=True)
        ks.append(kg * lax.rsqrt(ms + NORM_EPS) * gain_ref[...])
    k_ref[...] = jnp.concatenate(ks, axis=1).astype(k_ref.dtype)
    v_ref[...] = pre[:, LANES:2 * LANES].astype(v_ref.dtype)


def _nsa_cmp(bcmp, cmp_pos, cmp_w, k_gain, bsz, seq):
    nch = seq // B_CMP_STRIDE
    half = B_CMP_LEN // 2
    width = half * 256
    x = bcmp.reshape(bsz * nch, width)

    def wmat(l0):
        w = jnp.zeros((half, 4, HEAD_DIM, 4, HEAD_DIM), F32)
        for j in range(4):
            w = w.at[:, j, :, j, :].set(cmp_w[j // 2, l0:l0 + half])
        return w.reshape(width, 256).astype(BF16)

    def prow(l0):
        p = jnp.stack([cmp_pos[0, l0:l0 + half], cmp_pos[0, l0:l0 + half],
                       cmp_pos[1, l0:l0 + half], cmp_pos[1, l0:l0 + half]], axis=1)
        return p.reshape(width)

    pos = jnp.stack([prow(0), prow(half)]).astype(F32)
    return pl.pallas_call(
        _nsa_cmp_kernel,
        grid=(bsz,),
        in_specs=[
            pl.BlockSpec((nch, width), lambda b: (b, 0)),
            pl.BlockSpec((2, width), lambda b: (0, 0)),
            pl.BlockSpec((width, 256), lambda b: (0, 0)),
            pl.BlockSpec((width, 256), lambda b: (0, 0)),
            pl.BlockSpec((1, HEAD_DIM), lambda b: (0, 0)),
        ],
        out_specs=[pl.BlockSpec((nch, LANES), lambda b: (b, 0)), pl.BlockSpec((nch, LANES), lambda b: (b, 0))],
        out_shape=[jax.ShapeDtypeStruct((bsz * nch, LANES), BF16)] * 2,
        compiler_params=_params(("parallel",)),
        name="nsa_cmp",
    )(x, pos, wmat(0), wmat(half), k_gain.reshape(1, HEAD_DIM).astype(F32))


def _nsa_kernel(seq, nd, q_ref, g_ref, kc_ref, vc_ref, kv_ref, bias_ref, o_ref,
                imp_ref, sel_ref, qs_ref, m_ref, acc_ref):
    i = pl.program_id(1)
    ncp = seq // B_CMP_STRIDE
    ns = seq // B_SEL_LEN
    n_top = min(B_SEL_TOPK_MAX, ns)
    hg = B_GROUP
    row = lax.broadcasted_iota(I32, (TQ, TK), 0)
    col = lax.broadcasted_iota(I32, (TQ, TK), 1)
    tpos = i * TQ + row
    _split_heads(q_ref, qs_ref, B_HEADS)

    n_idx = lax.broadcasted_iota(I32, (ncp, TQ), 0)
    t_c = i * TQ + lax.broadcasted_iota(I32, (ncp, TQ), 1)
    cmask = n_idx * B_CMP_STRIDE + (B_CMP_LEN - 1) <= t_c
    om = lax.broadcasted_iota(I32, (ns, ncp), 0) * B_SEL_LEN
    on = lax.broadcasted_iota(I32, (ns, ncp), 1) * B_CMP_STRIDE
    ovt = jnp.where((on < om + B_SEL_LEN) & (on + B_CMP_LEN > om), 1.0, 0.0).astype(BF16)
    m_idx = lax.broadcasted_iota(I32, (ns, TQ), 0)
    jt = (i * TQ + lax.broadcasted_iota(I32, (ns, TQ), 1)) // B_SEL_LEN
    forced = (m_idx == 0) | (m_idx == jt) | (m_idx == jt - 1)

    kc = [kc_ref[:, g * HEAD_DIM:(g + 1) * HEAD_DIM] for g in range(B_KV_HEADS)]
    vc = [vc_ref[:, g * HEAD_DIM:(g + 1) * HEAD_DIM] for g in range(B_KV_HEADS)]
    st_all = [_dot_t(kc[h // hg], qs_ref[h]) for h in range(B_HEADS)]
    pts = []
    for h in range(B_HEADS):
        st = jnp.where(cmask, st_all[h], -jnp.inf)
        mx = jnp.max(st, axis=0, keepdims=True)
        mx = jnp.where(mx == -jnp.inf, 0.0, mx)
        e = jnp.exp(st - mx)
        pts.append(e / jnp.maximum(jnp.sum(e, axis=0, keepdims=True), 1e-30))
    oc = [_dot(pts[h].T.astype(BF16), vc[h // hg]) for h in range(B_HEADS)]

    for g in range(B_KV_HEADS):
        psum = functools.reduce(lambda a, b: a + b, pts[g * hg:(g + 1) * hg])
        hi = psum.astype(BF16)
        lo = (psum - hi.astype(F32)).astype(BF16)
        imp = _dot(ovt, hi) + _dot(ovt, lo)
        imp = jnp.where(forced, jnp.inf, jnp.where(m_idx <= jt, imp, -jnp.inf))
        imp_ref[g] = imp

        def rank_step(mp, rank, g=g, imp=imp):
            vp = jnp.broadcast_to(imp_ref[g, pl.ds(mp, 1), :], (ns, TQ))
            before = (vp > imp) | ((vp == imp) & (mp < m_idx))
            return rank + jnp.where(before, 1.0, 0.0)

        rank = lax.fori_loop(0, ns, rank_step, jnp.zeros((ns, TQ), F32), unroll=8)
        selt = jnp.where((rank < float(n_top)) & (m_idx <= jt), 1.0, 0.0)
        selt = jnp.concatenate([selt, jnp.zeros((LANES - ns, TQ), F32)], axis=0)
        sel_ref[g] = selt.T.astype(BF16)

    def branch_block(j0, ntiles, koff, voff, mask_fn):
        r0 = pl.multiple_of(j0 * TK, TK)
        d = [jnp.clip(i - j0 - t, 0, nd - 1) * B_HEADS for t in range(ntiles)]
        rows = pl.ds(r0, ntiles * TK)
        kt = [kv_ref[rows, koff + g * LANES:koff + g * LANES + HEAD_DIM] for g in range(B_KV_HEADS)]
        va = [kv_ref[rows, voff + g * LANES:voff + (g + 1) * LANES] for g in range(B_KV_HEADS)]
        masks = [mask_fn(g, j0) for g in range(B_KV_HEADS)]
        _flash_block(qs_ref, range(B_HEADS), lambda h: h // hg, kt, va, lambda h, t: bias_ref[d[t] + h],
                     masks, m_ref, acc_ref)

    kb = KB_TILES * TK
    e_m = lax.broadcasted_iota(I32, (LANES, kb), 0)
    e_c = lax.broadcasted_iota(I32, (LANES, kb), 1) // B_SEL_LEN

    def sel_masks(g, j0):
        expand = jnp.where(e_m == j0 * (TK // B_SEL_LEN) + e_c, 1.0, 0.0).astype(BF16)
        chosen = _dot(sel_ref[g], expand) > 0.5
        return [chosen[:, t * TK:(t + 1) * TK] & ((j0 + t) * TK + col <= tpos) for t in range(KB_TILES)]

    def sel_body(jb, c):
        branch_block(jb * KB_TILES, KB_TILES, 0, 4 * LANES, sel_masks)
        return c

    gates = g_ref[...]

    def gate(h, br):
        return gates[:, 3 * h + br:3 * h + br + 1]

    _flash_init(m_ref, acc_ref)
    lax.fori_loop(0, (i + KB_TILES) // KB_TILES, sel_body, 0)
    part = [gate(h, 0) * oc[h] + gate(h, 1) * _flash_out(acc_ref[h]) for h in range(B_HEADS)]

    wt = B_WINDOW // TK + 1
    w0 = jnp.maximum(i + 1 - wt, 0)

    def win_masks(g, j0):
        out = []
        for t in range(wt):
            dist = tpos - ((j0 + t) * TK + col)
            out.append((dist >= 0) & (dist < B_WINDOW))
        return out

    _flash_init(m_ref, acc_ref)
    branch_block(w0, wt, 2 * LANES, 6 * LANES, win_masks)
    outs = [part[h] + gate(h, 2) * _flash_out(acc_ref[h]) for h in range(B_HEADS)]
    o_ref[...] = jnp.concatenate(outs, axis=1).astype(o_ref.dtype)


def _nsa(bq, bg, kcmp, vcmp, bsw, bias, nd, bsz, seq):
    nq = seq // TQ
    ncp = seq // B_CMP_STRIDE
    ns = seq // B_SEL_LEN
    return pl.pallas_call(
        functools.partial(_nsa_kernel, seq, nd),
        grid=(bsz, nq),
        in_specs=[
            pl.BlockSpec((TQ, 512), lambda b, i: (b * nq + i, 0)),
            pl.BlockSpec((TQ, LANES), lambda b, i: (b * nq + i, 0)),
            pl.BlockSpec((ncp, LANES), lambda b, i: (b, 0)),
            pl.BlockSpec((ncp, LANES), lambda b, i: (b, 0)),
            pl.BlockSpec((seq, 1024), lambda b, i: (b, 0)),
            pl.BlockSpec(bias.shape, lambda b, i: (0, 0, 0)),
        ],
        out_specs=pl.BlockSpec((TQ, 512), lambda b, i: (b * nq + i, 0)),
        out_shape=jax.ShapeDtypeStruct((bsz * seq, 512), BF16),
        scratch_shapes=[pltpu.VMEM((B_KV_HEADS, ns, TQ), F32), pltpu.VMEM((B_KV_HEADS, TQ, LANES), BF16),
                        pltpu.VMEM((B_HEADS, TQ, HEAD_DIM), BF16),
                        pltpu.VMEM((B_HEADS, TQ, LANES), F32),
                        pltpu.VMEM((B_HEADS, TQ, LANES), F32)],
        compiler_params=_params(("parallel", "arbitrary")),
        name="nsa",
    )(bq, bg, kcmp, vcmp, bsw, bias)


def _dil_kernel(q_ref, kp_ref, kc_ref, vp_ref, vc_ref, bias_ref, o_ref, lse_ref):
    i = pl.program_id(2)
    q = q_ref[...]
    k2 = jnp.concatenate([kp_ref[...], kc_ref[...]], axis=0)
    v2 = jnp.concatenate([vp_ref[...], vc_ref[...]], axis=0)
    row = lax.broadcasted_iota(I32, (TQ, 2 * TK), 0)
    col = lax.broadcasted_iota(I32, (TQ, 2 * TK), 1)
    du = row + TK - col
    valid = (du >= 0) & (du <= TK) & ((col >= TK) | (i > 0))
    for hh in range(C_HEADS_PER_GROUP):
        hs = slice(hh * HEAD_DIM, (hh + 1) * HEAD_DIM)
        s = jnp.where(valid, _dot_t(q[:, hs], k2[:, hs]) + bias_ref[hh], -jnp.inf)
        m = jnp.max(s, axis=-1, keepdims=True)
        e = jnp.exp(s - m)
        den = jnp.sum(e, axis=-1, keepdims=True)
        o_ref[:, hs] = _dot(e.astype(BF16), v2[:, hs]) / den
        lse_ref[:, hs] = jnp.broadcast_to(m + jnp.log(den), (TQ, HEAD_DIM))


def _dilated_group(cq, ck, cv, bias, g, dil, bsz, seq):
    ln = seq // dil
    nq = ln // TQ
    width = C_HEADS * HEAD_DIM
    gw = C_HEADS_PER_GROUP * HEAD_DIM
    ncb = width // gw
    views = [a.reshape(bsz * ln, dil * width) for a in (cq, ck, cv)]

    def cur(b, r, i):
        return (b * nq + i, r * ncb + g)

    def prev(b, r, i):
        return (b * nq + jnp.maximum(i - 1, 0), r * ncb + g)

    blk = (TQ, gw)
    o, lse = pl.pallas_call(
        _dil_kernel,
        grid=(bsz, dil, nq),
        in_specs=[pl.BlockSpec(blk, cur), pl.BlockSpec(blk, prev), pl.BlockSpec(blk, cur),
                  pl.BlockSpec(blk, prev), pl.BlockSpec(blk, cur),
                  pl.BlockSpec(bias.shape, lambda b, r, i: (0, 0, 0))],
        out_specs=[pl.BlockSpec(blk, lambda b, r, i: (b * nq + i, r))] * 2,
        out_shape=[jax.ShapeDtypeStruct((bsz * ln, dil * gw), F32)] * 2,
        compiler_params=_params(("parallel", "parallel", "arbitrary")),
        name=f"dilated_d{dil}",
    )(views[0], views[1], views[1], views[2], views[2], bias)
    return o.reshape(bsz * seq, gw), lse.reshape(bsz * seq, gw)


def _merge_kernel(x_ref, ya_ref, yb_ref, o0_ref, l0_ref, o1_ref, l1_ref, o2_ref, l2_ref, g_ref,
                  wa_ref, wb_ref, wc_ref, wo_ref, out_ref):
    l0, l1, l2 = l0_ref[...], l1_ref[...], l2_ref[...]
    mx = jnp.maximum(jnp.maximum(l0, l1), l2)
    e0, e1, e2 = jnp.exp(l0 - mx), jnp.exp(l1 - mx), jnp.exp(l2 - mx)
    yc = (e0 * o0_ref[...] + e1 * o1_ref[...] + e2 * o2_ref[...]) / (e0 + e1 + e2)
    ya = _dot(ya_ref[...], wa_ref[...])
    yb = _dot(yb_ref[...], wb_ref[...])
    yc = _dot(yc.astype(BF16), wc_ref[...])
    d = D_MODEL
    z = g_ref[:, 0:d] * ya + g_ref[:, d:2 * d] * yb + g_ref[:, 2 * d:3 * d] * yc
    out_ref[...] = x_ref[...] + _dot(z.astype(BF16), wo_ref[...])


def _merge(x2d, ya, yb, c_outs, mixg, wa, wb, wc, wo, tm):
    m = x2d.shape[0]

    def rows(w):
        return pl.BlockSpec((tm, w), lambda i: (i, 0))

    def full(a):
        return pl.BlockSpec(a.shape, lambda i: (0, 0))

    c_flat = [a for pair in c_outs for a in pair]
    return pl.pallas_call(
        _merge_kernel,
        grid=(m // tm,),
        in_specs=[rows(D_MODEL), rows(512), rows(512)] + [rows(256)] * 6 + [rows(3 * D_MODEL)]
                 + [full(wa), full(wb), full(wc), full(wo)],
        out_specs=rows(D_MODEL),
        out_shape=jax.ShapeDtypeStruct((m, D_MODEL), F32),
        compiler_params=_params(("parallel",)),
        name="merge",
    )(x2d, ya, yb, *c_flat, mixg, wa, wb, wc, wo)


def _ffn_kernel(x_ref, g_ref, wg_ref, wu_ref, wd_ref, out_ref):
    x = x_ref[...]
    ms = jnp.mean(x * x, axis=-1, keepdims=True)
    h = (x * lax.rsqrt(ms + NORM_EPS) * g_ref[...]).astype(BF16)
    gate = _dot(h, wg_ref[...])
    up = _dot(h, wu_ref[...])
    act = gate / (1.0 + jnp.exp(-gate)) * up
    out_ref[...] = x + _dot(act.astype(BF16), wd_ref[...])


def _ffn(x2d, gain, w_in, w_out, tm):
    m = x2d.shape[0]
    wg = w_in[:, :D_FF].astype(BF16)
    wu = w_in[:, D_FF:].astype(BF16)
    wd = w_out.astype(BF16)

    def full(a):
        return pl.BlockSpec(a.shape, lambda i: (0, 0))

    return pl.pallas_call(
        _ffn_kernel,
        grid=(m // tm,),
        in_specs=[pl.BlockSpec((tm, D_MODEL), lambda i: (i, 0)), pl.BlockSpec((1, D_MODEL), lambda i: (0, 0)),
                  full(wg), full(wu), full(wd)],
        out_specs=pl.BlockSpec((tm, D_MODEL), lambda i: (i, 0)),
        out_shape=jax.ShapeDtypeStruct((m, D_MODEL), F32),
        compiler_params=_params(("parallel",)),
        name="ffn",
    )(x2d, gain.reshape(1, D_MODEL).astype(F32), wg, wu, wd)


def _layer(x2d, bsz, seq, norm1_g, norm2_g, w_in, qk, cmp_pos, cmp_w, w_a, w_b, w_c, w_out, w_ffn_in, w_ffn_out,
           bias_a, bias_b, nd, bias_c):
    pieces_ab, pieces_c, pieces_g = _proj_pieces(w_in, qk)
    aq, ak, avt, iwt, iq, ik, bq, bcmp, bsw, bg = _proj(x2d, norm1_g, pieces_ab, 512)
    cq, ck, cv = _proj(x2d, norm1_g, pieces_c, 512)
    (mixg,) = _proj(x2d, norm1_g, pieces_g, 512)

    ya = _dsa(iq, iwt, ik, aq, ak, avt, bias_a, nd, bsz, seq)
    kcmp, vcmp = _nsa_cmp(bcmp, cmp_pos, cmp_w, qk[3], bsz, seq)
    yb = _nsa(bq, bg, kcmp, vcmp, bsw, bias_b, nd, bsz, seq)
    c_outs = [_dilated_group(cq, ck, cv, bias_c[g], g, dil, bsz, seq) for g, (_, dil) in enumerate(C_GROUPS)]

    x1 = _merge(x2d, ya, yb, c_outs, mixg, w_a.astype(BF16), w_b.astype(BF16), w_c.astype(BF16),
                w_out.astype(BF16), 256)
    return _ffn(x1, norm2_g, w_ffn_in, w_ffn_out, 256)


def kernel(x, norm1_g, norm2_g, w_in, qk_norm_g, nsa_cmp_pos, nsa_cmp_w, w_branch_a, w_branch_b, w_branch_c, w_out, w_ffn_in, w_ffn_out, rel_bias):
    bsz, seq, d = x.shape
    assert d == D_MODEL and seq % (TQ * max(dil for _, dil in C_GROUPS)) == 0
    assert seq % (KB_TILES * TK) == 0 and seq >= B_WINDOW + TK
    for win, dil in C_GROUPS:
        assert win == TK * dil
    bias_a, nd = _toeplitz_bias(rel_bias[:, :A_HEADS], seq, keys_on_rows=True, group=A_PAIR, scale=LOG2E)
    bias_b, _ = _toeplitz_bias(rel_bias[:, A_HEADS:A_HEADS + B_HEADS], seq)
    rel_c = rel_bias[:, A_HEADS + B_HEADS:]
    bias_c = [_dilated_bias(rel_c[:, g * C_HEADS_PER_GROUP:(g + 1) * C_HEADS_PER_GROUP], dil)
              for g, (_, dil) in enumerate(C_GROUPS)]
    x2d = x.reshape(bsz * seq, d)
    for layer in range(norm1_g.shape[0]):
        x2d = _layer(x2d, bsz, seq, norm1_g[layer], norm2_g[layer], w_in[layer], qk_norm_g[layer],
                     nsa_cmp_pos[layer], nsa_cmp_w[layer], w_branch_a[layer], w_branch_b[layer],
                     w_branch_c[layer], w_out[layer], w_ffn_in[layer], w_ffn_out[layer],
                     bias_a, bias_b, nd, bias_c)
    return x2d.reshape(bsz, seq, d)
```

```python
import functools
import math

import numpy as np
import jax
import jax.numpy as jnp
from jax import lax
from jax.experimental import pallas as pl
from jax.experimental.pallas import tpu as pltpu

F32 = jnp.float32
BF16 = jnp.bfloat16
I32 = jnp.int32

D_MODEL = 1024
HEAD_DIM = 64
NORM_EPS = 1e-6
REL_BUCKETS = 32
REL_MAX_DIST = 2048

A_HEADS = 8
A_IDX_HEADS = 4
A_TOPK_MAX = 256
B_HEADS = 8
B_KV_HEADS = 2
B_GROUP = B_HEADS // B_KV_HEADS
B_CMP_LEN = 32
B_CMP_STRIDE = 16
B_SEL_LEN = 64
B_SEL_TOPK_MAX = 16
B_WINDOW = 512
C_GROUPS = ((128, 1), (512, 4), (2048, 16))
C_HEADS_PER_GROUP = 4
C_HEADS = C_HEADS_PER_GROUP * len(C_GROUPS)
D_FF = ((8 * D_MODEL + 3 * 256 - 1) // (3 * 256)) * 256

_O_AQ = 0
_O_AK = _O_AQ + A_HEADS * HEAD_DIM
_O_AV = _O_AK + HEAD_DIM
_O_IQ = _O_AV + HEAD_DIM
_O_IK = _O_IQ + A_IDX_HEADS * HEAD_DIM
_O_IW = _O_IK + HEAD_DIM
_O_BQ = _O_IW + A_IDX_HEADS
_O_BKV = _O_BQ + B_HEADS * HEAD_DIM
_O_BG = _O_BKV + 6 * B_KV_HEADS * HEAD_DIM
_O_CQ = _O_BG + 3 * B_HEADS
_O_CK = _O_CQ + C_HEADS * HEAD_DIM
_O_CV = _O_CK + C_HEADS * HEAD_DIM
_O_MIX = _O_CV + C_HEADS * HEAD_DIM
_O_END = _O_MIX + 3 * D_MODEL

TQ = 128
TK = 128
KB_TILES = 4
A_TQ = 256
A_PAIR = 2
LOG2E = 1.4426950408889634
LANES = 128
VMEM_LIMIT = 56 * 1024 * 1024
INT_MIN = -2 ** 31
NEG_INIT = -1e30
SCALE = HEAD_DIM ** -0.5


def _params(sem):
    return pltpu.CompilerParams(dimension_semantics=sem, vmem_limit_bytes=VMEM_LIMIT)


def _dot_t(a, b):
    return lax.dot_general(a, b, (((1,), (1,)), ((), ())), preferred_element_type=F32)


def _dot(a, b):
    return jnp.dot(a, b, preferred_element_type=F32)


def _split_dot(a, b_bf16):
    hi = a.astype(BF16)
    lo = (a - hi.astype(F32)).astype(BF16)
    return _dot(hi, b_bf16) + _dot(lo, b_bf16)


def _bucket_table(n_max):
    n = np.arange(n_max, dtype=np.int64)
    exact = REL_BUCKETS // 2
    nf = np.maximum(n, 1).astype(np.float32)
    large = exact + (np.log(nf / np.float32(exact)) / np.float32(math.log(REL_MAX_DIST / exact))
                     * np.float32(REL_BUCKETS - exact)).astype(np.int32)
    return np.where(n < exact, n, np.minimum(large, REL_BUCKETS - 1)).astype(np.int32)


def _num_bias_tiles(seq, tq):
    bucket = _bucket_table(seq + tq)
    first_sat = int(np.min(np.nonzero(bucket == REL_BUCKETS - 1)[0]))
    assert np.all(bucket[first_sat:] == REL_BUCKETS - 1)
    nd = -(-(first_sat + TK - 1) // TK) + tq // TK
    return min(nd, seq // TK)


def _bias_tile_index(i, j, tq, nd):
    return jnp.clip((tq // TK) * (i + 1) - 1 - j, 0, nd - 1)


def _bias_kernel(nh, group, scale, idx_ref, rel_ref, o_ref):
    idx = idx_ref[0]
    c = idx.shape[1]
    acc = [jnp.zeros(idx.shape, F32) for _ in range(nh)]
    for b in range(REL_BUCKETS):
        hit = idx == b
        for h in range(nh):
            acc[h] = jnp.where(hit, rel_ref[b, h] * scale, acc[h])
    for h in range(nh):
        o_ref[h // group, :, (h % group) * c:(h % group + 1) * c] = acc[h]


def _bias_tiles(rel_cols, idx, group=1, scale=1.0):
    n, r, c = idx.shape
    nh = rel_cols.shape[1]
    return pl.pallas_call(
        functools.partial(_bias_kernel, nh, group, scale),
        grid=(n,),
        in_specs=[pl.BlockSpec((1, r, c), lambda k: (k, 0, 0)),
                  pl.BlockSpec(memory_space=pltpu.SMEM)],
        out_specs=pl.BlockSpec((nh // group, r, group * c), lambda k: (k, 0, 0)),
        out_shape=jax.ShapeDtypeStruct((n * nh // group, r, group * c), F32),
        compiler_params=_params(("parallel",)),
        name="bias_tiles",
    )(jnp.asarray(idx, I32), rel_cols.astype(F32))


def _toeplitz_bias(rel_cols, seq, tq=TQ, keys_on_rows=False, group=1, scale=1.0):
    nd = _num_bias_tiles(seq, tq)
    bucket = _bucket_table(seq + tq)
    d = ((np.arange(nd)[:, None, None] - (tq // TK - 1)) * TK
         + np.arange(tq)[None, :, None] - np.arange(TK)[None, None, :])
    idx = bucket[np.clip(d, 0, None)]
    return _bias_tiles(rel_cols, idx.transpose(0, 2, 1) if keys_on_rows else idx, group, scale), nd


def _dilated_bias(rel_cols, dil):
    bucket = _bucket_table(2 * TK * dil + 1)
    du = np.arange(TQ)[:, None] + TK - np.arange(2 * TK)[None, :]
    return _bias_tiles(rel_cols, bucket[np.clip(du, 0, None) * dil][None])


def _proj_kernel(kinds, *refs):
    n = len(kinds)
    x_ref, g_ref, gs_ref = refs[0], refs[1], refs[2]
    w_refs = refs[3:3 + 2 * n:2]
    aux_refs = refs[4:4 + 2 * n:2]
    out_refs = iter(refs[3 + 2 * n:])
    x = x_ref[...]
    ms = jnp.mean(x * x, axis=-1, keepdims=True)
    h = (x * lax.rsqrt(ms + NORM_EPS) * g_ref[...]).astype(BF16)
    for kind, w_ref, aux_ref in zip(kinds, w_refs, aux_refs):
        if kind == "dsa_t":
            vt_ref, iwt_ref = next(out_refs), next(out_refs)
            yt = _dot_t(w_ref[...], h)
            rows = lax.broadcasted_iota(I32, (LANES, yt.shape[1]), 0)
            vt = (yt[0:LANES] + jnp.where(rows >= HEAD_DIM, 1.0, 0.0)).astype(vt_ref.dtype)
            kbw = vt_ref.shape[2]
            for c in range(vt_ref.shape[0]):
                vt_ref[c] = vt[:, c * kbw:(c + 1) * kbw]
            iwt_ref[...] = yt[LANES:LANES + 8]
            continue
        o_ref = next(out_refs)
        width = w_ref.shape[1]
        cw = 256 if width % 256 == 0 else LANES
        for c0 in range(0, width, cw):
            y = _dot(h, w_ref[:, c0:c0 + cw])
            if kind == "norm":
                gsum = _split_dot(y * y, gs_ref[:cw, :cw])
                r = lax.rsqrt(gsum * (1.0 / HEAD_DIM) + NORM_EPS)
                mask = aux_ref[0:1, c0:c0 + cw]
                fac = mask * (r * aux_ref[1:2, c0:c0 + cw]) + (1.0 - mask)
                y = y * fac + aux_ref[2:3, c0:c0 + cw]
            elif kind == "sigmoid":
                y = 1.0 / (1.0 + jnp.exp(-y))
            o_ref[:, c0:c0 + cw] = y.astype(o_ref.dtype)


def _proj(x2d, gain, pieces, tm):
    m, d = x2d.shape
    kinds = tuple(p[0] for p in pieces)
    gs = (np.arange(256)[:, None] // HEAD_DIM == np.arange(256)[None, :] // HEAD_DIM)
    gs = jnp.asarray(gs, BF16)
    in_specs = [pl.BlockSpec((tm, d), lambda i: (i, 0)),
                pl.BlockSpec((1, d), lambda i: (0, 0)),
                pl.BlockSpec((256, 256), lambda i: (0, 0))]
    args = [x2d, gain.reshape(1, d).astype(F32), gs]
    out_specs, out_shapes = [], []
    kb = KB_TILES * TK
    for kind, w, aux, dt in pieces:
        in_specs += [pl.BlockSpec(w.shape, lambda i: (0, 0)), pl.BlockSpec(aux.shape, lambda i: (0, 0))]
        args += [w, aux]
        if kind == "dsa_t":
            out_specs += [pl.BlockSpec((tm // kb, LANES, kb), lambda i: (i, 0, 0)),
                          pl.BlockSpec((8, tm), lambda i: (0, i))]
            out_shapes += [jax.ShapeDtypeStruct((m // kb, LANES, kb), dt), jax.ShapeDtypeStruct((8, m), F32)]
            continue
        nw = w.shape[1]
        out_specs.append(pl.BlockSpec((tm, nw), lambda i: (i, 0)))
        out_shapes.append(jax.ShapeDtypeStruct((m, nw), dt))
    return pl.pallas_call(
        functools.partial(_proj_kernel, kinds),
        grid=(m // tm,),
        in_specs=in_specs, out_specs=out_specs, out_shape=out_shapes,
        compiler_params=_params(("parallel",)),
        name="proj",
    )(*args)


def _aux(width, mask=None, gain=None, add=None):
    z = jnp.zeros((width,), F32)
    return jnp.stack([z if mask is None else mask, z if gain is None else gain, z if add is None else add])


def _seg(*parts):
    ref = next(p for p in parts if not isinstance(p, int))
    return jnp.concatenate([jnp.zeros(ref.shape[:-1] + (p,), ref.dtype) if isinstance(p, int) else p
                            for p in parts], axis=-1)


def _proj_pieces(w_in, qk):
    w = w_in.astype(BF16)
    hd = HEAD_DIM
    ones, zeros = jnp.ones((hd,), F32), jnp.zeros((hd,), F32)
    cat = jnp.concatenate

    def cols(a, b):
        return w[:, a:b]

    pieces_ab = [
        ("norm", cols(_O_AQ, _O_AK), _aux(512, jnp.ones((512,), F32), jnp.tile(qk[0], A_HEADS) * (SCALE * LOG2E)),
         BF16),
        ("norm", _seg(cols(_O_AK, _O_AV), hd), _aux(LANES, cat([ones, zeros]), cat([qk[1], zeros])), BF16),
        ("dsa_t", _seg(cols(_O_AV, _O_IQ), hd, cols(_O_IW, _O_BQ), 16 - A_IDX_HEADS).T, _aux(LANES), BF16),
        ("plain", cols(_O_IQ, _O_IK), _aux(256), BF16),
        ("plain", _seg(cols(_O_IK, _O_IW), LANES - hd), _aux(LANES), BF16),
        ("norm", cols(_O_BQ, _O_BKV), _aux(512, jnp.ones((512,), F32), jnp.tile(qk[2], B_HEADS) * SCALE), BF16),
        ("plain", cols(_O_BKV, _O_BKV + 256), _aux(256), F32),
    ]
    o = _O_BKV + 256
    ks0, ks1, vs0, vs1, kw0, kw1, vw0, vw1 = [cols(o + i * hd, o + (i + 1) * hd) for i in range(8)]
    kmask = cat([ones, zeros] * 4 + [zeros, zeros] * 4)
    kgain = cat([qk[3], zeros] * 4 + [zeros, zeros] * 4)
    vadd = cat([zeros, zeros] * 4 + [zeros, ones] * 4)
    pieces_ab += [
        ("norm", _seg(ks0, hd, ks1, hd, kw0, hd, kw1, hd, vs0, hd, vs1, hd, vw0, hd, vw1, hd),
         _aux(1024, kmask, kgain, vadd), BF16),
        ("sigmoid", _seg(cols(_O_BG, _O_CQ), LANES - 3 * B_HEADS), _aux(LANES), F32),
    ]
    pieces_c = [
        ("norm", cols(_O_CQ, _O_CK), _aux(768, jnp.ones((768,), F32), jnp.tile(qk[4], C_HEADS) * SCALE), BF16),
        ("norm", cols(_O_CK, _O_CV), _aux(768, jnp.ones((768,), F32), jnp.tile(qk[5], C_HEADS)), BF16),
        ("plain", cols(_O_CV, _O_MIX), _aux(768), BF16),
    ]
    pieces_g = [("sigmoid", cols(_O_MIX, _O_END), _aux(3 * D_MODEL), F32)]
    return pieces_ab, pieces_c, pieces_g


def _flash_init(m_ref, acc_ref):
    m_ref[...] = jnp.full(m_ref.shape, NEG_INIT, F32)
    acc_ref[...] = jnp.zeros(acc_ref.shape, F32)


def _flash_block(q_ref, heads, group_of, kt, v_aug, bias_fn, masks, m_ref, acc_ref):
    heads = list(heads)
    s_all = [_dot_t(q_ref[h], kt[group_of(h)]) for h in heads]
    m_old = [m_ref[h] for h in heads]
    ps, alphas = [], []
    for k, h in enumerate(heads):
        s, mk = s_all[k], masks[group_of(h)]
        sc = [jnp.where(mk[c], s[:, c * TK:(c + 1) * TK] + bias_fn(h, c), -jnp.inf) for c in range(len(mk))]
        m_new = jnp.maximum(m_old[k], jnp.max(functools.reduce(jnp.maximum, sc), axis=-1, keepdims=True))
        ps.append(jnp.concatenate([jnp.exp(x - m_new).astype(BF16) for x in sc], axis=1))
        alphas.append(jnp.exp(m_old[k] - m_new))
        m_ref[h] = m_new
    for k, h in enumerate(heads):
        acc_ref[h] = alphas[k] * acc_ref[h] + _dot(ps[k], v_aug[group_of(h)])


def _flash_block_t(q_ref, slots, kt, v_aug_t, bias_fn, masks, m_ref, acc_ref):
    slots = list(slots)
    s_all = [_dot_t(kt, q_ref[p]) for p in slots]
    m_old = [m_ref[p][0:1] for p in slots]
    ps, alphas = [], []
    for k, p in enumerate(slots):
        s = s_all[k]
        sc = [jnp.where(masks[c], s[c * TK:(c + 1) * TK] + bias_fn(p, c), -jnp.inf) for c in range(len(masks))]
        m_new = jnp.maximum(m_old[k], jnp.max(functools.reduce(jnp.maximum, sc), axis=0, keepdims=True))
        ps.append(jnp.concatenate([jnp.exp2(x - m_new).astype(BF16) for x in sc], axis=0))
        alphas.append(jnp.exp2(m_old[k] - m_new))
        m_ref[p] = jnp.broadcast_to(m_new, m_ref.shape[1:])
    for k, p in enumerate(slots):
        acc_ref[p] = alphas[k] * acc_ref[p] + _dot(v_aug_t, ps[k])


def _flash_out(acc):
    return acc[:, :HEAD_DIM] / jnp.maximum(acc[:, HEAD_DIM:HEAD_DIM + 1], 1e-30)


def _split_heads(q_ref, qs_ref, nh):
    for h in range(nh):
        qs_ref[h] = q_ref[:, h * HEAD_DIM:(h + 1) * HEAD_DIM]


def _dsa_kernel(topk, nd, pbits, iq_ref, iwt_ref, ik_ref, aq_ref, ak_ref, avt_ref, bias_ref, o_ref,
                keys_ref, half_ref, iqs_ref, qs_ref, m_ref, acc_ref):
    TQ = A_TQ
    i = pl.program_id(1)
    nk = (i + 1) * (TQ // TK)
    nkb = (nk + KB_TILES - 1) // KB_TILES
    kb = KB_TILES * TK
    krow = lax.broadcasted_iota(I32, (TK, TQ), 0)
    qpos = i * TQ + lax.broadcasted_iota(I32, (TK, TQ), 1)
    i16 = jnp.int16
    last = keys_ref.shape[0] * TK - 1

    iwt = iwt_ref[...]
    for h in range(A_IDX_HEADS):
        iqs_ref[h * TQ:(h + 1) * TQ] = iq_ref[:, h * HEAD_DIM:(h + 1) * HEAD_DIM]

    def score_block(jb, c):
        r0 = pl.multiple_of(jb * kb, kb)
        d = _dot_t(ik_ref[pl.ds(r0, kb), 0:HEAD_DIM], iqs_ref[...])
        acc = jnp.zeros((kb, TQ), F32)
        for h in range(A_IDX_HEADS):
            acc = acc + jnp.maximum(d[:, h * TQ:(h + 1) * TQ], 0.0) * iwt[h:h + 1]
        bits = lax.bitcast_convert_type(acc, I32)
        key = jnp.where(bits < 0, bits ^ 0x7FFFFFFF, bits + (last + 1))
        for t in range(KB_TILES):
            j = jb * KB_TILES + t
            kidx = j * TK + krow
            kj = jnp.where(acc[t * TK:(t + 1) * TK] == 0.0, last - kidx, key[t * TK:(t + 1) * TK])
            kj = jnp.where(kidx <= qpos, kj, INT_MIN)
            keys_ref[j] = kj
            half_ref[j] = jnp.right_shift(kj, 16).astype(i16)
        return c

    lax.fori_loop(0, nkb, score_block, 0)

    kf = float(topk)

    nb_max = keys_ref.shape[0] // KB_TILES
    sub = 16

    def search_half(nbits, u0):
        def run(nblk, u_init):
            def step(b, u):
                cand = u | jnp.left_shift(jnp.int32(1), nbits - 1 - b)
                cb = jnp.broadcast_to((cand - 32768).astype(i16), (TK, TQ))
                acc = jnp.zeros((sub, TQ), i16)
                for j in range(nblk * KB_TILES):
                    hit = jnp.where(half_ref[j] >= cb, jnp.ones((), i16), jnp.zeros((), i16))
                    acc = acc + functools.reduce(lambda a, b2: a + b2,
                                                 [hit[r:r + sub] for r in range(0, TK, sub)])
                cnt = jnp.sum(acc.astype(F32), axis=0, keepdims=True)
                return jnp.where(cnt >= kf, cand, u)

            return lax.fori_loop(0, nbits, step, u_init)

        return lax.switch(nkb - 1, [functools.partial(run, n) for n in range(1, nb_max + 1)], u0)

    def for_tiles(fn):
        def body(jb, c):
            for t in range(KB_TILES):
                fn(jb * KB_TILES + t)
            return c
        lax.fori_loop(0, nkb, body, 0)

    zero = jnp.zeros((1, TQ), I32)
    t_hi = search_half(16, zero) - 32768

    def low_tile(j):
        k = keys_ref[j]
        hi = jnp.right_shift(k, 16)
        lo = (k & 0xFFFF) - 32768
        half_ref[j] = jnp.where(hi > t_hi, 32767, jnp.where(hi < t_hi, -32768, lo)).astype(i16)

    for_tiles(low_tile)
    thr = t_hi * 65536 + search_half(16, zero)

    def count_ge(jb, acc):
        for t in range(KB_TILES):
            acc = acc + jnp.where(keys_ref[jb * KB_TILES + t] >= thr, 1.0, 0.0)
        return acc

    n_ge = jnp.sum(lax.fori_loop(0, nkb, count_ge, jnp.zeros((TK, TQ), F32)), axis=0, keepdims=True)

    @pl.when(jnp.max(jnp.where(thr > INT_MIN, n_ge, 0.0)) > kf)
    def _():
        def tie_tile(j):
            k = keys_ref[j]
            rev = last - (j * TK + krow)
            half_ref[j] = jnp.where(k > thr, 32767, jnp.where(k == thr, rev, -32768)).astype(i16)

        for_tiles(tie_tile)
        keep = search_half(pbits, zero + 32768) - 32768

        def demote(j):
            k = keys_ref[j]
            rev = last - (j * TK + krow)
            keys_ref[j] = jnp.where((k == thr) & (rev < keep) & (thr > INT_MIN), k - 1, k)

        for_tiles(demote)

    sel_thr = jnp.maximum(thr, INT_MIN + 1)

    nslot = A_HEADS // A_PAIR
    for h in range(A_HEADS):
        qs_ref[h // A_PAIR, (h % A_PAIR) * TQ:(h % A_PAIR + 1) * TQ] = aq_ref[:, h * HEAD_DIM:(h + 1) * HEAD_DIM]
    _flash_init(m_ref, acc_ref)

    def att_block(jb, c):
        r0 = pl.multiple_of(jb * kb, kb)
        kt = ak_ref[pl.ds(r0, kb), 0:HEAD_DIM]
        j0 = jb * KB_TILES
        masks = []
        for t in range(KB_TILES):
            mk = keys_ref[j0 + t] >= sel_thr
            masks.append(jnp.concatenate([mk] * A_PAIR, axis=1))
        d = [_bias_tile_index(i, j0 + t, TQ, nd) * nslot for t in range(KB_TILES)]
        _flash_block_t(qs_ref, range(nslot), kt, avt_ref[jb], lambda p, t: bias_ref[d[t] + p], masks,
                       m_ref, acc_ref)
        return c

    lax.fori_loop(0, nkb, att_block, 0)
    outs = []
    for h in range(A_HEADS):
        acc = acc_ref[h // A_PAIR][:, (h % A_PAIR) * TQ:(h % A_PAIR + 1) * TQ]
        outs.append(acc[:HEAD_DIM] / jnp.maximum(acc[HEAD_DIM:HEAD_DIM + 1], 1e-30))
    o_ref[...] = jnp.concatenate(outs, axis=0).T.astype(o_ref.dtype)


def _dsa(iq, iwt, ik, aq, ak, avt, bias, nd, bsz, seq):
    TQ = A_TQ
    nq = seq // TQ
    kb = KB_TILES * TK
    topk = min(A_TOPK_MAX, seq // 4)
    pbits = max(1, (seq - 1).bit_length())
    return pl.pallas_call(
        functools.partial(_dsa_kernel, topk, nd, pbits),
        grid=(bsz, nq),
        in_specs=[
            pl.BlockSpec((TQ, 256), lambda b, i: (b * nq + i, 0)),
            pl.BlockSpec((8, TQ), lambda b, i: (0, b * nq + i)),
            pl.BlockSpec((seq, LANES), lambda b, i: (b, 0)),
            pl.BlockSpec((TQ, 512), lambda b, i: (b * nq + i, 0)),
            pl.BlockSpec((seq, LANES), lambda b, i: (b, 0)),
            pl.BlockSpec((seq // kb, LANES, kb), lambda b, i: (b, 0, 0)),
            pl.BlockSpec(bias.shape, lambda b, i: (0, 0, 0), pipeline_mode=pl.Buffered(1)),
        ],
        out_specs=pl.BlockSpec((TQ, 512), lambda b, i: (b * nq + i, 0)),
        out_shape=jax.ShapeDtypeStruct((bsz * seq, 512), BF16),
        scratch_shapes=[pltpu.VMEM((seq // TK, TK, TQ), I32),
                        pltpu.VMEM((seq // TK, TK, TQ), jnp.int16),
                        pltpu.VMEM((A_IDX_HEADS * TQ, HEAD_DIM), BF16),
                        pltpu.VMEM((A_HEADS // A_PAIR, A_PAIR * TQ, HEAD_DIM), BF16),
                        pltpu.VMEM((A_HEADS // A_PAIR, 8, A_PAIR * TQ), F32),
                        pltpu.VMEM((A_HEADS // A_PAIR, LANES, A_PAIR * TQ), F32)],
        compiler_params=_params(("parallel", "arbitrary")),
        name="dsa",
    )(iq, iwt, ik, aq, ak, avt, bias)


def _nsa_cmp_kernel(x_ref, pos_ref, wlo_ref, whi_ref, gain_ref, k_ref, v_ref):
    x = x_ref[...]
    lo = _dot((x + pos_ref[0:1, :]).astype(BF16), wlo_ref[...])
    hi = _dot((x + pos_ref[1:2, :]).astype(BF16), whi_ref[...])
    nrow = x.shape[0]
    pre = lo + pltpu.roll(hi, nrow - 1, 0)
    ks = []
    for g in range(B_KV_HEADS):
        kg = pre[:, g * HEAD_DIM:(g + 1) * HEAD_DIM]
        ms = jnp.mean(kg * kg, axis=-1, keepdims=True)
        ks.append(kg * lax.rsqrt(ms + NORM_EPS) * gain_ref[...])
    k_ref[...] = jnp.concatenate(ks, axis=1).astype(k_ref.dtype)
    v_ref[...] = pre[:, LANES:2 * LANES].astype(v_ref.dtype)


def _nsa_cmp(bcmp, cmp_pos, cmp_w, k_gain, bsz, seq):
    nch = seq // B_CMP_STRIDE
    half = B_CMP_LEN // 2
    width = half * 256
    x = bcmp.reshape(bsz * nch, width)

    def wmat(l0):
        w = jnp.zeros((half, 4, HEAD_DIM, 4, HEAD_DIM), F32)
        for j in range(4):
            w = w.at[:, j, :, j, :].set(cmp_w[j // 2, l0:l0 + half])
        return w.reshape(width, 256).astype(BF16)

    def prow(l0):
        p = jnp.stack([cmp_pos[0, l0:l0 + half], cmp_pos[0, l0:l0 + half],
                       cmp_pos[1, l0:l0 + half], cmp_pos[1, l0:l0 + half]], axis=1)
        return p.reshape(width)

    pos = jnp.stack([prow(0), prow(half)]).astype(F32)
    return pl.pallas_call(
        _nsa_cmp_kernel,
        grid=(bsz,),
        in_specs=[
            pl.BlockSpec((nch, width), lambda b: (b, 0)),
            pl.BlockSpec((2, width), lambda b: (0, 0)),
            pl.BlockSpec((width, 256), lambda b: (0, 0)),
            pl.BlockSpec((width, 256), lambda b: (0, 0)),
            pl.BlockSpec((1, HEAD_DIM), lambda b: (0, 0)),
        ],
        out_specs=[pl.BlockSpec((nch, LANES), lambda b: (b, 0)), pl.BlockSpec((nch, LANES), lambda b: (b, 0))],
        out_shape=[jax.ShapeDtypeStruct((bsz * nch, LANES), BF16)] * 2,
        compiler_params=_params(("parallel",)),
        name="nsa_cmp",
    )(x, pos, wmat(0), wmat(half), k_gain.reshape(1, HEAD_DIM).astype(F32))


def _nsa_kernel(seq, nd, q_ref, g_ref, kc_ref, vc_ref, kv_ref, bias_ref, o_ref,
                imp_ref, sel_ref, qs_ref, m_ref, acc_ref):
    i = pl.program_id(1)
    ncp = seq // B_CMP_STRIDE
    ns = seq // B_SEL_LEN
    n_top = min(B_SEL_TOPK_MAX, ns)
    hg = B_GROUP
    row = lax.broadcasted_iota(I32, (TQ, TK), 0)
    col = lax.broadcasted_iota(I32, (TQ, TK), 1)
    tpos = i * TQ + row
    _split_heads(q_ref, qs_ref, B_HEADS)

    n_idx = lax.broadcasted_iota(I32, (ncp, TQ), 0)
    t_c = i * TQ + lax.broadcasted_iota(I32, (ncp, TQ), 1)
    cmask = n_idx * B_CMP_STRIDE + (B_CMP_LEN - 1) <= t_c
    om = lax.broadcasted_iota(I32, (ns, ncp), 0) * B_SEL_LEN
    on = lax.broadcasted_iota(I32, (ns, ncp), 1) * B_CMP_STRIDE
    ovt = jnp.where((on < om + B_SEL_LEN) & (on + B_CMP_LEN > om), 1.0, 0.0).astype(BF16)
    m_idx = lax.broadcasted_iota(I32, (ns, TQ), 0)
    jt = (i * TQ + lax.broadcasted_iota(I32, (ns, TQ), 1)) // B_SEL_LEN
    forced = (m_idx == 0) | (m_idx == jt) | (m_idx == jt - 1)

    kc = [kc_ref[:, g * HEAD_DIM:(g + 1) * HEAD_DIM] for g in range(B_KV_HEADS)]
    vc = [vc_ref[:, g * HEAD_DIM:(g + 1) * HEAD_DIM] for g in range(B_KV_HEADS)]
    st_all = [_dot_t(kc[h // hg], qs_ref[h]) for h in range(B_HEADS)]
    pts = []
    for h in range(B_HEADS):
        st = jnp.where(cmask, st_all[h], -jnp.inf)
        mx = jnp.max(st, axis=0, keepdims=True)
        mx = jnp.where(mx == -jnp.inf, 0.0, mx)
        e = jnp.exp(st - mx)
        pts.append(e / jnp.maximum(jnp.sum(e, axis=0, keepdims=True), 1e-30))
    oc = [_dot(pts[h].T.astype(BF16), vc[h // hg]) for h in range(B_HEADS)]

    for g in range(B_KV_HEADS):
        psum = functools.reduce(lambda a, b: a + b, pts[g * hg:(g + 1) * hg])
        hi = psum.astype(BF16)
        lo = (psum - hi.astype(F32)).astype(BF16)
        imp = _dot(ovt, hi) + _dot(ovt, lo)
        imp = jnp.where(forced, jnp.inf, jnp.where(m_idx <= jt, imp, -jnp.inf))
        imp_ref[g] = imp

        def rank_step(mp, rank, g=g, imp=imp):
            vp = jnp.broadcast_to(imp_ref[g, pl.ds(mp, 1), :], (ns, TQ))
            before = (vp > imp) | ((vp == imp) & (mp < m_idx))
            return rank + jnp.where(before, 1.0, 0.0)

        rank = lax.fori_loop(0, ns, rank_step, jnp.zeros((ns, TQ), F32), unroll=8)
        selt = jnp.where((rank < float(n_top)) & (m_idx <= jt), 1.0, 0.0)
        selt = jnp.concatenate([selt, jnp.zeros((LANES - ns, TQ), F32)], axis=0)
        sel_ref[g] = selt.T.astype(BF16)

    def branch_block(j0, ntiles, koff, voff, mask_fn):
        r0 = pl.multiple_of(j0 * TK, TK)
        d = [_bias_tile_index(i, j0 + t, TQ, nd) * B_HEADS for t in range(ntiles)]
        rows = pl.ds(r0, ntiles * TK)
        kt = [kv_ref[rows, koff + g * LANES:koff + g * LANES + HEAD_DIM] for g in range(B_KV_HEADS)]
        va = [kv_ref[rows, voff + g * LANES:voff + (g + 1) * LANES] for g in range(B_KV_HEADS)]
        masks = [mask_fn(g, j0) for g in range(B_KV_HEADS)]
        _flash_block(qs_ref, range(B_HEADS), lambda h: h // hg, kt, va, lambda h, t: bias_ref[d[t] + h],
                     masks, m_ref, acc_ref)

    kb = KB_TILES * TK
    e_m = lax.broadcasted_iota(I32, (LANES, kb), 0)
    e_c = lax.broadcasted_iota(I32, (LANES, kb), 1) // B_SEL_LEN

    def sel_masks(g, j0):
        expand = jnp.where(e_m == j0 * (TK // B_SEL_LEN) + e_c, 1.0, 0.0).astype(BF16)
        chosen = _dot(sel_ref[g], expand) > 0.5
        return [chosen[:, t * TK:(t + 1) * TK] & ((j0 + t) * TK + col <= tpos) for t in range(KB_TILES)]

    def sel_body(jb, c):
        branch_block(jb * KB_TILES, KB_TILES, 0, 4 * LANES, sel_masks)
        return c

    gates = g_ref[...]

    def gate(h, br):
        return gates[:, 3 * h + br:3 * h + br + 1]

    _flash_init(m_ref, acc_ref)
    lax.fori_loop(0, (i + KB_TILES) // KB_TILES, sel_body, 0)
    part = [gate(h, 0) * oc[h] + gate(h, 1) * _flash_out(acc_ref[h]) for h in range(B_HEADS)]

    wt = B_WINDOW // TK + 1
    w0 = jnp.maximum(i + 1 - wt, 0)

    def win_masks(g, j0):
        out = []
        for t in range(wt):
            dist = tpos - ((j0 + t) * TK + col)
            out.append((dist >= 0) & (dist < B_WINDOW))
        return out

    _flash_init(m_ref, acc_ref)
    branch_block(w0, wt, 2 * LANES, 6 * LANES, win_masks)
    outs = [part[h] + gate(h, 2) * _flash_out(acc_ref[h]) for h in range(B_HEADS)]
    o_ref[...] = jnp.concatenate(outs, axis=1).astype(o_ref.dtype)


def _nsa(bq, bg, kcmp, vcmp, bsw, bias, nd, bsz, seq):
    nq = seq // TQ
    ncp = seq // B_CMP_STRIDE
    ns = seq // B_SEL_LEN
    return pl.pallas_call(
        functools.partial(_nsa_kernel, seq, nd),
        grid=(bsz, nq),
        in_specs=[
            pl.BlockSpec((TQ, 512), lambda b, i: (b * nq + i, 0)),
            pl.BlockSpec((TQ, LANES), lambda b, i: (b * nq + i, 0)),
            pl.BlockSpec((ncp, LANES), lambda b, i: (b, 0)),
            pl.BlockSpec((ncp, LANES), lambda b, i: (b, 0)),
            pl.BlockSpec((seq, 1024), lambda b, i: (b, 0)),
            pl.BlockSpec(bias.shape, lambda b, i: (0, 0, 0)),
        ],
        out_specs=pl.BlockSpec((TQ, 512), lambda b, i: (b * nq + i, 0)),
        out_shape=jax.ShapeDtypeStruct((bsz * seq, 512), BF16),
        scratch_shapes=[pltpu.VMEM((B_KV_HEADS, ns, TQ), F32), pltpu.VMEM((B_KV_HEADS, TQ, LANES), BF16),
                        pltpu.VMEM((B_HEADS, TQ, HEAD_DIM), BF16),
                        pltpu.VMEM((B_HEADS, TQ, LANES), F32),
                        pltpu.VMEM((B_HEADS, TQ, LANES), F32)],
        compiler_params=_params(("parallel", "arbitrary")),
        name="nsa",
    )(bq, bg, kcmp, vcmp, bsw, bias)


def _dil_kernel(q_ref, kp_ref, kc_ref, vp_ref, vc_ref, bias_ref, o_ref, lse_ref):
    i = pl.program_id(2)
    q = q_ref[...]
    k2 = jnp.concatenate([kp_ref[...], kc_ref[...]], axis=0)
    v2 = jnp.concatenate([vp_ref[...], vc_ref[...]], axis=0)
    row = lax.broadcasted_iota(I32, (TQ, 2 * TK), 0)
    col = lax.broadcasted_iota(I32, (TQ, 2 * TK), 1)
    du = row + TK - col
    valid = (du >= 0) & (du <= TK) & ((col >= TK) | (i > 0))
    for hh in range(C_HEADS_PER_GROUP):
        hs = slice(hh * HEAD_DIM, (hh + 1) * HEAD_DIM)
        s = jnp.where(valid, _dot_t(q[:, hs], k2[:, hs]) + bias_ref[hh], -jnp.inf)
        m = jnp.max(s, axis=-1, keepdims=True)
        e = jnp.exp(s - m)
        den = jnp.sum(e, axis=-1, keepdims=True)
        o_ref[:, hs] = _dot(e.astype(BF16), v2[:, hs]) / den
        lse_ref[:, hs] = jnp.broadcast_to(m + jnp.log(den), (TQ, HEAD_DIM))


def _dilated_group(cq, ck, cv, bias, g, dil, bsz, seq):
    ln = seq // dil
    nq = ln // TQ
    width = C_HEADS * HEAD_DIM
    gw = C_HEADS_PER_GROUP * HEAD_DIM
    ncb = width // gw
    views = [a.reshape(bsz * ln, dil * width) for a in (cq, ck, cv)]

    def cur(b, r, i):
        return (b * nq + i, r * ncb + g)

    def prev(b, r, i):
        return (b * nq + jnp.maximum(i - 1, 0), r * ncb + g)

    blk = (TQ, gw)
    o, lse = pl.pallas_call(
        _dil_kernel,
        grid=(bsz, dil, nq),
        in_specs=[pl.BlockSpec(blk, cur), pl.BlockSpec(blk, prev), pl.BlockSpec(blk, cur),
                  pl.BlockSpec(blk, prev), pl.BlockSpec(blk, cur),
                  pl.BlockSpec(bias.shape, lambda b, r, i: (0, 0, 0))],
        out_specs=[pl.BlockSpec(blk, lambda b, r, i: (b * nq + i, r))] * 2,
        out_shape=[jax.ShapeDtypeStruct((bsz * ln, dil * gw), F32)] * 2,
        compiler_params=_params(("parallel", "parallel", "arbitrary")),
        name=f"dilated_d{dil}",
    )(views[0], views[1], views[1], views[2], views[2], bias)
    return o.reshape(bsz * seq, gw), lse.reshape(bsz * seq, gw)


def _merge_kernel(x_ref, ya_ref, yb_ref, o0_ref, l0_ref, o1_ref, l1_ref, o2_ref, l2_ref, g_ref,
                  wa_ref, wb_ref, wc_ref, wo_ref, out_ref):
    l0, l1, l2 = l0_ref[...], l1_ref[...], l2_ref[...]
    mx = jnp.maximum(jnp.maximum(l0, l1), l2)
    e0, e1, e2 = jnp.exp(l0 - mx), jnp.exp(l1 - mx), jnp.exp(l2 - mx)
    yc = (e0 * o0_ref[...] + e1 * o1_ref[...] + e2 * o2_ref[...]) / (e0 + e1 + e2)
    ya = _dot(ya_ref[...], wa_ref[...])
    yb = _dot(yb_ref[...], wb_ref[...])
    yc = _dot(yc.astype(BF16), wc_ref[...])
    d = D_MODEL
    z = g_ref[:, 0:d] * ya + g_ref[:, d:2 * d] * yb + g_ref[:, 2 * d:3 * d] * yc
    out_ref[...] = x_ref[...] + _dot(z.astype(BF16), wo_ref[...])


def _merge(x2d, ya, yb, c_outs, mixg, wa, wb, wc, wo, tm):
    m = x2d.shape[0]

    def rows(w):
        return pl.BlockSpec((tm, w), lambda i: (i, 0))

    def full(a):
        return pl.BlockSpec(a.shape, lambda i: (0, 0))

    c_flat = [a for pair in c_outs for a in pair]
    return pl.pallas_call(
        _merge_kernel,
        grid=(m // tm,),
        in_specs=[rows(D_MODEL), rows(512), rows(512)] + [rows(256)] * 6 + [rows(3 * D_MODEL)]
                 + [full(wa), full(wb), full(wc), full(wo)],
        out_specs=rows(D_MODEL),
        out_shape=jax.ShapeDtypeStruct((m, D_MODEL), F32),
        compiler_params=_params(("parallel",)),
        name="merge",
    )(x2d, ya, yb, *c_flat, mixg, wa, wb, wc, wo)


def _ffn_kernel(x_ref, g_ref, wg_ref, wu_ref, wd_ref, out_ref):
    x = x_ref[...]
    ms = jnp.mean(x * x, axis=-1, keepdims=True)
    h = (x * lax.rsqrt(ms + NORM_EPS) * g_ref[...]).astype(BF16)
    gate = _dot(h, wg_ref[...])
    up = _dot(h, wu_ref[...])
    act = gate / (1.0 + jnp.exp(-gate)) * up
    out_ref[...] = x + _dot(act.astype(BF16), wd_ref[...])


def _ffn(x2d, gain, w_in, w_out, tm):
    m = x2d.shape[0]
    wg = w_in[:, :D_FF].astype(BF16)
    wu = w_in[:, D_FF:].astype(BF16)
    wd = w_out.astype(BF16)

    def full(a):
        return pl.BlockSpec(a.shape, lambda i: (0, 0))

    return pl.pallas_call(
        _ffn_kernel,
        grid=(m // tm,),
        in_specs=[pl.BlockSpec((tm, D_MODEL), lambda i: (i, 0)), pl.BlockSpec((1, D_MODEL), lambda i: (0, 0)),
                  full(wg), full(wu), full(wd)],
        out_specs=pl.BlockSpec((tm, D_MODEL), lambda i: (i, 0)),
        out_shape=jax.ShapeDtypeStruct((m, D_MODEL), F32),
        compiler_params=_params(("parallel",)),
        name="ffn",
    )(x2d, gain.reshape(1, D_MODEL).astype(F32), wg, wu, wd)


def _layer(x2d, bsz, seq, norm1_g, norm2_g, w_in, qk, cmp_pos, cmp_w, w_a, w_b, w_c, w_out, w_ffn_in, w_ffn_out,
           bias_a, bias_b, bias_c):
    pieces_ab, pieces_c, pieces_g = _proj_pieces(w_in, qk)
    aq, ak, avt, iwt, iq, ik, bq, bcmp, bsw, bg = _proj(x2d, norm1_g, pieces_ab, 512)
    cq, ck, cv = _proj(x2d, norm1_g, pieces_c, 512)
    (mixg,) = _proj(x2d, norm1_g, pieces_g, 512)

    ya = _dsa(iq, iwt, ik, aq, ak, avt, *bias_a, bsz, seq)
    kcmp, vcmp = _nsa_cmp(bcmp, cmp_pos, cmp_w, qk[3], bsz, seq)
    yb = _nsa(bq, bg, kcmp, vcmp, bsw, *bias_b, bsz, seq)
    c_outs = [_dilated_group(cq, ck, cv, bias_c[g], g, dil, bsz, seq) for g, (_, dil) in enumerate(C_GROUPS)]

    x1 = _merge(x2d, ya, yb, c_outs, mixg, w_a.astype(BF16), w_b.astype(BF16), w_c.astype(BF16),
                w_out.astype(BF16), 256)
    return _ffn(x1, norm2_g, w_ffn_in, w_ffn_out, 256)


def kernel(x, norm1_g, norm2_g, w_in, qk_norm_g, nsa_cmp_pos, nsa_cmp_w, w_branch_a, w_branch_b, w_branch_c, w_out, w_ffn_in, w_ffn_out, rel_bias):
    bsz, seq, d = x.shape
    assert d == D_MODEL and seq % (TQ * max(dil for _, dil in C_GROUPS)) == 0 and seq % A_TQ == 0
    assert seq % (KB_TILES * TK) == 0 and seq >= B_WINDOW + TK
    for win, dil in C_GROUPS:
        assert win == TK * dil
    bias_a = _toeplitz_bias(rel_bias[:, :A_HEADS], seq, A_TQ, keys_on_rows=True, group=A_PAIR, scale=LOG2E)
    bias_b = _toeplitz_bias(rel_bias[:, A_HEADS:A_HEADS + B_HEADS], seq)
    rel_c = rel_bias[:, A_HEADS + B_HEADS:]
    bias_c = [_dilated_bias(rel_c[:, g * C_HEADS_PER_GROUP:(g + 1) * C_HEADS_PER_GROUP], dil)
              for g, (_, dil) in enumerate(C_GROUPS)]
    x2d = x.reshape(bsz * seq, d)
    for layer in range(norm1_g.shape[0]):
        x2d = _layer(x2d, bsz, seq, norm1_g[layer], norm2_g[layer], w_in[layer], qk_norm_g[layer],
                     nsa_cmp_pos[layer], nsa_cmp_w[layer], w_branch_a[layer], w_branch_b[layer],
                     w_branch_c[layer], w_out[layer], w_ffn_in[layer], w_ffn_out[layer],
                     bias_a, bias_b, bias_c)
    return x2d.reshape(bsz, seq, d)
```

```python
import functools
import math

import numpy as np
import jax
import jax.numpy as jnp
from jax import lax
from jax.experimental import pallas as pl
from jax.experimental.pallas import tpu as pltpu

F32 = jnp.float32
BF16 = jnp.bfloat16
I32 = jnp.int32

D_MODEL = 1024
HEAD_DIM = 64
NORM_EPS = 1e-6
REL_BUCKETS = 32
REL_MAX_DIST = 2048

A_HEADS = 8
A_IDX_HEADS = 4
A_TOPK_MAX = 256
B_HEADS = 8
B_KV_HEADS = 2
B_GROUP = B_HEADS // B_KV_HEADS
B_CMP_LEN = 32
B_CMP_STRIDE = 16
B_SEL_LEN = 64
B_SEL_TOPK_MAX = 16
B_WINDOW = 512
C_GROUPS = ((128, 1), (512, 4), (2048, 16))
C_HEADS_PER_GROUP = 4
C_HEADS = C_HEADS_PER_GROUP * len(C_GROUPS)
D_FF = ((8 * D_MODEL + 3 * 256 - 1) // (3 * 256)) * 256

_O_AQ = 0
_O_AK = _O_AQ + A_HEADS * HEAD_DIM
_O_AV = _O_AK + HEAD_DIM
_O_IQ = _O_AV + HEAD_DIM
_O_IK = _O_IQ + A_IDX_HEADS * HEAD_DIM
_O_IW = _O_IK + HEAD_DIM
_O_BQ = _O_IW + A_IDX_HEADS
_O_BKV = _O_BQ + B_HEADS * HEAD_DIM
_O_BG = _O_BKV + 6 * B_KV_HEADS * HEAD_DIM
_O_CQ = _O_BG + 3 * B_HEADS
_O_CK = _O_CQ + C_HEADS * HEAD_DIM
_O_CV = _O_CK + C_HEADS * HEAD_DIM
_O_MIX = _O_CV + C_HEADS * HEAD_DIM
_O_END = _O_MIX + 3 * D_MODEL

TQ = 128
TK = 128
KB_TILES = 4
A_TQ = 256
A_PAIR = 2
B_TQ = 256
B_PAIR = 2
LOG2E = 1.4426950408889634
LANES = 128
VMEM_LIMIT = 56 * 1024 * 1024
INT_MIN = -2 ** 31
NEG_INIT = -1e30
SCALE = HEAD_DIM ** -0.5


def _params(sem):
    return pltpu.CompilerParams(dimension_semantics=sem, vmem_limit_bytes=VMEM_LIMIT)


def _dot_t(a, b):
    return lax.dot_general(a, b, (((1,), (1,)), ((), ())), preferred_element_type=F32)


def _dot(a, b):
    return jnp.dot(a, b, preferred_element_type=F32)


def _split_dot_left(a_bf16, b):
    hi = b.astype(BF16)
    lo = (b - hi.astype(F32)).astype(BF16)
    return _dot(a_bf16, hi) + _dot(a_bf16, lo)


def _split_dot(a, b_bf16):
    hi = a.astype(BF16)
    lo = (a - hi.astype(F32)).astype(BF16)
    return _dot(hi, b_bf16) + _dot(lo, b_bf16)


def _bucket_table(n_max):
    n = np.arange(n_max, dtype=np.int64)
    exact = REL_BUCKETS // 2
    nf = np.maximum(n, 1).astype(np.float32)
    large = exact + (np.log(nf / np.float32(exact)) / np.float32(math.log(REL_MAX_DIST / exact))
                     * np.float32(REL_BUCKETS - exact)).astype(np.int32)
    return np.where(n < exact, n, np.minimum(large, REL_BUCKETS - 1)).astype(np.int32)


def _num_bias_tiles(seq, tq):
    bucket = _bucket_table(seq + tq)
    first_sat = int(np.min(np.nonzero(bucket == REL_BUCKETS - 1)[0]))
    assert np.all(bucket[first_sat:] == REL_BUCKETS - 1)
    nd = -(-(first_sat + TK - 1) // TK) + tq // TK
    return min(nd, seq // TK)


def _bias_tile_index(i, j, tq, nd):
    return jnp.clip((tq // TK) * (i + 1) - 1 - j, 0, nd - 1)


def _bias_kernel(nh, group, scale, idx_ref, rel_ref, o_ref):
    idx = idx_ref[0]
    c = idx.shape[1]
    acc = [jnp.zeros(idx.shape, F32) for _ in range(nh)]
    for b in range(REL_BUCKETS):
        hit = idx == b
        for h in range(nh):
            acc[h] = jnp.where(hit, rel_ref[b, h] * scale, acc[h])
    for h in range(nh):
        o_ref[h // group, :, (h % group) * c:(h % group + 1) * c] = acc[h]


def _bias_tiles(rel_cols, idx, group=1, scale=1.0):
    n, r, c = idx.shape
    nh = rel_cols.shape[1]
    return pl.pallas_call(
        functools.partial(_bias_kernel, nh, group, scale),
        grid=(n,),
        in_specs=[pl.BlockSpec((1, r, c), lambda k: (k, 0, 0)),
                  pl.BlockSpec(memory_space=pltpu.SMEM)],
        out_specs=pl.BlockSpec((nh // group, r, group * c), lambda k: (k, 0, 0)),
        out_shape=jax.ShapeDtypeStruct((n * nh // group, r, group * c), F32),
        compiler_params=_params(("parallel",)),
        name="bias_tiles",
    )(jnp.asarray(idx, I32), rel_cols.astype(F32))


def _toeplitz_bias(rel_cols, seq, tq=TQ, keys_on_rows=False, group=1, scale=1.0):
    nd = _num_bias_tiles(seq, tq)
    bucket = _bucket_table(seq + tq)
    d = ((np.arange(nd)[:, None, None] - (tq // TK - 1)) * TK
         + np.arange(tq)[None, :, None] - np.arange(TK)[None, None, :])
    idx = bucket[np.clip(d, 0, None)]
    return _bias_tiles(rel_cols, idx.transpose(0, 2, 1) if keys_on_rows else idx, group, scale), nd


def _dilated_bias(rel_cols, dil):
    bucket = _bucket_table(2 * TK * dil + 1)
    du = np.arange(TQ)[:, None] + TK - np.arange(2 * TK)[None, :]
    return _bias_tiles(rel_cols, bucket[np.clip(du, 0, None) * dil][None])


def _proj_kernel(kinds, *refs):
    n = len(kinds)
    x_ref, g_ref, gs_ref = refs[0], refs[1], refs[2]
    w_refs = refs[3:3 + 2 * n:2]
    aux_refs = refs[4:4 + 2 * n:2]
    out_refs = iter(refs[3 + 2 * n:])
    x = x_ref[...]
    ms = jnp.mean(x * x, axis=-1, keepdims=True)
    h = (x * lax.rsqrt(ms + NORM_EPS) * g_ref[...]).astype(BF16)
    for kind, w_ref, aux_ref in zip(kinds, w_refs, aux_refs):
        if isinstance(kind, tuple):
            _, nv, nextra, act = kind
            vt_ref, ex_ref = next(out_refs), next(out_refs)
            yt = _dot_t(w_ref[...], h)
            rows = lax.broadcasted_iota(I32, (nv * LANES, yt.shape[1]), 0)
            vt = (yt[0:nv * LANES] + jnp.where(rows % LANES >= HEAD_DIM, 1.0, 0.0)).astype(vt_ref.dtype)
            for c in range(vt_ref.shape[0]):
                vt_ref[c] = vt[:, c * TK:(c + 1) * TK]
            ex = yt[nv * LANES:nv * LANES + nextra]
            ex_ref[...] = 1.0 / (1.0 + jnp.exp(-ex)) if act == "sigmoid" else ex
            continue
        o_ref = next(out_refs)
        width = w_ref.shape[1]
        cw = 256 if width % 256 == 0 else LANES
        for c0 in range(0, width, cw):
            y = _dot(h, w_ref[:, c0:c0 + cw])
            if kind == "norm":
                gsum = _split_dot(y * y, gs_ref[:cw, :cw])
                r = lax.rsqrt(gsum * (1.0 / HEAD_DIM) + NORM_EPS)
                mask = aux_ref[0:1, c0:c0 + cw]
                fac = mask * (r * aux_ref[1:2, c0:c0 + cw]) + (1.0 - mask)
                y = y * fac + aux_ref[2:3, c0:c0 + cw]
            elif kind == "sigmoid":
                y = 1.0 / (1.0 + jnp.exp(-y))
            o_ref[:, c0:c0 + cw] = y.astype(o_ref.dtype)


def _proj(x2d, gain, pieces, tm):
    m, d = x2d.shape
    kinds = tuple(p[0] for p in pieces)
    gs = (np.arange(256)[:, None] // HEAD_DIM == np.arange(256)[None, :] // HEAD_DIM)
    gs = jnp.asarray(gs, BF16)
    in_specs = [pl.BlockSpec((tm, d), lambda i: (i, 0)),
                pl.BlockSpec((1, d), lambda i: (0, 0)),
                pl.BlockSpec((256, 256), lambda i: (0, 0))]
    args = [x2d, gain.reshape(1, d).astype(F32), gs]
    out_specs, out_shapes = [], []
    kb = KB_TILES * TK
    for kind, w, aux, dt in pieces:
        in_specs += [pl.BlockSpec(w.shape, lambda i: (0, 0)), pl.BlockSpec(aux.shape, lambda i: (0, 0))]
        args += [w, aux]
        if isinstance(kind, tuple):
            _, nv, nextra, _ = kind
            out_specs += [pl.BlockSpec((tm // TK, nv * LANES, TK), lambda i: (i, 0, 0)),
                          pl.BlockSpec((nextra, tm), lambda i: (0, i))]
            out_shapes += [jax.ShapeDtypeStruct((m // TK, nv * LANES, TK), dt),
                           jax.ShapeDtypeStruct((nextra, m), F32)]
            continue
        nw = w.shape[1]
        out_specs.append(pl.BlockSpec((tm, nw), lambda i: (i, 0)))
        out_shapes.append(jax.ShapeDtypeStruct((m, nw), dt))
    return pl.pallas_call(
        functools.partial(_proj_kernel, kinds),
        grid=(m // tm,),
        in_specs=in_specs, out_specs=out_specs, out_shape=out_shapes,
        compiler_params=_params(("parallel",)),
        name="proj",
    )(*args)


def _aux(width, mask=None, gain=None, add=None):
    z = jnp.zeros((width,), F32)
    return jnp.stack([z if mask is None else mask, z if gain is None else gain, z if add is None else add])


def _seg(*parts):
    ref = next(p for p in parts if not isinstance(p, int))
    return jnp.concatenate([jnp.zeros(ref.shape[:-1] + (p,), ref.dtype) if isinstance(p, int) else p
                            for p in parts], axis=-1)


def _proj_pieces(w_in, qk):
    w = w_in.astype(BF16)
    hd = HEAD_DIM
    ones, zeros = jnp.ones((hd,), F32), jnp.zeros((hd,), F32)
    cat = jnp.concatenate

    def cols(a, b):
        return w[:, a:b]

    pieces_ab = [
        ("norm", cols(_O_AQ, _O_AK), _aux(512, jnp.ones((512,), F32), jnp.tile(qk[0], A_HEADS) * (SCALE * LOG2E)),
         BF16),
        ("norm", _seg(cols(_O_AK, _O_AV), hd), _aux(LANES, cat([ones, zeros]), cat([qk[1], zeros])), BF16),
        (("trans", 1, 8, "none"), _seg(cols(_O_AV, _O_IQ), hd, cols(_O_IW, _O_BQ), 16 - A_IDX_HEADS).T,
         _aux(LANES), BF16),
        ("plain", cols(_O_IQ, _O_IK), _aux(256), BF16),
        ("plain", _seg(cols(_O_IK, _O_IW), LANES - hd), _aux(LANES), BF16),
        ("norm", cols(_O_BQ, _O_BKV), _aux(512, jnp.ones((512,), F32), jnp.tile(qk[2], B_HEADS) * (SCALE * LOG2E)),
         BF16),
        ("plain", cols(_O_BKV, _O_BKV + 256), _aux(256), F32),
    ]
    o = _O_BKV + 256
    ks0, ks1, vs0, vs1, kw0, kw1, vw0, vw1 = [cols(o + i * hd, o + (i + 1) * hd) for i in range(8)]
    pieces_ab += [
        ("norm", _seg(ks0, hd, ks1, hd, kw0, hd, kw1, hd),
         _aux(512, cat([ones, zeros] * 4), cat([qk[3], zeros] * 4)), BF16),
        (("trans", 4, 32, "sigmoid"),
         _seg(vs0, hd, vs1, hd, vw0, hd, vw1, hd, cols(_O_BG, _O_CQ), 32 - 3 * B_HEADS).T, _aux(LANES), BF16),
    ]
    pieces_c = [
        ("norm", cols(_O_CQ, _O_CK), _aux(768, jnp.ones((768,), F32), jnp.tile(qk[4], C_HEADS) * SCALE), BF16),
        ("norm", cols(_O_CK, _O_CV), _aux(768, jnp.ones((768,), F32), jnp.tile(qk[5], C_HEADS)), BF16),
        ("plain", cols(_O_CV, _O_MIX), _aux(768), BF16),
    ]
    pieces_g = [("sigmoid", cols(_O_MIX, _O_END), _aux(3 * D_MODEL), F32)]
    return pieces_ab, pieces_c, pieces_g


def _flash_init(m_ref, acc_ref):
    m_ref[...] = jnp.full(m_ref.shape, NEG_INIT, F32)
    acc_ref[...] = jnp.zeros(acc_ref.shape, F32)


def _flash_block(q_ref, heads, group_of, kt, v_aug, bias_fn, masks, m_ref, acc_ref):
    heads = list(heads)
    s_all = [_dot_t(q_ref[h], kt[group_of(h)]) for h in heads]
    m_old = [m_ref[h] for h in heads]
    ps, alphas = [], []
    for k, h in enumerate(heads):
        s, mk = s_all[k], masks[group_of(h)]
        sc = [jnp.where(mk[c], s[:, c * TK:(c + 1) * TK] + bias_fn(h, c), -jnp.inf) for c in range(len(mk))]
        m_new = jnp.maximum(m_old[k], jnp.max(functools.reduce(jnp.maximum, sc), axis=-1, keepdims=True))
        ps.append(jnp.concatenate([jnp.exp(x - m_new).astype(BF16) for x in sc], axis=1))
        alphas.append(jnp.exp(m_old[k] - m_new))
        m_ref[h] = m_new
    for k, h in enumerate(heads):
        acc_ref[h] = alphas[k] * acc_ref[h] + _dot(ps[k], v_aug[group_of(h)])


def _flash_block_t(q_ref, slots, group_of, kt, v_aug_t, bias_fn, masks, m_ref, acc_ref):
    slots = list(slots)
    s_all = [_dot_t(kt[group_of(p)], q_ref[p]) for p in slots]
    m_old = [m_ref[p][0:1] for p in slots]
    ps, alphas = [], []
    for k, p in enumerate(slots):
        s, mk = s_all[k], masks[group_of(p)]
        sc = [jnp.where(mk[c], s[c * TK:(c + 1) * TK] + bias_fn(p, c), -jnp.inf) for c in range(len(mk))]
        m_new = jnp.maximum(m_old[k], jnp.max(functools.reduce(jnp.maximum, sc), axis=0, keepdims=True))
        ps.append(jnp.concatenate([jnp.exp2(x - m_new).astype(BF16) for x in sc], axis=0))
        alphas.append(jnp.exp2(m_old[k] - m_new))
        m_ref[p] = jnp.broadcast_to(m_new, m_ref.shape[1:])
    for k, p in enumerate(slots):
        acc_ref[p] = alphas[k] * acc_ref[p] + _dot(v_aug_t[group_of(p)], ps[k])


def _flash_out(acc):
    return acc[:, :HEAD_DIM] / jnp.maximum(acc[:, HEAD_DIM:HEAD_DIM + 1], 1e-30)


def _split_heads(q_ref, qs_ref, nh):
    for h in range(nh):
        qs_ref[h] = q_ref[:, h * HEAD_DIM:(h + 1) * HEAD_DIM]


def _dsa_kernel(topk, nd, pbits, iq_ref, iwt_ref, ik_ref, aq_ref, ak_ref, avt_ref, bias_ref, o_ref,
                keys_ref, half_ref, iqs_ref, qs_ref, m_ref, acc_ref):
    TQ = A_TQ
    i = pl.program_id(1)
    nk = (i + 1) * (TQ // TK)
    nkb = (nk + KB_TILES - 1) // KB_TILES
    kb = KB_TILES * TK
    krow = lax.broadcasted_iota(I32, (TK, TQ), 0)
    qpos = i * TQ + lax.broadcasted_iota(I32, (TK, TQ), 1)
    i16 = jnp.int16
    last = keys_ref.shape[0] * TK - 1

    iwt = iwt_ref[...]
    for h in range(A_IDX_HEADS):
        iqs_ref[h * TQ:(h + 1) * TQ] = iq_ref[:, h * HEAD_DIM:(h + 1) * HEAD_DIM]

    def score_block(jb, c):
        r0 = pl.multiple_of(jb * kb, kb)
        d = _dot_t(ik_ref[pl.ds(r0, kb), 0:HEAD_DIM], iqs_ref[...])
        acc = jnp.zeros((kb, TQ), F32)
        for h in range(A_IDX_HEADS):
            acc = acc + jnp.maximum(d[:, h * TQ:(h + 1) * TQ], 0.0) * iwt[h:h + 1]
        bits = lax.bitcast_convert_type(acc, I32)
        key = jnp.where(bits < 0, bits ^ 0x7FFFFFFF, bits + (last + 1))
        for t in range(KB_TILES):
            j = jb * KB_TILES + t
            kidx = j * TK + krow
            kj = jnp.where(acc[t * TK:(t + 1) * TK] == 0.0, last - kidx, key[t * TK:(t + 1) * TK])
            kj = jnp.where(kidx <= qpos, kj, INT_MIN)
            keys_ref[j] = kj
            half_ref[j] = jnp.right_shift(kj, 16).astype(i16)
        return c

    lax.fori_loop(0, nkb, score_block, 0)

    kf = float(topk)

    nb_max = keys_ref.shape[0] // KB_TILES
    sub = 16

    def search_half(nbits, u0):
        def run(nblk, u_init):
            def step(b, u):
                cand = u | jnp.left_shift(jnp.int32(1), nbits - 1 - b)
                cb = jnp.broadcast_to((cand - 32768).astype(i16), (TK, TQ))
                acc = jnp.zeros((sub, TQ), i16)
                for j in range(nblk * KB_TILES):
                    hit = jnp.where(half_ref[j] >= cb, jnp.ones((), i16), jnp.zeros((), i16))
                    acc = acc + functools.reduce(lambda a, b2: a + b2,
                                                 [hit[r:r + sub] for r in range(0, TK, sub)])
                cnt = jnp.sum(acc.astype(F32), axis=0, keepdims=True)
                return jnp.where(cnt >= kf, cand, u)

            return lax.fori_loop(0, nbits, step, u_init)

        return lax.switch(nkb - 1, [functools.partial(run, n) for n in range(1, nb_max + 1)], u0)

    def for_tiles(fn):
        def body(jb, c):
            for t in range(KB_TILES):
                fn(jb * KB_TILES + t)
            return c
        lax.fori_loop(0, nkb, body, 0)

    zero = jnp.zeros((1, TQ), I32)
    t_hi = search_half(16, zero) - 32768

    def low_tile(j):
        k = keys_ref[j]
        hi = jnp.right_shift(k, 16)
        lo = (k & 0xFFFF) - 32768
        half_ref[j] = jnp.where(hi > t_hi, 32767, jnp.where(hi < t_hi, -32768, lo)).astype(i16)

    for_tiles(low_tile)
    thr = t_hi * 65536 + search_half(16, zero)

    def count_ge(jb, acc):
        for t in range(KB_TILES):
            acc = acc + jnp.where(keys_ref[jb * KB_TILES + t] >= thr, 1.0, 0.0)
        return acc

    n_ge = jnp.sum(lax.fori_loop(0, nkb, count_ge, jnp.zeros((TK, TQ), F32)), axis=0, keepdims=True)

    @pl.when(jnp.max(jnp.where(thr > INT_MIN, n_ge, 0.0)) > kf)
    def _():
        def tie_tile(j):
            k = keys_ref[j]
            rev = last - (j * TK + krow)
            half_ref[j] = jnp.where(k > thr, 32767, jnp.where(k == thr, rev, -32768)).astype(i16)

        for_tiles(tie_tile)
        keep = search_half(pbits, zero + 32768) - 32768

        def demote(j):
            k = keys_ref[j]
            rev = last - (j * TK + krow)
            keys_ref[j] = jnp.where((k == thr) & (rev < keep) & (thr > INT_MIN), k - 1, k)

        for_tiles(demote)

    sel_thr = jnp.maximum(thr, INT_MIN + 1)

    nslot = A_HEADS // A_PAIR
    for h in range(A_HEADS):
        qs_ref[h // A_PAIR, (h % A_PAIR) * TQ:(h % A_PAIR + 1) * TQ] = aq_ref[:, h * HEAD_DIM:(h + 1) * HEAD_DIM]
    _flash_init(m_ref, acc_ref)

    def att_block(jb, c):
        r0 = pl.multiple_of(jb * kb, kb)
        kt = ak_ref[pl.ds(r0, kb), 0:HEAD_DIM]
        j0 = jb * KB_TILES
        masks = []
        for t in range(KB_TILES):
            mk = keys_ref[j0 + t] >= sel_thr
            masks.append(jnp.concatenate([mk] * A_PAIR, axis=1))
        d = [_bias_tile_index(i, j0 + t, TQ, nd) * nslot for t in range(KB_TILES)]
        vat = jnp.concatenate([avt_ref[j0 + t] for t in range(KB_TILES)], axis=1)
        _flash_block_t(qs_ref, range(nslot), lambda p: 0, [kt], [vat], lambda p, t: bias_ref[d[t] + p], [masks],
                       m_ref, acc_ref)
        return c

    lax.fori_loop(0, nkb, att_block, 0)
    outs = []
    for h in range(A_HEADS):
        acc = acc_ref[h // A_PAIR][:, (h % A_PAIR) * TQ:(h % A_PAIR + 1) * TQ]
        outs.append(acc[:HEAD_DIM] / jnp.maximum(acc[HEAD_DIM:HEAD_DIM + 1], 1e-30))
    o_ref[...] = jnp.concatenate(outs, axis=0).T.astype(o_ref.dtype)


def _dsa(iq, iwt, ik, aq, ak, avt, bias, nd, bsz, seq):
    TQ = A_TQ
    nq = seq // TQ
    kb = KB_TILES * TK
    topk = min(A_TOPK_MAX, seq // 4)
    pbits = max(1, (seq - 1).bit_length())
    return pl.pallas_call(
        functools.partial(_dsa_kernel, topk, nd, pbits),
        grid=(bsz, nq),
        in_specs=[
            pl.BlockSpec((TQ, 256), lambda b, i: (b * nq + i, 0)),
            pl.BlockSpec((8, TQ), lambda b, i: (0, b * nq + i)),
            pl.BlockSpec((seq, LANES), lambda b, i: (b, 0)),
            pl.BlockSpec((TQ, 512), lambda b, i: (b * nq + i, 0)),
            pl.BlockSpec((seq, LANES), lambda b, i: (b, 0)),
            pl.BlockSpec((seq // TK, LANES, TK), lambda b, i: (b, 0, 0)),
            pl.BlockSpec(bias.shape, lambda b, i: (0, 0, 0), pipeline_mode=pl.Buffered(1)),
        ],
        out_specs=pl.BlockSpec((TQ, 512), lambda b, i: (b * nq + i, 0)),
        out_shape=jax.ShapeDtypeStruct((bsz * seq, 512), BF16),
        scratch_shapes=[pltpu.VMEM((seq // TK, TK, TQ), I32),
                        pltpu.VMEM((seq // TK, TK, TQ), jnp.int16),
                        pltpu.VMEM((A_IDX_HEADS * TQ, HEAD_DIM), BF16),
                        pltpu.VMEM((A_HEADS // A_PAIR, A_PAIR * TQ, HEAD_DIM), BF16),
                        pltpu.VMEM((A_HEADS // A_PAIR, 8, A_PAIR * TQ), F32),
                        pltpu.VMEM((A_HEADS // A_PAIR, LANES, A_PAIR * TQ), F32)],
        compiler_params=_params(("parallel", "arbitrary")),
        name="dsa",
    )(iq, iwt, ik, aq, ak, avt, bias)


def _nsa_cmp_kernel(x_ref, pos_ref, wlo_ref, whi_ref, gain_ref, k_ref, v_ref):
    x = x_ref[...]
    lo = _dot((x + pos_ref[0:1, :]).astype(BF16), wlo_ref[...])
    hi = _dot((x + pos_ref[1:2, :]).astype(BF16), whi_ref[...])
    nrow = x.shape[0]
    pre = lo + pltpu.roll(hi, nrow - 1, 0)
    ks = []
    for g in range(B_KV_HEADS):
        kg = pre[:, g * HEAD_DIM:(g + 1) * HEAD_DIM]
        ms = jnp.mean(kg * kg, axis=-1, keepdims=True)
        ks.append(kg * lax.rsqrt(ms + NORM_EPS) * gain_ref[...])
    k_ref[...] = jnp.concatenate(ks, axis=1).astype(k_ref.dtype)
    v_ref[...] = pre[:, LANES:2 * LANES].T.astype(v_ref.dtype)


def _nsa_cmp(bcmp, cmp_pos, cmp_w, k_gain, bsz, seq):
    nch = seq // B_CMP_STRIDE
    half = B_CMP_LEN // 2
    width = half * 256
    x = bcmp.reshape(bsz * nch, width)

    def wmat(l0):
        w = jnp.zeros((half, 4, HEAD_DIM, 4, HEAD_DIM), F32)
        for j in range(4):
            w = w.at[:, j, :, j, :].set(cmp_w[j // 2, l0:l0 + half])
        return w.reshape(width, 256).astype(BF16)

    def prow(l0):
        p = jnp.stack([cmp_pos[0, l0:l0 + half], cmp_pos[0, l0:l0 + half],
                       cmp_pos[1, l0:l0 + half], cmp_pos[1, l0:l0 + half]], axis=1)
        return p.reshape(width)

    pos = jnp.stack([prow(0), prow(half)]).astype(F32)
    return pl.pallas_call(
        _nsa_cmp_kernel,
        grid=(bsz,),
        in_specs=[
            pl.BlockSpec((nch, width), lambda b: (b, 0)),
            pl.BlockSpec((2, width), lambda b: (0, 0)),
            pl.BlockSpec((width, 256), lambda b: (0, 0)),
            pl.BlockSpec((width, 256), lambda b: (0, 0)),
            pl.BlockSpec((1, HEAD_DIM), lambda b: (0, 0)),
        ],
        out_specs=[pl.BlockSpec((nch, LANES), lambda b: (b, 0)), pl.BlockSpec((LANES, nch), lambda b: (b, 0))],
        out_shape=[jax.ShapeDtypeStruct((bsz * nch, LANES), BF16), jax.ShapeDtypeStruct((bsz * LANES, nch), BF16)],
        compiler_params=_params(("parallel",)),
        name="nsa_cmp",
    )(x, pos, wmat(0), wmat(half), k_gain.reshape(1, HEAD_DIM).astype(F32))


def _nsa_kernel(seq, nd, q_ref, gt_ref, kc_ref, vct_ref, k_ref, vt_ref, bias_ref, o_ref,
                imp_ref, sel_ref, qs_ref, m_ref, acc_ref):
    TQ = B_TQ
    rq = TQ // TK
    i = pl.program_id(1)
    ncp = seq // B_CMP_STRIDE
    ns = seq // B_SEL_LEN
    n_top = min(B_SEL_TOPK_MAX, ns)
    nslot = B_HEADS // B_PAIR
    spg = B_GROUP // B_PAIR
    krow = lax.broadcasted_iota(I32, (TK, TQ), 0)
    qpos = i * TQ + lax.broadcasted_iota(I32, (TK, TQ), 1)

    def lanes(h):
        return slice((h % B_PAIR) * TQ, (h % B_PAIR + 1) * TQ)

    def dup(x):
        return jnp.concatenate([x] * B_PAIR, axis=1)

    for h in range(B_HEADS):
        qs_ref[h // B_PAIR, lanes(h)] = q_ref[:, h * HEAD_DIM:(h + 1) * HEAD_DIM]

    n_idx = lax.broadcasted_iota(I32, (ncp, TQ), 0)
    t_c = i * TQ + lax.broadcasted_iota(I32, (ncp, TQ), 1)
    cmask = dup(n_idx * B_CMP_STRIDE + (B_CMP_LEN - 1) <= t_c)
    om = lax.broadcasted_iota(I32, (ns, ncp), 0) * B_SEL_LEN
    on = lax.broadcasted_iota(I32, (ns, ncp), 1) * B_CMP_STRIDE
    ovt = jnp.where((on < om + B_SEL_LEN) & (on + B_CMP_LEN > om), 1.0, 0.0).astype(BF16)
    m_idx = lax.broadcasted_iota(I32, (ns, TQ), 0)
    jt = (i * TQ + lax.broadcasted_iota(I32, (ns, TQ), 1)) // B_SEL_LEN
    forced = (m_idx == 0) | (m_idx == jt) | (m_idx == jt - 1)

    kc = [kc_ref[:, g * HEAD_DIM:(g + 1) * HEAD_DIM] for g in range(B_KV_HEADS)]
    vct = [vct_ref[g * HEAD_DIM:(g + 1) * HEAD_DIM, :] for g in range(B_KV_HEADS)]
    st_all = [_dot_t(kc[p // spg], qs_ref[p]) for p in range(nslot)]
    pts = []
    for p in range(nslot):
        st = jnp.where(cmask, st_all[p], -jnp.inf)
        mx = jnp.max(st, axis=0, keepdims=True)
        mx = jnp.where(mx == -jnp.inf, 0.0, mx)
        e = jnp.exp2(st - mx)
        pts.append(e / jnp.maximum(jnp.sum(e, axis=0, keepdims=True), 1e-30))
    oc = [_dot(vct[p // spg], pts[p].astype(BF16)) for p in range(nslot)]

    for g in range(B_KV_HEADS):
        psum = functools.reduce(lambda a, b: a + b, [pts[h // B_PAIR][:, lanes(h)]
                                                     for h in range(g * B_GROUP, (g + 1) * B_GROUP)])
        imp = _split_dot_left(ovt, psum)
        imp = jnp.where(forced, jnp.inf, jnp.where(m_idx <= jt, imp, -jnp.inf))

        sub = 8
        imp_ref[g] = imp
        grp = [imp[r:r + sub] for r in range(0, ns, sub)]
        rank = [jnp.zeros((sub, TQ), F32) for _ in grp]
        rsub = lax.broadcasted_iota(I32, (sub, TQ), 0)
        for mp in range(ns):
            vp = jnp.broadcast_to(imp_ref[g, mp:mp + 1, :], (sub, TQ))
            for r in range(len(grp)):
                if r * sub + sub - 1 < mp:
                    before = vp > grp[r]
                elif r * sub > mp:
                    before = vp >= grp[r]
                else:
                    before = (vp > grp[r]) | ((vp == grp[r]) & (rsub + r * sub > mp))
                rank[r] = rank[r] + jnp.where(before, 1.0, 0.0)
        rank = jnp.concatenate(rank, axis=0)
        sel_ref[g] = jnp.where((rank < float(n_top)) & (m_idx <= jt), 1.0, 0.0)

    def slot_bias(p, j):
        parts = []
        for u in range(B_PAIR):
            for sub in range(rq):
                k = jnp.clip(rq * i + sub - j, 0, nd - 1)
                parts.append(bias_ref[k * B_HEADS + p * B_PAIR + u])
        return jnp.concatenate(parts, axis=1)

    def branch_block(j0, ntiles, koff, vrow, mask_fn):
        rows = pl.ds(pl.multiple_of(j0 * TK, TK), ntiles * TK)
        kt = [k_ref[rows, koff + g * LANES:koff + g * LANES + HEAD_DIM] for g in range(B_KV_HEADS)]
        vt = [jnp.concatenate([vt_ref[j0 + t, vrow + g * LANES:vrow + (g + 1) * LANES, :] for t in range(ntiles)],
                              axis=1) for g in range(B_KV_HEADS)]
        masks = [[dup(mask_fn(g, j0 + t)) for t in range(ntiles)] for g in range(B_KV_HEADS)]
        _flash_block_t(qs_ref, range(nslot), lambda p: p // spg, kt, vt, lambda p, t: slot_bias(p, j0 + t),
                       masks, m_ref, acc_ref)

    bpt = TK // B_SEL_LEN

    def sel_mask(g, j):
        chosen = jnp.concatenate([jnp.broadcast_to(sel_ref[g, pl.ds(j * bpt + b, 1), :], (B_SEL_LEN, TQ))
                                  for b in range(bpt)], axis=0) > 0.5
        return chosen & (j * TK + krow <= qpos)

    def win_mask(g, j):
        dist = qpos - (j * TK + krow)
        return (dist >= 0) & (dist < B_WINDOW)

    def sel_body(jb, c):
        branch_block(jb * KB_TILES, KB_TILES, 0, 0, sel_mask)
        return c

    gt = gt_ref[...]

    def gate(h, br):
        return gt[3 * h + br:3 * h + br + 1]

    def head_out(h):
        a = acc_ref[h // B_PAIR][:, lanes(h)]
        return a[:HEAD_DIM] / jnp.maximum(a[HEAD_DIM:HEAD_DIM + 1], 1e-30)

    _flash_init(m_ref, acc_ref)
    lax.fori_loop(0, (rq * (i + 1) + KB_TILES - 1) // KB_TILES, sel_body, 0)
    part = [gate(h, 0) * oc[h // B_PAIR][:, lanes(h)] + gate(h, 1) * head_out(h) for h in range(B_HEADS)]

    wt = B_WINDOW // TK + rq
    _flash_init(m_ref, acc_ref)
    branch_block(jnp.maximum(rq * (i + 1) - wt, 0), wt, 2 * LANES, 2 * LANES, win_mask)
    outs = [part[h] + gate(h, 2) * head_out(h) for h in range(B_HEADS)]
    o_ref[...] = jnp.concatenate(outs, axis=0).T.astype(o_ref.dtype)


def _nsa(bq, bgt, kcmp, vcmpt, bk, bvt, bias, nd, bsz, seq):
    TQ = B_TQ
    nq = seq // TQ
    ncp = seq // B_CMP_STRIDE
    ns = seq // B_SEL_LEN
    nslot, w = B_HEADS // B_PAIR, B_PAIR * TQ
    return pl.pallas_call(
        functools.partial(_nsa_kernel, seq, nd),
        grid=(bsz, nq),
        in_specs=[
            pl.BlockSpec((TQ, 512), lambda b, i: (b * nq + i, 0)),
            pl.BlockSpec((32, TQ), lambda b, i: (0, b * nq + i)),
            pl.BlockSpec((ncp, LANES), lambda b, i: (b, 0)),
            pl.BlockSpec((LANES, ncp), lambda b, i: (b, 0)),
            pl.BlockSpec((seq, 512), lambda b, i: (b, 0)),
            pl.BlockSpec((seq // TK, 4 * LANES, TK), lambda b, i: (b, 0, 0)),
            pl.BlockSpec(bias.shape, lambda b, i: (0, 0, 0), pipeline_mode=pl.Buffered(1)),
        ],
        out_specs=pl.BlockSpec((TQ, 512), lambda b, i: (b * nq + i, 0)),
        out_shape=jax.ShapeDtypeStruct((bsz * seq, 512), BF16),
        scratch_shapes=[pltpu.VMEM((B_KV_HEADS, ns, TQ), F32), pltpu.VMEM((B_KV_HEADS, ns, TQ), F32),
                        pltpu.VMEM((nslot, w, HEAD_DIM), BF16),
                        pltpu.VMEM((nslot, 8, w), F32),
                        pltpu.VMEM((nslot, LANES, w), F32)],
        compiler_params=_params(("parallel", "arbitrary")),
        name="nsa",
    )(bq, bgt, kcmp, vcmpt, bk, bvt, bias)


def _nsa_kernel_rows(seq, nd, q_ref, g_ref, kc_ref, vc_ref, kv_ref, bias_ref, o_ref,
                     imp_ref, sel_ref, qs_ref, m_ref, acc_ref):
    i = pl.program_id(1)
    ncp = seq // B_CMP_STRIDE
    ns = seq // B_SEL_LEN
    n_top = min(B_SEL_TOPK_MAX, ns)
    hg = B_GROUP
    row = lax.broadcasted_iota(I32, (TQ, TK), 0)
    col = lax.broadcasted_iota(I32, (TQ, TK), 1)
    tpos = i * TQ + row
    _split_heads(q_ref, qs_ref, B_HEADS)

    n_idx = lax.broadcasted_iota(I32, (ncp, TQ), 0)
    t_c = i * TQ + lax.broadcasted_iota(I32, (ncp, TQ), 1)
    cmask = n_idx * B_CMP_STRIDE + (B_CMP_LEN - 1) <= t_c
    om = lax.broadcasted_iota(I32, (ns, ncp), 0) * B_SEL_LEN
    on = lax.broadcasted_iota(I32, (ns, ncp), 1) * B_CMP_STRIDE
    ovt = jnp.where((on < om + B_SEL_LEN) & (on + B_CMP_LEN > om), 1.0, 0.0).astype(BF16)
    m_idx = lax.broadcasted_iota(I32, (ns, TQ), 0)
    jt = (i * TQ + lax.broadcasted_iota(I32, (ns, TQ), 1)) // B_SEL_LEN
    forced = (m_idx == 0) | (m_idx == jt) | (m_idx == jt - 1)

    kc = [kc_ref[:, g * HEAD_DIM:(g + 1) * HEAD_DIM] for g in range(B_KV_HEADS)]
    vc = [vc_ref[:, g * HEAD_DIM:(g + 1) * HEAD_DIM] for g in range(B_KV_HEADS)]
    st_all = [_dot_t(kc[h // hg], qs_ref[h]) for h in range(B_HEADS)]
    pts = []
    for h in range(B_HEADS):
        st = jnp.where(cmask, st_all[h], -jnp.inf)
        mx = jnp.max(st, axis=0, keepdims=True)
        mx = jnp.where(mx == -jnp.inf, 0.0, mx)
        e = jnp.exp(st - mx)
        pts.append(e / jnp.maximum(jnp.sum(e, axis=0, keepdims=True), 1e-30))
    oc = [_dot(pts[h].T.astype(BF16), vc[h // hg]) for h in range(B_HEADS)]

    for g in range(B_KV_HEADS):
        psum = functools.reduce(lambda a, b: a + b, pts[g * hg:(g + 1) * hg])
        hi = psum.astype(BF16)
        lo = (psum - hi.astype(F32)).astype(BF16)
        imp = _dot(ovt, hi) + _dot(ovt, lo)
        imp = jnp.where(forced, jnp.inf, jnp.where(m_idx <= jt, imp, -jnp.inf))
        imp_ref[g] = imp

        def rank_step(mp, rank, g=g, imp=imp):
            vp = jnp.broadcast_to(imp_ref[g, pl.ds(mp, 1), :], (ns, TQ))
            before = (vp > imp) | ((vp == imp) & (mp < m_idx))
            return rank + jnp.where(before, 1.0, 0.0)

        rank = lax.fori_loop(0, ns, rank_step, jnp.zeros((ns, TQ), F32), unroll=8)
        selt = jnp.where((rank < float(n_top)) & (m_idx <= jt), 1.0, 0.0)
        selt = jnp.concatenate([selt, jnp.zeros((LANES - ns, TQ), F32)], axis=0)
        sel_ref[g] = selt.T.astype(BF16)

    def branch_block(j0, ntiles, koff, voff, mask_fn):
        r0 = pl.multiple_of(j0 * TK, TK)
        d = [_bias_tile_index(i, j0 + t, TQ, nd) * B_HEADS for t in range(ntiles)]
        rows = pl.ds(r0, ntiles * TK)
        kt = [kv_ref[rows, koff + g * LANES:koff + g * LANES + HEAD_DIM] for g in range(B_KV_HEADS)]
        va = [kv_ref[rows, voff + g * LANES:voff + (g + 1) * LANES] for g in range(B_KV_HEADS)]
        masks = [mask_fn(g, j0) for g in range(B_KV_HEADS)]
        _flash_block(qs_ref, range(B_HEADS), lambda h: h // hg, kt, va, lambda h, t: bias_ref[d[t] + h],
                     masks, m_ref, acc_ref)

    kb = KB_TILES * TK
    e_m = lax.broadcasted_iota(I32, (LANES, kb), 0)
    e_c = lax.broadcasted_iota(I32, (LANES, kb), 1) // B_SEL_LEN

    def sel_masks(g, j0):
        expand = jnp.where(e_m == j0 * (TK // B_SEL_LEN) + e_c, 1.0, 0.0).astype(BF16)
        chosen = _dot(sel_ref[g], expand) > 0.5
        return [chosen[:, t * TK:(t + 1) * TK] & ((j0 + t) * TK + col <= tpos) for t in range(KB_TILES)]

    def sel_body(jb, c):
        branch_block(jb * KB_TILES, KB_TILES, 0, 4 * LANES, sel_masks)
        return c

    gates = g_ref[...]

    def gate(h, br):
        return gates[:, 3 * h + br:3 * h + br + 1]

    _flash_init(m_ref, acc_ref)
    lax.fori_loop(0, (i + KB_TILES) // KB_TILES, sel_body, 0)
    part = [gate(h, 0) * oc[h] + gate(h, 1) * _flash_out(acc_ref[h]) for h in range(B_HEADS)]

    wt = B_WINDOW // TK + 1
    w0 = jnp.maximum(i + 1 - wt, 0)

    def win_masks(g, j0):
        out = []
        for t in range(wt):
            dist = tpos - ((j0 + t) * TK + col)
            out.append((dist >= 0) & (dist < B_WINDOW))
        return out

    _flash_init(m_ref, acc_ref)
    branch_block(w0, wt, 2 * LANES, 6 * LANES, win_masks)
    outs = [part[h] + gate(h, 2) * _flash_out(acc_ref[h]) for h in range(B_HEADS)]
    o_ref[...] = jnp.concatenate(outs, axis=1).astype(o_ref.dtype)


def _nsa_rows(bq, bg, kcmp, vcmp, bsw, bias, nd, bsz, seq):
    nq = seq // TQ
    ncp = seq // B_CMP_STRIDE
    ns = seq // B_SEL_LEN
    return pl.pallas_call(
        functools.partial(_nsa_kernel_rows, seq, nd),
        grid=(bsz, nq),
        in_specs=[
            pl.BlockSpec((TQ, 512), lambda b, i: (b * nq + i, 0)),
            pl.BlockSpec((TQ, LANES), lambda b, i: (b * nq + i, 0)),
            pl.BlockSpec((ncp, LANES), lambda b, i: (b, 0)),
            pl.BlockSpec((ncp, LANES), lambda b, i: (b, 0)),
            pl.BlockSpec((seq, 1024), lambda b, i: (b, 0)),
            pl.BlockSpec(bias.shape, lambda b, i: (0, 0, 0)),
        ],
        out_specs=pl.BlockSpec((TQ, 512), lambda b, i: (b * nq + i, 0)),
        out_shape=jax.ShapeDtypeStruct((bsz * seq, 512), BF16),
        scratch_shapes=[pltpu.VMEM((B_KV_HEADS, ns, TQ), F32), pltpu.VMEM((B_KV_HEADS, TQ, LANES), BF16),
                        pltpu.VMEM((B_HEADS, TQ, HEAD_DIM), BF16),
                        pltpu.VMEM((B_HEADS, TQ, LANES), F32),
                        pltpu.VMEM((B_HEADS, TQ, LANES), F32)],
        compiler_params=_params(("parallel", "arbitrary")),
        name="nsa",
    )(bq, bg, kcmp, vcmp, bsw, bias)


def _dil_kernel(q_ref, kp_ref, kc_ref, vp_ref, vc_ref, bias_ref, o_ref, lse_ref):
    i = pl.program_id(2)
    q = q_ref[...]
    k2 = jnp.concatenate([kp_ref[...], kc_ref[...]], axis=0)
    v2 = jnp.concatenate([vp_ref[...], vc_ref[...]], axis=0)
    row = lax.broadcasted_iota(I32, (TQ, 2 * TK), 0)
    col = lax.broadcasted_iota(I32, (TQ, 2 * TK), 1)
    du = row + TK - col
    valid = (du >= 0) & (du <= TK) & ((col >= TK) | (i > 0))
    for hh in range(C_HEADS_PER_GROUP):
        hs = slice(hh * HEAD_DIM, (hh + 1) * HEAD_DIM)
        s = jnp.where(valid, _dot_t(q[:, hs], k2[:, hs]) + bias_ref[hh], -jnp.inf)
        m = jnp.max(s, axis=-1, keepdims=True)
        e = jnp.exp(s - m)
        den = jnp.sum(e, axis=-1, keepdims=True)
        o_ref[:, hs] = _dot(e.astype(BF16), v2[:, hs]) / den
        lse_ref[:, hs] = jnp.broadcast_to(m + jnp.log(den), (TQ, HEAD_DIM))


def _dilated_group(cq, ck, cv, bias, g, dil, bsz, seq):
    ln = seq // dil
    nq = ln // TQ
    width = C_HEADS * HEAD_DIM
    gw = C_HEADS_PER_GROUP * HEAD_DIM
    ncb = width // gw
    views = [a.reshape(bsz * ln, dil * width) for a in (cq, ck, cv)]

    def cur(b, r, i):
        return (b * nq + i, r * ncb + g)

    def prev(b, r, i):
        return (b * nq + jnp.maximum(i - 1, 0), r * ncb + g)

    blk = (TQ, gw)
    o, lse = pl.pallas_call(
        _dil_kernel,
        grid=(bsz, dil, nq),
        in_specs=[pl.BlockSpec(blk, cur), pl.BlockSpec(blk, prev), pl.BlockSpec(blk, cur),
                  pl.BlockSpec(blk, prev), pl.BlockSpec(blk, cur),
                  pl.BlockSpec(bias.shape, lambda b, r, i: (0, 0, 0))],
        out_specs=[pl.BlockSpec(blk, lambda b, r, i: (b * nq + i, r))] * 2,
        out_shape=[jax.ShapeDtypeStruct((bsz * ln, dil * gw), F32)] * 2,
        compiler_params=_params(("parallel", "parallel", "arbitrary")),
        name=f"dilated_d{dil}",
    )(views[0], views[1], views[1], views[2], views[2], bias)
    return o.reshape(bsz * seq, gw), lse.reshape(bsz * seq, gw)


def _merge_kernel(x_ref, ya_ref, yb_ref, o0_ref, l0_ref, o1_ref, l1_ref, o2_ref, l2_ref, g_ref,
                  wa_ref, wb_ref, wc_ref, wo_ref, out_ref):
    l0, l1, l2 = l0_ref[...], l1_ref[...], l2_ref[...]
    mx = jnp.maximum(jnp.maximum(l0, l1), l2)
    e0, e1, e2 = jnp.exp(l0 - mx), jnp.exp(l1 - mx), jnp.exp(l2 - mx)
    yc = (e0 * o0_ref[...] + e1 * o1_ref[...] + e2 * o2_ref[...]) / (e0 + e1 + e2)
    ya = _dot(ya_ref[...], wa_ref[...])
    yb = _dot(yb_ref[...], wb_ref[...])
    yc = _dot(yc.astype(BF16), wc_ref[...])
    d = D_MODEL
    z = g_ref[:, 0:d] * ya + g_ref[:, d:2 * d] * yb + g_ref[:, 2 * d:3 * d] * yc
    out_ref[...] = x_ref[...] + _dot(z.astype(BF16), wo_ref[...])


def _merge(x2d, ya, yb, c_outs, mixg, wa, wb, wc, wo, tm):
    m = x2d.shape[0]

    def rows(w):
        return pl.BlockSpec((tm, w), lambda i: (i, 0))

    def full(a):
        return pl.BlockSpec(a.shape, lambda i: (0, 0))

    c_flat = [a for pair in c_outs for a in pair]
    return pl.pallas_call(
        _merge_kernel,
        grid=(m // tm,),
        in_specs=[rows(D_MODEL), rows(512), rows(512)] + [rows(256)] * 6 + [rows(3 * D_MODEL)]
                 + [full(wa), full(wb), full(wc), full(wo)],
        out_specs=rows(D_MODEL),
        out_shape=jax.ShapeDtypeStruct((m, D_MODEL), F32),
        compiler_params=_params(("parallel",)),
        name="merge",
    )(x2d, ya, yb, *c_flat, mixg, wa, wb, wc, wo)


def _ffn_kernel(x_ref, g_ref, wg_ref, wu_ref, wd_ref, out_ref):
    x = x_ref[...]
    ms = jnp.mean(x * x, axis=-1, keepdims=True)
    h = (x * lax.rsqrt(ms + NORM_EPS) * g_ref[...]).astype(BF16)
    gate = _dot(h, wg_ref[...])
    up = _dot(h, wu_ref[...])
    act = gate / (1.0 + jnp.exp(-gate)) * up
    out_ref[...] = x + _dot(act.astype(BF16), wd_ref[...])


def _ffn(x2d, gain, w_in, w_out, tm):
    m = x2d.shape[0]
    wg = w_in[:, :D_FF].astype(BF16)
    wu = w_in[:, D_FF:].astype(BF16)
    wd = w_out.astype(BF16)

    def full(a):
        return pl.BlockSpec(a.shape, lambda i: (0, 0))

    return pl.pallas_call(
        _ffn_kernel,
        grid=(m // tm,),
        in_specs=[pl.BlockSpec((tm, D_MODEL), lambda i: (i, 0)), pl.BlockSpec((1, D_MODEL), lambda i: (0, 0)),
                  full(wg), full(wu), full(wd)],
        out_specs=pl.BlockSpec((tm, D_MODEL), lambda i: (i, 0)),
        out_shape=jax.ShapeDtypeStruct((m, D_MODEL), F32),
        compiler_params=_params(("parallel",)),
        name="ffn",
    )(x2d, gain.reshape(1, D_MODEL).astype(F32), wg, wu, wd)


def _layer(x2d, bsz, seq, norm1_g, norm2_g, w_in, qk, cmp_pos, cmp_w, w_a, w_b, w_c, w_out, w_ffn_in, w_ffn_out,
           bias_a, bias_b, bias_c):
    pieces_ab, pieces_c, pieces_g = _proj_pieces(w_in, qk)
    aq, ak, avt, iwt, iq, ik, bq, bcmp, bk, bvt, bgt = _proj(x2d, norm1_g, pieces_ab, 512)
    cq, ck, cv = _proj(x2d, norm1_g, pieces_c, 512)
    (mixg,) = _proj(x2d, norm1_g, pieces_g, 512)

    ya = _dsa(iq, iwt, ik, aq, ak, avt, *bias_a, bsz, seq)
    kcmp, vcmpt = _nsa_cmp(bcmp, cmp_pos, cmp_w, qk[3], bsz, seq)
    yb = _nsa(bq, bgt, kcmp, vcmpt, bk, bvt, *bias_b, bsz, seq)
    c_outs = [_dilated_group(cq, ck, cv, bias_c[g], g, dil, bsz, seq) for g, (_, dil) in enumerate(C_GROUPS)]

    x1 = _merge(x2d, ya, yb, c_outs, mixg, w_a.astype(BF16), w_b.astype(BF16), w_c.astype(BF16),
                w_out.astype(BF16), 256)
    return _ffn(x1, norm2_g, w_ffn_in, w_ffn_out, 256)


def kernel(x, norm1_g, norm2_g, w_in, qk_norm_g, nsa_cmp_pos, nsa_cmp_w, w_branch_a, w_branch_b, w_branch_c, w_out, w_ffn_in, w_ffn_out, rel_bias):
    bsz, seq, d = x.shape
    assert d == D_MODEL and seq % (TQ * max(dil for _, dil in C_GROUPS)) == 0 and seq % A_TQ == 0
    assert seq % (KB_TILES * TK) == 0 and seq % B_TQ == 0 and seq >= B_WINDOW + B_TQ
    for win, dil in C_GROUPS:
        assert win == TK * dil
    bias_a = _toeplitz_bias(rel_bias[:, :A_HEADS], seq, A_TQ, keys_on_rows=True, group=A_PAIR, scale=LOG2E)
    bias_b = _toeplitz_bias(rel_bias[:, A_HEADS:A_HEADS + B_HEADS], seq, TK, keys_on_rows=True, scale=LOG2E)
    rel_c = rel_bias[:, A_HEADS + B_HEADS:]
    bias_c = [_dilated_bias(rel_c[:, g * C_HEADS_PER_GROUP:(g + 1) * C_HEADS_PER_GROUP], dil)
              for g, (_, dil) in enumerate(C_GROUPS)]
    x2d = x.reshape(bsz * seq, d)
    for layer in range(norm1_g.shape[0]):
        x2d = _layer(x2d, bsz, seq, norm1_g[layer], norm2_g[layer], w_in[layer], qk_norm_g[layer],
                     nsa_cmp_pos[layer], nsa_cmp_w[layer], w_branch_a[layer], w_branch_b[layer],
                     w_branch_c[layer], w_out[layer], w_ffn_in[layer], w_ffn_out[layer],
                     bias_a, bias_b, bias_c)
    return x2d.reshape(bsz, seq, d)
```

```python
import functools
import math

import numpy as np
import jax
import jax.numpy as jnp
from jax import lax
from jax.experimental import pallas as pl
from jax.experimental.pallas import tpu as pltpu

F32 = jnp.float32
BF16 = jnp.bfloat16
I32 = jnp.int32

D_MODEL = 1024
HEAD_DIM = 64
NORM_EPS = 1e-6
REL_BUCKETS = 32
REL_MAX_DIST = 2048

A_HEADS = 8
A_IDX_HEADS = 4
A_TOPK_MAX = 256
B_HEADS = 8
B_KV_HEADS = 2
B_GROUP = B_HEADS // B_KV_HEADS
B_CMP_LEN = 32
B_CMP_STRIDE = 16
B_SEL_LEN = 64
B_SEL_TOPK_MAX = 16
B_WINDOW = 512
C_GROUPS = ((128, 1), (512, 4), (2048, 16))
C_HEADS_PER_GROUP = 4
C_HEADS = C_HEADS_PER_GROUP * len(C_GROUPS)
D_FF = ((8 * D_MODEL + 3 * 256 - 1) // (3 * 256)) * 256

_O_AQ = 0
_O_AK = _O_AQ + A_HEADS * HEAD_DIM
_O_AV = _O_AK + HEAD_DIM
_O_IQ = _O_AV + HEAD_DIM
_O_IK = _O_IQ + A_IDX_HEADS * HEAD_DIM
_O_IW = _O_IK + HEAD_DIM
_O_BQ = _O_IW + A_IDX_HEADS
_O_BKV = _O_BQ + B_HEADS * HEAD_DIM
_O_BG = _O_BKV + 6 * B_KV_HEADS * HEAD_DIM
_O_CQ = _O_BG + 3 * B_HEADS
_O_CK = _O_CQ + C_HEADS * HEAD_DIM
_O_CV = _O_CK + C_HEADS * HEAD_DIM
_O_MIX = _O_CV + C_HEADS * HEAD_DIM
_O_END = _O_MIX + 3 * D_MODEL

TQ = 128
TK = 128
KB_TILES = 4
A_TQ = 256
A_PAIR = 2
CLASS_TILE = 512
C_TILES_PER_STEP = 4
B_TQ = 256
B_PAIR = 2
LOG2E = 1.4426950408889634
LANES = 128
VMEM_LIMIT = 56 * 1024 * 1024
INT_MIN = -2 ** 31
NEG_INIT = -1e30
SCALE = HEAD_DIM ** -0.5


def _params(sem):
    return pltpu.CompilerParams(dimension_semantics=sem, vmem_limit_bytes=VMEM_LIMIT)


def _dot_t(a, b):
    return lax.dot_general(a, b, (((1,), (1,)), ((), ())), preferred_element_type=F32)


def _dot(a, b):
    return jnp.dot(a, b, preferred_element_type=F32)


def _split_dot_left(a_bf16, b):
    hi = b.astype(BF16)
    lo = (b - hi.astype(F32)).astype(BF16)
    return _dot(a_bf16, hi) + _dot(a_bf16, lo)


def _split_dot(a, b_bf16):
    hi = a.astype(BF16)
    lo = (a - hi.astype(F32)).astype(BF16)
    return _dot(hi, b_bf16) + _dot(lo, b_bf16)


def _bucket_table(n_max):
    n = np.arange(n_max, dtype=np.int64)
    exact = REL_BUCKETS // 2
    nf = np.maximum(n, 1).astype(np.float32)
    large = exact + (np.log(nf / np.float32(exact)) / np.float32(math.log(REL_MAX_DIST / exact))
                     * np.float32(REL_BUCKETS - exact)).astype(np.int32)
    return np.where(n < exact, n, np.minimum(large, REL_BUCKETS - 1)).astype(np.int32)


def _num_bias_tiles(seq, tq):
    bucket = _bucket_table(seq + tq)
    first_sat = int(np.min(np.nonzero(bucket == REL_BUCKETS - 1)[0]))
    assert np.all(bucket[first_sat:] == REL_BUCKETS - 1)
    nd = -(-(first_sat + TK - 1) // TK) + tq // TK
    return min(nd, seq // TK)


def _bias_tile_index(i, j, tq, nd):
    return jnp.clip((tq // TK) * (i + 1) - 1 - j, 0, nd - 1)


def _bias_kernel(nh, group, scale, idx_ref, rel_ref, o_ref):
    idx = idx_ref[0]
    c = idx.shape[1]
    acc = [jnp.zeros(idx.shape, F32) for _ in range(nh)]
    for b in range(REL_BUCKETS):
        hit = idx == b
        for h in range(nh):
            acc[h] = jnp.where(hit, rel_ref[b, h] * scale, acc[h])
    for h in range(nh):
        o_ref[h // group, :, (h % group) * c:(h % group + 1) * c] = acc[h]


def _bias_tiles(rel_cols, idx, group=1, scale=1.0):
    n, r, c = idx.shape
    nh = rel_cols.shape[1]
    return pl.pallas_call(
        functools.partial(_bias_kernel, nh, group, scale),
        grid=(n,),
        in_specs=[pl.BlockSpec((1, r, c), lambda k: (k, 0, 0)),
                  pl.BlockSpec(memory_space=pltpu.SMEM)],
        out_specs=pl.BlockSpec((nh // group, r, group * c), lambda k: (k, 0, 0)),
        out_shape=jax.ShapeDtypeStruct((n * nh // group, r, group * c), F32),
        compiler_params=_params(("parallel",)),
        name="bias_tiles",
    )(jnp.asarray(idx, I32), rel_cols.astype(F32))


def _toeplitz_bias(rel_cols, seq, tq=TQ, keys_on_rows=False, group=1, scale=1.0):
    nd = _num_bias_tiles(seq, tq)
    bucket = _bucket_table(seq + tq)
    d = ((np.arange(nd)[:, None, None] - (tq // TK - 1)) * TK
         + np.arange(tq)[None, :, None] - np.arange(TK)[None, None, :])
    idx = bucket[np.clip(d, 0, None)]
    return _bias_tiles(rel_cols, idx.transpose(0, 2, 1) if keys_on_rows else idx, group, scale), nd


def _dilated_bias(rel_cols, dil):
    bucket = _bucket_table(2 * TK * dil + 1)
    du = np.arange(TQ)[:, None] + TK - np.arange(2 * TK)[None, :]
    return _bias_tiles(rel_cols, bucket[np.clip(du, 0, None) * dil][None])


def _proj_kernel(kinds, *refs):
    n = len(kinds)
    x_ref, g_ref, gs_ref = refs[0], refs[1], refs[2]
    w_refs = refs[3:3 + 2 * n:2]
    aux_refs = refs[4:4 + 2 * n:2]
    n_out = sum(2 if isinstance(k, tuple) and k[0] == "trans" else 1 for k in kinds)
    out_refs = iter(refs[3 + 2 * n:3 + 2 * n + n_out])
    scratch_refs = refs[3 + 2 * n + n_out:]
    x = x_ref[...]
    ms = jnp.mean(x * x, axis=-1, keepdims=True)
    h = (x * lax.rsqrt(ms + NORM_EPS) * g_ref[...]).astype(BF16)
    for kind, w_ref, aux_ref in zip(kinds, w_refs, aux_refs):
        if isinstance(kind, tuple) and kind[0] == "trans":
            _, nv, nextra, act = kind
            vt_ref, ex_ref = next(out_refs), next(out_refs)
            yt = _dot_t(w_ref[...], h)
            rows = lax.broadcasted_iota(I32, (nv * LANES, yt.shape[1]), 0)
            vt = (yt[0:nv * LANES] + jnp.where(rows % LANES >= HEAD_DIM, 1.0, 0.0)).astype(vt_ref.dtype)
            for c in range(vt_ref.shape[0]):
                vt_ref[c] = vt[:, c * TK:(c + 1) * TK]
            ex = yt[nv * LANES:nv * LANES + nextra]
            ex_ref[...] = 1.0 / (1.0 + jnp.exp(-ex)) if act == "sigmoid" else ex
            continue
        o_ref = next(out_refs)
        classes = kind[1] if isinstance(kind, tuple) else 0
        width = w_ref.shape[1]
        cw = 256 if width % 256 == 0 else LANES
        for c0 in range(0, width, cw):
            y = _dot(h, w_ref[:, c0:c0 + cw])
            if kind == "norm" or classes:
                gsum = _split_dot(y * y, gs_ref[:cw, :cw])
                r = lax.rsqrt(gsum * (1.0 / HEAD_DIM) + NORM_EPS)
                mask = aux_ref[0:1, c0:c0 + cw]
                fac = mask * (r * aux_ref[1:2, c0:c0 + cw]) + (1.0 - mask)
                y = y * fac + aux_ref[2:3, c0:c0 + cw]
            elif kind == "sigmoid":
                y = 1.0 / (1.0 + jnp.exp(-y))
            if classes:
                for c in range(0, cw, LANES):
                    scratch_refs[0][(c0 + c) // LANES] = y[:, c:c + LANES]
            else:
                o_ref[:, c0:c0 + cw] = y.astype(o_ref.dtype)
        for r in range(classes):
            rows = pl.ds(r, x.shape[0] // classes, stride=classes)
            o_ref[r] = jnp.concatenate([scratch_refs[0][c, rows, :] for c in range(width // LANES)],
                                       axis=1).astype(o_ref.dtype)


def _proj(x2d, gain, pieces, tm):
    m, d = x2d.shape
    kinds = tuple(p[0] for p in pieces)
    gs = (np.arange(256)[:, None] // HEAD_DIM == np.arange(256)[None, :] // HEAD_DIM)
    gs = jnp.asarray(gs, BF16)
    in_specs = [pl.BlockSpec((tm, d), lambda i: (i, 0)),
                pl.BlockSpec((1, d), lambda i: (0, 0)),
                pl.BlockSpec((256, 256), lambda i: (0, 0))]
    args = [x2d, gain.reshape(1, d).astype(F32), gs]
    out_specs, out_shapes, scratch = [], [], []
    for kind, w, aux, dt in pieces:
        in_specs += [pl.BlockSpec(w.shape, lambda i: (0, 0)), pl.BlockSpec(aux.shape, lambda i: (0, 0))]
        args += [w, aux]
        if isinstance(kind, tuple) and kind[0] == "trans":
            _, nv, nextra, _ = kind
            out_specs += [pl.BlockSpec((tm // TK, nv * LANES, TK), lambda i: (i, 0, 0)),
                          pl.BlockSpec((nextra, tm), lambda i: (0, i))]
            out_shapes += [jax.ShapeDtypeStruct((m // TK, nv * LANES, TK), dt),
                           jax.ShapeDtypeStruct((nextra, m), F32)]
            continue
        nw = w.shape[1]
        if isinstance(kind, tuple) and kind[0] == "classes":
            dil = kind[1]
            assert tm == CLASS_TILE
            out_specs.append(pl.BlockSpec((None, dil, tm // dil, nw), lambda i: (i, 0, 0, 0)))
            out_shapes.append(jax.ShapeDtypeStruct((m // tm, dil, tm // dil, nw), dt))
            scratch = [pltpu.VMEM((nw // LANES, tm, LANES), F32)]
            continue
        out_specs.append(pl.BlockSpec((tm, nw), lambda i: (i, 0)))
        out_shapes.append(jax.ShapeDtypeStruct((m, nw), dt))
    return pl.pallas_call(
        functools.partial(_proj_kernel, kinds),
        grid=(m // tm,),
        in_specs=in_specs, out_specs=out_specs, out_shape=out_shapes, scratch_shapes=scratch,
        compiler_params=_params(("parallel",)),
        name="proj",
    )(*args)


def _aux(width, mask=None, gain=None, add=None):
    z = jnp.zeros((width,), F32)
    return jnp.stack([z if mask is None else mask, z if gain is None else gain, z if add is None else add])


def _seg(*parts):
    ref = next(p for p in parts if not isinstance(p, int))
    return jnp.concatenate([jnp.zeros(ref.shape[:-1] + (p,), ref.dtype) if isinstance(p, int) else p
                            for p in parts], axis=-1)


def _proj_pieces(w_in, qk):
    w = w_in.astype(BF16)
    hd = HEAD_DIM
    ones, zeros = jnp.ones((hd,), F32), jnp.zeros((hd,), F32)
    cat = jnp.concatenate

    def cols(a, b):
        return w[:, a:b]

    pieces_ab = [
        ("norm", cols(_O_AQ, _O_AK), _aux(512, jnp.ones((512,), F32), jnp.tile(qk[0], A_HEADS) * (SCALE * LOG2E)),
         BF16),
        ("norm", _seg(cols(_O_AK, _O_AV), hd), _aux(LANES, cat([ones, zeros]), cat([qk[1], zeros])), BF16),
        (("trans", 1, 8, "none"), _seg(cols(_O_AV, _O_IQ), hd, cols(_O_IW, _O_BQ), 16 - A_IDX_HEADS).T,
         _aux(LANES), BF16),
        ("plain", cols(_O_IQ, _O_IK), _aux(256), BF16),
        ("plain", _seg(cols(_O_IK, _O_IW), LANES - hd), _aux(LANES), BF16),
        ("norm", cols(_O_BQ, _O_BKV), _aux(512, jnp.ones((512,), F32), jnp.tile(qk[2], B_HEADS) * (SCALE * LOG2E)),
         BF16),
        ("plain", cols(_O_BKV, _O_BKV + 256), _aux(256), F32),
    ]
    o = _O_BKV + 256
    ks0, ks1, vs0, vs1, kw0, kw1, vw0, vw1 = [cols(o + i * hd, o + (i + 1) * hd) for i in range(8)]
    pieces_ab += [
        ("norm", _seg(ks0, hd, ks1, hd, kw0, hd, kw1, hd),
         _aux(512, cat([ones, zeros] * 4), cat([qk[3], zeros] * 4)), BF16),
        (("trans", 4, 32, "sigmoid"),
         _seg(vs0, hd, vs1, hd, vw0, hd, vw1, hd, cols(_O_BG, _O_CQ), 32 - 3 * B_HEADS).T, _aux(LANES), BF16),
    ]
    gw = C_HEADS_PER_GROUP * hd
    pieces_c = []
    for g, (_, dil) in enumerate(C_GROUPS):
        wg = cat([cols(o0 + g * gw, o0 + (g + 1) * gw) for o0 in (_O_CQ, _O_CK, _O_CV)], axis=1)
        mask = cat([jnp.ones((2 * gw,), F32), jnp.zeros((gw,), F32)])
        gain = cat([jnp.tile(qk[4], C_HEADS_PER_GROUP) * SCALE, jnp.tile(qk[5], C_HEADS_PER_GROUP),
                    jnp.zeros((gw,), F32)])
        pieces_c.append((("classes", dil), wg, _aux(3 * gw, mask, gain), BF16))
    pieces_g = [("sigmoid", cols(_O_MIX, _O_END), _aux(3 * D_MODEL), F32)]
    return pieces_ab, pieces_c, pieces_g


def _flash_init(m_ref, acc_ref):
    m_ref[...] = jnp.full(m_ref.shape, NEG_INIT, F32)
    acc_ref[...] = jnp.zeros(acc_ref.shape, F32)


def _flash_block(q_ref, heads, group_of, kt, v_aug, bias_fn, masks, m_ref, acc_ref):
    heads = list(heads)
    s_all = [_dot_t(q_ref[h], kt[group_of(h)]) for h in heads]
    m_old = [m_ref[h] for h in heads]
    ps, alphas = [], []
    for k, h in enumerate(heads):
        s, mk = s_all[k], masks[group_of(h)]
        sc = [jnp.where(mk[c], s[:, c * TK:(c + 1) * TK] + bias_fn(h, c), -jnp.inf) for c in range(len(mk))]
        m_new = jnp.maximum(m_old[k], jnp.max(functools.reduce(jnp.maximum, sc), axis=-1, keepdims=True))
        ps.append(jnp.concatenate([jnp.exp(x - m_new).astype(BF16) for x in sc], axis=1))
        alphas.append(jnp.exp(m_old[k] - m_new))
        m_ref[h] = m_new
    for k, h in enumerate(heads):
        acc_ref[h] = alphas[k] * acc_ref[h] + _dot(ps[k], v_aug[group_of(h)])


def _flash_block_t(q_ref, slots, group_of, kt, v_aug_t, bias_fn, masks, m_ref, acc_ref):
    slots = list(slots)
    s_all = [_dot_t(kt[group_of(p)], q_ref[p]) for p in slots]
    m_old = [m_ref[p][0:1] for p in slots]
    ps, alphas = [], []
    for k, p in enumerate(slots):
        s, mk = s_all[k], masks[group_of(p)]
        sc = [jnp.where(mk[c], s[c * TK:(c + 1) * TK] + bias_fn(p, c), -jnp.inf) for c in range(len(mk))]
        m_new = jnp.maximum(m_old[k], jnp.max(functools.reduce(jnp.maximum, sc), axis=0, keepdims=True))
        ps.append(jnp.concatenate([jnp.exp2(x - m_new).astype(BF16) for x in sc], axis=0))
        alphas.append(jnp.exp2(m_old[k] - m_new))
        m_ref[p] = jnp.broadcast_to(m_new, m_ref.shape[1:])
    for k, p in enumerate(slots):
        acc_ref[p] = alphas[k] * acc_ref[p] + _dot(v_aug_t[group_of(p)], ps[k])


def _flash_out(acc):
    return acc[:, :HEAD_DIM] / jnp.maximum(acc[:, HEAD_DIM:HEAD_DIM + 1], 1e-30)


def _split_heads(q_ref, qs_ref, nh):
    for h in range(nh):
        qs_ref[h] = q_ref[:, h * HEAD_DIM:(h + 1) * HEAD_DIM]


def _dsa_kernel(topk, nd, pbits, iq_ref, iwt_ref, ik_ref, aq_ref, ak_ref, avt_ref, bias_ref, o_ref,
                keys_ref, half_ref, iqs_ref, qs_ref, m_ref, acc_ref):
    TQ = A_TQ
    i = pl.program_id(1)
    nk = (i + 1) * (TQ // TK)
    nkb = (nk + KB_TILES - 1) // KB_TILES
    kb = KB_TILES * TK
    krow = lax.broadcasted_iota(I32, (TK, TQ), 0)
    qpos = i * TQ + lax.broadcasted_iota(I32, (TK, TQ), 1)
    i16 = jnp.int16
    last = keys_ref.shape[0] * TK - 1

    iwt = iwt_ref[...]
    for h in range(A_IDX_HEADS):
        iqs_ref[h * TQ:(h + 1) * TQ] = iq_ref[:, h * HEAD_DIM:(h + 1) * HEAD_DIM]

    def score_block(jb, c):
        r0 = pl.multiple_of(jb * kb, kb)
        d = _dot_t(ik_ref[pl.ds(r0, kb), 0:HEAD_DIM], iqs_ref[...])
        acc = jnp.zeros((kb, TQ), F32)
        for h in range(A_IDX_HEADS):
            acc = acc + jnp.maximum(d[:, h * TQ:(h + 1) * TQ], 0.0) * iwt[h:h + 1]
        bits = lax.bitcast_convert_type(acc, I32)
        key = jnp.where(bits < 0, bits ^ 0x7FFFFFFF, bits + (last + 1))
        for t in range(KB_TILES):
            j = jb * KB_TILES + t
            kidx = j * TK + krow
            kj = jnp.where(acc[t * TK:(t + 1) * TK] == 0.0, last - kidx, key[t * TK:(t + 1) * TK])
            kj = jnp.where(kidx <= qpos, kj, INT_MIN)
            keys_ref[j] = kj
            half_ref[j] = jnp.right_shift(kj, 16).astype(i16)
        return c

    lax.fori_loop(0, nkb, score_block, 0)

    kf = float(topk)

    nb_max = keys_ref.shape[0] // KB_TILES
    sub = 16

    def search_half(nbits, u0):
        def run(nblk, u_init):
            def step(b, u):
                cand = u | jnp.left_shift(jnp.int32(1), nbits - 1 - b)
                cb = jnp.broadcast_to((cand - 32768).astype(i16), (TK, TQ))
                acc = jnp.zeros((sub, TQ), i16)
                for j in range(nblk * KB_TILES):
                    hit = jnp.where(half_ref[j] >= cb, jnp.ones((), i16), jnp.zeros((), i16))
                    acc = acc + functools.reduce(lambda a, b2: a + b2,
                                                 [hit[r:r + sub] for r in range(0, TK, sub)])
                cnt = jnp.sum(acc.astype(F32), axis=0, keepdims=True)
                return jnp.where(cnt >= kf, cand, u)

            return lax.fori_loop(0, nbits, step, u_init)

        return lax.switch(nkb - 1, [functools.partial(run, n) for n in range(1, nb_max + 1)], u0)

    def for_tiles(fn):
        def body(jb, c):
            for t in range(KB_TILES):
                fn(jb * KB_TILES + t)
            return c
        lax.fori_loop(0, nkb, body, 0)

    zero = jnp.zeros((1, TQ), I32)
    t_hi = search_half(16, zero) - 32768

    def low_tile(j):
        k = keys_ref[j]
        hi = jnp.right_shift(k, 16)
        lo = (k & 0xFFFF) - 32768
        half_ref[j] = jnp.where(hi > t_hi, 32767, jnp.where(hi < t_hi, -32768, lo)).astype(i16)

    for_tiles(low_tile)
    thr = t_hi * 65536 + search_half(16, zero)

    def count_ge(jb, acc):
        for t in range(KB_TILES):
            acc = acc + jnp.where(keys_ref[jb * KB_TILES + t] >= thr, 1.0, 0.0)
        return acc

    n_ge = jnp.sum(lax.fori_loop(0, nkb, count_ge, jnp.zeros((TK, TQ), F32)), axis=0, keepdims=True)

    @pl.when(jnp.max(jnp.where(thr > INT_MIN, n_ge, 0.0)) > kf)
    def _():
        def tie_tile(j):
            k = keys_ref[j]
            rev = last - (j * TK + krow)
            half_ref[j] = jnp.where(k > thr, 32767, jnp.where(k == thr, rev, -32768)).astype(i16)

        for_tiles(tie_tile)
        keep = search_half(pbits, zero + 32768) - 32768

        def demote(j):
            k = keys_ref[j]
            rev = last - (j * TK + krow)
            keys_ref[j] = jnp.where((k == thr) & (rev < keep) & (thr > INT_MIN), k - 1, k)

        for_tiles(demote)

    sel_thr = jnp.maximum(thr, INT_MIN + 1)

    nslot = A_HEADS // A_PAIR
    for h in range(A_HEADS):
        qs_ref[h // A_PAIR, (h % A_PAIR) * TQ:(h % A_PAIR + 1) * TQ] = aq_ref[:, h * HEAD_DIM:(h + 1) * HEAD_DIM]
    _flash_init(m_ref, acc_ref)

    def att_block(jb, c):
        r0 = pl.multiple_of(jb * kb, kb)
        kt = ak_ref[pl.ds(r0, kb), 0:HEAD_DIM]
        j0 = jb * KB_TILES
        masks = []
        for t in range(KB_TILES):
            mk = keys_ref[j0 + t] >= sel_thr
            masks.append(jnp.concatenate([mk] * A_PAIR, axis=1))
        d = [_bias_tile_index(i, j0 + t, TQ, nd) * nslot for t in range(KB_TILES)]
        vat = jnp.concatenate([avt_ref[j0 + t] for t in range(KB_TILES)], axis=1)
        _flash_block_t(qs_ref, range(nslot), lambda p: 0, [kt], [vat], lambda p, t: bias_ref[d[t] + p], [masks],
                       m_ref, acc_ref)
        return c

    lax.fori_loop(0, nkb, att_block, 0)
    outs = []
    for h in range(A_HEADS):
        acc = acc_ref[h // A_PAIR][:, (h % A_PAIR) * TQ:(h % A_PAIR + 1) * TQ]
        outs.append(acc[:HEAD_DIM] / jnp.maximum(acc[HEAD_DIM:HEAD_DIM + 1], 1e-30))
    o_ref[...] = jnp.concatenate(outs, axis=0).T.astype(o_ref.dtype)


def _dsa(iq, iwt, ik, aq, ak, avt, bias, nd, bsz, seq):
    TQ = A_TQ
    nq = seq // TQ
    kb = KB_TILES * TK
    topk = min(A_TOPK_MAX, seq // 4)
    pbits = max(1, (seq - 1).bit_length())
    return pl.pallas_call(
        functools.partial(_dsa_kernel, topk, nd, pbits),
        grid=(bsz, nq),
        in_specs=[
            pl.BlockSpec((TQ, 256), lambda b, i: (b * nq + i, 0)),
            pl.BlockSpec((8, TQ), lambda b, i: (0, b * nq + i)),
            pl.BlockSpec((seq, LANES), lambda b, i: (b, 0)),
            pl.BlockSpec((TQ, 512), lambda b, i: (b * nq + i, 0)),
            pl.BlockSpec((seq, LANES), lambda b, i: (b, 0)),
            pl.BlockSpec((seq // TK, LANES, TK), lambda b, i: (b, 0, 0)),
            pl.BlockSpec(bias.shape, lambda b, i: (0, 0, 0), pipeline_mode=pl.Buffered(1)),
        ],
        out_specs=pl.BlockSpec((TQ, 512), lambda b, i: (b * nq + i, 0)),
        out_shape=jax.ShapeDtypeStruct((bsz * seq, 512), BF16),
        scratch_shapes=[pltpu.VMEM((seq // TK, TK, TQ), I32),
                        pltpu.VMEM((seq // TK, TK, TQ), jnp.int16),
                        pltpu.VMEM((A_IDX_HEADS * TQ, HEAD_DIM), BF16),
                        pltpu.VMEM((A_HEADS // A_PAIR, A_PAIR * TQ, HEAD_DIM), BF16),
                        pltpu.VMEM((A_HEADS // A_PAIR, 8, A_PAIR * TQ), F32),
                        pltpu.VMEM((A_HEADS // A_PAIR, LANES, A_PAIR * TQ), F32)],
        compiler_params=_params(("parallel", "arbitrary")),
        name="dsa",
    )(iq, iwt, ik, aq, ak, avt, bias)


def _nsa_cmp_kernel(x_ref, pos_ref, wlo_ref, whi_ref, gain_ref, k_ref, v_ref):
    x = x_ref[...]
    lo = _dot((x + pos_ref[0:1, :]).astype(BF16), wlo_ref[...])
    hi = _dot((x + pos_ref[1:2, :]).astype(BF16), whi_ref[...])
    nrow = x.shape[0]
    pre = lo + pltpu.roll(hi, nrow - 1, 0)
    ks = []
    for g in range(B_KV_HEADS):
        kg = pre[:, g * HEAD_DIM:(g + 1) * HEAD_DIM]
        ms = jnp.mean(kg * kg, axis=-1, keepdims=True)
        ks.append(kg * lax.rsqrt(ms + NORM_EPS) * gain_ref[...])
    k_ref[...] = jnp.concatenate(ks, axis=1).astype(k_ref.dtype)
    v_ref[...] = pre[:, LANES:2 * LANES].T.astype(v_ref.dtype)


def _nsa_cmp(bcmp, cmp_pos, cmp_w, k_gain, bsz, seq):
    nch = seq // B_CMP_STRIDE
    half = B_CMP_LEN // 2
    width = half * 256
    x = bcmp.reshape(bsz * nch, width)

    def wmat(l0):
        w = jnp.zeros((half, 4, HEAD_DIM, 4, HEAD_DIM), F32)
        for j in range(4):
            w = w.at[:, j, :, j, :].set(cmp_w[j // 2, l0:l0 + half])
        return w.reshape(width, 256).astype(BF16)

    def prow(l0):
        p = jnp.stack([cmp_pos[0, l0:l0 + half], cmp_pos[0, l0:l0 + half],
                       cmp_pos[1, l0:l0 + half], cmp_pos[1, l0:l0 + half]], axis=1)
        return p.reshape(width)

    pos = jnp.stack([prow(0), prow(half)]).astype(F32)
    return pl.pallas_call(
        _nsa_cmp_kernel,
        grid=(bsz,),
        in_specs=[
            pl.BlockSpec((nch, width), lambda b: (b, 0)),
            pl.BlockSpec((2, width), lambda b: (0, 0)),
            pl.BlockSpec((width, 256), lambda b: (0, 0)),
            pl.BlockSpec((width, 256), lambda b: (0, 0)),
            pl.BlockSpec((1, HEAD_DIM), lambda b: (0, 0)),
        ],
        out_specs=[pl.BlockSpec((nch, LANES), lambda b: (b, 0)), pl.BlockSpec((LANES, nch), lambda b: (b, 0))],
        out_shape=[jax.ShapeDtypeStruct((bsz * nch, LANES), BF16), jax.ShapeDtypeStruct((bsz * LANES, nch), BF16)],
        compiler_params=_params(("parallel",)),
        name="nsa_cmp",
    )(x, pos, wmat(0), wmat(half), k_gain.reshape(1, HEAD_DIM).astype(F32))


def _nsa_kernel(seq, nd, q_ref, gt_ref, kc_ref, vct_ref, k_ref, vt_ref, bias_ref, o_ref,
                imp_ref, sel_ref, qs_ref, m_ref, acc_ref):
    TQ = B_TQ
    rq = TQ // TK
    i = pl.program_id(1)
    ncp = seq // B_CMP_STRIDE
    ns = seq // B_SEL_LEN
    n_top = min(B_SEL_TOPK_MAX, ns)
    nslot = B_HEADS // B_PAIR
    spg = B_GROUP // B_PAIR
    krow = lax.broadcasted_iota(I32, (TK, TQ), 0)
    qpos = i * TQ + lax.broadcasted_iota(I32, (TK, TQ), 1)

    def lanes(h):
        return slice((h % B_PAIR) * TQ, (h % B_PAIR + 1) * TQ)

    def dup(x):
        return jnp.concatenate([x] * B_PAIR, axis=1)

    for h in range(B_HEADS):
        qs_ref[h // B_PAIR, lanes(h)] = q_ref[:, h * HEAD_DIM:(h + 1) * HEAD_DIM]

    n_idx = lax.broadcasted_iota(I32, (ncp, TQ), 0)
    t_c = i * TQ + lax.broadcasted_iota(I32, (ncp, TQ), 1)
    cmask = dup(n_idx * B_CMP_STRIDE + (B_CMP_LEN - 1) <= t_c)
    om = lax.broadcasted_iota(I32, (ns, ncp), 0) * B_SEL_LEN
    on = lax.broadcasted_iota(I32, (ns, ncp), 1) * B_CMP_STRIDE
    ovt = jnp.where((on < om + B_SEL_LEN) & (on + B_CMP_LEN > om), 1.0, 0.0).astype(BF16)
    m_idx = lax.broadcasted_iota(I32, (ns, TQ), 0)
    jt = (i * TQ + lax.broadcasted_iota(I32, (ns, TQ), 1)) // B_SEL_LEN
    forced = (m_idx == 0) | (m_idx == jt) | (m_idx == jt - 1)

    kc = [kc_ref[:, g * HEAD_DIM:(g + 1) * HEAD_DIM] for g in range(B_KV_HEADS)]
    vct = [vct_ref[g * HEAD_DIM:(g + 1) * HEAD_DIM, :] for g in range(B_KV_HEADS)]
    st_all = [_dot_t(kc[p // spg], qs_ref[p]) for p in range(nslot)]
    pts = []
    for p in range(nslot):
        st = jnp.where(cmask, st_all[p], -jnp.inf)
        mx = jnp.max(st, axis=0, keepdims=True)
        mx = jnp.where(mx == -jnp.inf, 0.0, mx)
        e = jnp.exp2(st - mx)
        pts.append(e / jnp.maximum(jnp.sum(e, axis=0, keepdims=True), 1e-30))
    oc = [_dot(vct[p // spg], pts[p].astype(BF16)) for p in range(nslot)]

    for g in range(B_KV_HEADS):
        psum = functools.reduce(lambda a, b: a + b, [pts[h // B_PAIR][:, lanes(h)]
                                                     for h in range(g * B_GROUP, (g + 1) * B_GROUP)])
        imp = _split_dot_left(ovt, psum)
        imp = jnp.where(forced, jnp.inf, jnp.where(m_idx <= jt, imp, -jnp.inf))

        sub = 8
        imp_ref[g] = imp
        grp = [imp[r:r + sub] for r in range(0, ns, sub)]
        rank = [jnp.zeros((sub, TQ), F32) for _ in grp]
        rsub = lax.broadcasted_iota(I32, (sub, TQ), 0)
        for mp in range(ns):
            vp = jnp.broadcast_to(imp_ref[g, mp:mp + 1, :], (sub, TQ))
            for r in range(len(grp)):
                if r * sub + sub - 1 < mp:
                    before = vp > grp[r]
                elif r * sub > mp:
                    before = vp >= grp[r]
                else:
                    before = (vp > grp[r]) | ((vp == grp[r]) & (rsub + r * sub > mp))
                rank[r] = rank[r] + jnp.where(before, 1.0, 0.0)
        rank = jnp.concatenate(rank, axis=0)
        sel_ref[g] = jnp.where((rank < float(n_top)) & (m_idx <= jt), 1.0, 0.0)

    def slot_bias(p, j):
        parts = []
        for u in range(B_PAIR):
            for sub in range(rq):
                k = jnp.clip(rq * i + sub - j, 0, nd - 1)
                parts.append(bias_ref[k * B_HEADS + p * B_PAIR + u])
        return jnp.concatenate(parts, axis=1)

    def branch_block(j0, ntiles, koff, vrow, mask_fn):
        rows = pl.ds(pl.multiple_of(j0 * TK, TK), ntiles * TK)
        kt = [k_ref[rows, koff + g * LANES:koff + g * LANES + HEAD_DIM] for g in range(B_KV_HEADS)]
        vt = [jnp.concatenate([vt_ref[j0 + t, vrow + g * LANES:vrow + (g + 1) * LANES, :] for t in range(ntiles)],
                              axis=1) for g in range(B_KV_HEADS)]
        masks = [[dup(mask_fn(g, j0 + t)) for t in range(ntiles)] for g in range(B_KV_HEADS)]
        _flash_block_t(qs_ref, range(nslot), lambda p: p // spg, kt, vt, lambda p, t: slot_bias(p, j0 + t),
                       masks, m_ref, acc_ref)

    bpt = TK // B_SEL_LEN

    def sel_mask(g, j):
        chosen = jnp.concatenate([jnp.broadcast_to(sel_ref[g, pl.ds(j * bpt + b, 1), :], (B_SEL_LEN, TQ))
                                  for b in range(bpt)], axis=0) > 0.5
        return chosen & (j * TK + krow <= qpos)

    def win_mask(g, j):
        dist = qpos - (j * TK + krow)
        return (dist >= 0) & (dist < B_WINDOW)

    def sel_body(jb, c):
        branch_block(jb * KB_TILES, KB_TILES, 0, 0, sel_mask)
        return c

    gt = gt_ref[...]

    def gate(h, br):
        return gt[3 * h + br:3 * h + br + 1]

    def head_out(h):
        a = acc_ref[h // B_PAIR][:, lanes(h)]
        return a[:HEAD_DIM] / jnp.maximum(a[HEAD_DIM:HEAD_DIM + 1], 1e-30)

    _flash_init(m_ref, acc_ref)
    lax.fori_loop(0, (rq * (i + 1) + KB_TILES - 1) // KB_TILES, sel_body, 0)
    part = [gate(h, 0) * oc[h // B_PAIR][:, lanes(h)] + gate(h, 1) * head_out(h) for h in range(B_HEADS)]

    wt = B_WINDOW // TK + rq
    _flash_init(m_ref, acc_ref)
    branch_block(jnp.maximum(rq * (i + 1) - wt, 0), wt, 2 * LANES, 2 * LANES, win_mask)
    outs = [part[h] + gate(h, 2) * head_out(h) for h in range(B_HEADS)]
    o_ref[...] = jnp.concatenate(outs, axis=0).T.astype(o_ref.dtype)


def _nsa(bq, bgt, kcmp, vcmpt, bk, bvt, bias, nd, bsz, seq):
    TQ = B_TQ
    nq = seq // TQ
    ncp = seq // B_CMP_STRIDE
    ns = seq // B_SEL_LEN
    nslot, w = B_HEADS // B_PAIR, B_PAIR * TQ
    return pl.pallas_call(
        functools.partial(_nsa_kernel, seq, nd),
        grid=(bsz, nq),
        in_specs=[
            pl.BlockSpec((TQ, 512), lambda b, i: (b * nq + i, 0)),
            pl.BlockSpec((32, TQ), lambda b, i: (0, b * nq + i)),
            pl.BlockSpec((ncp, LANES), lambda b, i: (b, 0)),
            pl.BlockSpec((LANES, ncp), lambda b, i: (b, 0)),
            pl.BlockSpec((seq, 512), lambda b, i: (b, 0)),
            pl.BlockSpec((seq // TK, 4 * LANES, TK), lambda b, i: (b, 0, 0)),
            pl.BlockSpec(bias.shape, lambda b, i: (0, 0, 0), pipeline_mode=pl.Buffered(1)),
        ],
        out_specs=pl.BlockSpec((TQ, 512), lambda b, i: (b * nq + i, 0)),
        out_shape=jax.ShapeDtypeStruct((bsz * seq, 512), BF16),
        scratch_shapes=[pltpu.VMEM((B_KV_HEADS, ns, TQ), F32), pltpu.VMEM((B_KV_HEADS, ns, TQ), F32),
                        pltpu.VMEM((nslot, w, HEAD_DIM), BF16),
                        pltpu.VMEM((nslot, 8, w), F32),
                        pltpu.VMEM((nslot, LANES, w), F32)],
        compiler_params=_params(("parallel", "arbitrary")),
        name="nsa",
    )(bq, bgt, kcmp, vcmpt, bk, bvt, bias)


def _nsa_kernel_rows(seq, nd, q_ref, g_ref, kc_ref, vc_ref, kv_ref, bias_ref, o_ref,
                     imp_ref, sel_ref, qs_ref, m_ref, acc_ref):
    i = pl.program_id(1)
    ncp = seq // B_CMP_STRIDE
    ns = seq // B_SEL_LEN
    n_top = min(B_SEL_TOPK_MAX, ns)
    hg = B_GROUP
    row = lax.broadcasted_iota(I32, (TQ, TK), 0)
    col = lax.broadcasted_iota(I32, (TQ, TK), 1)
    tpos = i * TQ + row
    _split_heads(q_ref, qs_ref, B_HEADS)

    n_idx = lax.broadcasted_iota(I32, (ncp, TQ), 0)
    t_c = i * TQ + lax.broadcasted_iota(I32, (ncp, TQ), 1)
    cmask = n_idx * B_CMP_STRIDE + (B_CMP_LEN - 1) <= t_c
    om = lax.broadcasted_iota(I32, (ns, ncp), 0) * B_SEL_LEN
    on = lax.broadcasted_iota(I32, (ns, ncp), 1) * B_CMP_STRIDE
    ovt = jnp.where((on < om + B_SEL_LEN) & (on + B_CMP_LEN > om), 1.0, 0.0).astype(BF16)
    m_idx = lax.broadcasted_iota(I32, (ns, TQ), 0)
    jt = (i * TQ + lax.broadcasted_iota(I32, (ns, TQ), 1)) // B_SEL_LEN
    forced = (m_idx == 0) | (m_idx == jt) | (m_idx == jt - 1)

    kc = [kc_ref[:, g * HEAD_DIM:(g + 1) * HEAD_DIM] for g in range(B_KV_HEADS)]
    vc = [vc_ref[:, g * HEAD_DIM:(g + 1) * HEAD_DIM] for g in range(B_KV_HEADS)]
    st_all = [_dot_t(kc[h // hg], qs_ref[h]) for h in range(B_HEADS)]
    pts = []
    for h in range(B_HEADS):
        st = jnp.where(cmask, st_all[h], -jnp.inf)
        mx = jnp.max(st, axis=0, keepdims=True)
        mx = jnp.where(mx == -jnp.inf, 0.0, mx)
        e = jnp.exp(st - mx)
        pts.append(e / jnp.maximum(jnp.sum(e, axis=0, keepdims=True), 1e-30))
    oc = [_dot(pts[h].T.astype(BF16), vc[h // hg]) for h in range(B_HEADS)]

    for g in range(B_KV_HEADS):
        psum = functools.reduce(lambda a, b: a + b, pts[g * hg:(g + 1) * hg])
        hi = psum.astype(BF16)
        lo = (psum - hi.astype(F32)).astype(BF16)
        imp = _dot(ovt, hi) + _dot(ovt, lo)
        imp = jnp.where(forced, jnp.inf, jnp.where(m_idx <= jt, imp, -jnp.inf))
        imp_ref[g] = imp

        def rank_step(mp, rank, g=g, imp=imp):
            vp = jnp.broadcast_to(imp_ref[g, pl.ds(mp, 1), :], (ns, TQ))
            before = (vp > imp) | ((vp == imp) & (mp < m_idx))
            return rank + jnp.where(before, 1.0, 0.0)

        rank = lax.fori_loop(0, ns, rank_step, jnp.zeros((ns, TQ), F32), unroll=8)
        selt = jnp.where((rank < float(n_top)) & (m_idx <= jt), 1.0, 0.0)
        selt = jnp.concatenate([selt, jnp.zeros((LANES - ns, TQ), F32)], axis=0)
        sel_ref[g] = selt.T.astype(BF16)

    def branch_block(j0, ntiles, koff, voff, mask_fn):
        r0 = pl.multiple_of(j0 * TK, TK)
        d = [_bias_tile_index(i, j0 + t, TQ, nd) * B_HEADS for t in range(ntiles)]
        rows = pl.ds(r0, ntiles * TK)
        kt = [kv_ref[rows, koff + g * LANES:koff + g * LANES + HEAD_DIM] for g in range(B_KV_HEADS)]
        va = [kv_ref[rows, voff + g * LANES:voff + (g + 1) * LANES] for g in range(B_KV_HEADS)]
        masks = [mask_fn(g, j0) for g in range(B_KV_HEADS)]
        _flash_block(qs_ref, range(B_HEADS), lambda h: h // hg, kt, va, lambda h, t: bias_ref[d[t] + h],
                     masks, m_ref, acc_ref)

    kb = KB_TILES * TK
    e_m = lax.broadcasted_iota(I32, (LANES, kb), 0)
    e_c = lax.broadcasted_iota(I32, (LANES, kb), 1) // B_SEL_LEN

    def sel_masks(g, j0):
        expand = jnp.where(e_m == j0 * (TK // B_SEL_LEN) + e_c, 1.0, 0.0).astype(BF16)
        chosen = _dot(sel_ref[g], expand) > 0.5
        return [chosen[:, t * TK:(t + 1) * TK] & ((j0 + t) * TK + col <= tpos) for t in range(KB_TILES)]

    def sel_body(jb, c):
        branch_block(jb * KB_TILES, KB_TILES, 0, 4 * LANES, sel_masks)
        return c

    gates = g_ref[...]

    def gate(h, br):
        return gates[:, 3 * h + br:3 * h + br + 1]

    _flash_init(m_ref, acc_ref)
    lax.fori_loop(0, (i + KB_TILES) // KB_TILES, sel_body, 0)
    part = [gate(h, 0) * oc[h] + gate(h, 1) * _flash_out(acc_ref[h]) for h in range(B_HEADS)]

    wt = B_WINDOW // TK + 1
    w0 = jnp.maximum(i + 1 - wt, 0)

    def win_masks(g, j0):
        out = []
        for t in range(wt):
            dist = tpos - ((j0 + t) * TK + col)
            out.append((dist >= 0) & (dist < B_WINDOW))
        return out

    _flash_init(m_ref, acc_ref)
    branch_block(w0, wt, 2 * LANES, 6 * LANES, win_masks)
    outs = [part[h] + gate(h, 2) * _flash_out(acc_ref[h]) for h in range(B_HEADS)]
    o_ref[...] = jnp.concatenate(outs, axis=1).astype(o_ref.dtype)


def _nsa_rows(bq, bg, kcmp, vcmp, bsw, bias, nd, bsz, seq):
    nq = seq // TQ
    ncp = seq // B_CMP_STRIDE
    ns = seq // B_SEL_LEN
    return pl.pallas_call(
        functools.partial(_nsa_kernel_rows, seq, nd),
        grid=(bsz, nq),
        in_specs=[
            pl.BlockSpec((TQ, 512), lambda b, i: (b * nq + i, 0)),
            pl.BlockSpec((TQ, LANES), lambda b, i: (b * nq + i, 0)),
            pl.BlockSpec((ncp, LANES), lambda b, i: (b, 0)),
            pl.BlockSpec((ncp, LANES), lambda b, i: (b, 0)),
            pl.BlockSpec((seq, 1024), lambda b, i: (b, 0)),
            pl.BlockSpec(bias.shape, lambda b, i: (0, 0, 0)),
        ],
        out_specs=pl.BlockSpec((TQ, 512), lambda b, i: (b * nq + i, 0)),
        out_shape=jax.ShapeDtypeStruct((bsz * seq, 512), BF16),
        scratch_shapes=[pltpu.VMEM((B_KV_HEADS, ns, TQ), F32), pltpu.VMEM((B_KV_HEADS, TQ, LANES), BF16),
                        pltpu.VMEM((B_HEADS, TQ, HEAD_DIM), BF16),
                        pltpu.VMEM((B_HEADS, TQ, LANES), F32),
                        pltpu.VMEM((B_HEADS, TQ, LANES), F32)],
        compiler_params=_params(("parallel", "arbitrary")),
        name="nsa",
    )(bq, bg, kcmp, vcmp, bsw, bias)


def _dil_kernel(q_ref, kp_ref, kc_ref, vp_ref, vc_ref, bias_ref, o_ref, lse_ref):
    i = pl.program_id(2)

    def rows(ref):
        return ref[...].reshape(-1, ref.shape[-1])

    q = rows(q_ref)
    nt = q.shape[0] // TQ
    k = jnp.concatenate([rows(kp_ref), rows(kc_ref)], axis=0)
    v = jnp.concatenate([rows(vp_ref), rows(vc_ref)], axis=0)
    row = lax.broadcasted_iota(I32, (TQ, 2 * TK), 0)
    col = lax.broadcasted_iota(I32, (TQ, 2 * TK), 1)
    du = row + TK - col
    window = (du >= 0) & (du <= TK)
    units = [(u, hh) for u in range(nt) for hh in range(C_HEADS_PER_GROUP)]

    def head(x, u, n, hh):
        return x[u * TQ:(u + n) * TQ, hh * HEAD_DIM:(hh + 1) * HEAD_DIM]

    scores = [_dot_t(head(q, u, 1, hh), head(k, u, 2, hh)) for u, hh in units]
    es, dens, lses = [], [], []
    for (u, hh), s in zip(units, scores):
        valid = window & ((col >= TK) | (i * nt + u > 0))
        s = jnp.where(valid, s + bias_ref[hh], -jnp.inf)
        m = jnp.max(s, axis=-1, keepdims=True)
        e = jnp.exp(s - m)
        den = jnp.sum(e, axis=-1, keepdims=True)
        es.append(e.astype(BF16))
        dens.append(den)
        lses.append(jnp.broadcast_to(m + jnp.log(den), (TQ, HEAD_DIM)))
    outs = [_dot(e, head(v, u, 2, hh)) / den for (u, hh), e, den in zip(units, es, dens)]

    def assemble(parts):
        tiles = [jnp.concatenate(parts[u * C_HEADS_PER_GROUP:(u + 1) * C_HEADS_PER_GROUP], axis=1)
                 for u in range(nt)]
        return jnp.concatenate(tiles, axis=0)

    o_ref[...] = assemble(outs).reshape(o_ref.shape)
    lse_ref[...] = assemble(lses).reshape(lse_ref.shape)


def _dilated_group(qkv, bias, dil, bsz, seq):
    ln = seq // dil
    nt = min(C_TILES_PER_STEP, ln // TQ)
    gw = C_HEADS_PER_GROUP * HEAD_DIM
    rpc = CLASS_TILE // dil
    tiles_per_seq = seq // CLASS_TILE

    def spec(col, nrows, start):
        if rpc >= nrows:
            def index(b, r, i):
                s = start(i)
                return (b * tiles_per_seq + s // rpc, r, (s % rpc) // nrows, col)
            return pl.BlockSpec((None, None, nrows, gw), index)
        per = nrows // rpc
        assert tiles_per_seq % per == 0
        return pl.BlockSpec((per, None, rpc, gw),
                            lambda b, r, i: (b * (tiles_per_seq // per) + start(i) // nrows, r, 0, col))

    def cur(col):
        return spec(col, nt * TQ, lambda i: i * (nt * TQ))

    def prev(col):
        return spec(col, TQ, lambda i: jnp.maximum(i * nt - 1, 0) * TQ)

    return pl.pallas_call(
        _dil_kernel,
        grid=(bsz, dil, ln // (nt * TQ)),
        in_specs=[cur(0), prev(1), cur(1), prev(2), cur(2),
                  pl.BlockSpec(bias.shape, lambda b, r, i: (0, 0, 0))],
        out_specs=[cur(0)] * 2,
        out_shape=[jax.ShapeDtypeStruct(qkv.shape[:3] + (gw,), F32)] * 2,
        compiler_params=_params(("parallel", "parallel", "arbitrary")),
        name=f"dilated_d{dil}",
    )(qkv, qkv, qkv, qkv, qkv, bias)


def _merge_kernel(x_ref, ya_ref, yb_ref, o0_ref, l0_ref, o1_ref, l1_ref, o2_ref, l2_ref, g_ref,
                  wa_ref, wb_ref, wc_ref, wo_ref, out_ref, nat_ref):
    def natural(k, ref):
        dil, rpc = ref.shape[0], ref.shape[1]
        if dil == 1:
            return ref[0]
        ntile = ref.shape[2] // LANES
        for r in range(dil):
            blk = ref[r]
            for c in range(ntile):
                nat_ref[k * ntile + c, pl.ds(r, rpc, stride=dil), :] = blk[:, c * LANES:(c + 1) * LANES]
        return jnp.concatenate([nat_ref[k * ntile + c] for c in range(ntile)], axis=1)

    o0, l0 = natural(0, o0_ref), natural(1, l0_ref)
    o1, l1 = natural(2, o1_ref), natural(3, l1_ref)
    o2, l2 = natural(4, o2_ref), natural(5, l2_ref)
    mx = jnp.maximum(jnp.maximum(l0, l1), l2)
    e0, e1, e2 = jnp.exp(l0 - mx), jnp.exp(l1 - mx), jnp.exp(l2 - mx)
    yc = (e0 * o0 + e1 * o1 + e2 * o2) / (e0 + e1 + e2)
    ya = _dot(ya_ref[...], wa_ref[...])
    yb = _dot(yb_ref[...], wb_ref[...])
    yc = _dot(yc.astype(BF16), wc_ref[...])
    d = D_MODEL
    z = g_ref[:, 0:d] * ya + g_ref[:, d:2 * d] * yb + g_ref[:, 2 * d:3 * d] * yc
    out_ref[...] = x_ref[...] + _dot(z.astype(BF16), wo_ref[...])


def _merge(x2d, ya, yb, c_outs, mixg, wa, wb, wc, wo, tm):
    m = x2d.shape[0]

    def rows(w):
        return pl.BlockSpec((tm, w), lambda i: (i, 0))

    def full(a):
        return pl.BlockSpec(a.shape, lambda i: (0, 0))

    assert tm == CLASS_TILE
    c_flat = [a for pair in c_outs for a in pair]
    c_specs = [pl.BlockSpec((None,) + a.shape[1:], lambda i: (i, 0, 0, 0)) for a in c_flat]
    gw = C_HEADS_PER_GROUP * HEAD_DIM
    return pl.pallas_call(
        _merge_kernel,
        grid=(m // tm,),
        in_specs=[rows(D_MODEL), rows(512), rows(512)] + c_specs + [rows(3 * D_MODEL)]
                 + [full(wa), full(wb), full(wc), full(wo)],
        out_specs=rows(D_MODEL),
        out_shape=jax.ShapeDtypeStruct((m, D_MODEL), F32),
        scratch_shapes=[pltpu.VMEM((len(c_flat) * gw // LANES, tm, LANES), F32)],
        compiler_params=_params(("parallel",)),
        name="merge",
    )(x2d, ya, yb, *c_flat, mixg, wa, wb, wc, wo)


def _ffn_kernel(x_ref, g_ref, wg_ref, wu_ref, wd_ref, out_ref):
    x = x_ref[...]
    ms = jnp.mean(x * x, axis=-1, keepdims=True)
    h = (x * lax.rsqrt(ms + NORM_EPS) * g_ref[...]).astype(BF16)
    gate = _dot(h, wg_ref[...])
    up = _dot(h, wu_ref[...])
    act = gate / (1.0 + jnp.exp(-gate)) * up
    out_ref[...] = x + _dot(act.astype(BF16), wd_ref[...])


def _ffn(x2d, gain, w_in, w_out, tm):
    m = x2d.shape[0]
    wg = w_in[:, :D_FF].astype(BF16)
    wu = w_in[:, D_FF:].astype(BF16)
    wd = w_out.astype(BF16)

    def full(a):
        return pl.BlockSpec(a.shape, lambda i: (0, 0))

    return pl.pallas_call(
        _ffn_kernel,
        grid=(m // tm,),
        in_specs=[pl.BlockSpec((tm, D_MODEL), lambda i: (i, 0)), pl.BlockSpec((1, D_MODEL), lambda i: (0, 0)),
                  full(wg), full(wu), full(wd)],
        out_specs=pl.BlockSpec((tm, D_MODEL), lambda i: (i, 0)),
        out_shape=jax.ShapeDtypeStruct((m, D_MODEL), F32),
        compiler_params=_params(("parallel",)),
        name="ffn",
    )(x2d, gain.reshape(1, D_MODEL).astype(F32), wg, wu, wd)


def _layer(x2d, bsz, seq, norm1_g, norm2_g, w_in, qk, cmp_pos, cmp_w, w_a, w_b, w_c, w_out, w_ffn_in, w_ffn_out,
           bias_a, bias_b, bias_c):
    pieces_ab, pieces_c, pieces_g = _proj_pieces(w_in, qk)
    aq, ak, avt, iwt, iq, ik, bq, bcmp, bk, bvt, bgt = _proj(x2d, norm1_g, pieces_ab, 512)
    c_qkv = _proj(x2d, norm1_g, pieces_c, CLASS_TILE)
    (mixg,) = _proj(x2d, norm1_g, pieces_g, 512)

    ya = _dsa(iq, iwt, ik, aq, ak, avt, *bias_a, bsz, seq)
    kcmp, vcmpt = _nsa_cmp(bcmp, cmp_pos, cmp_w, qk[3], bsz, seq)
    yb = _nsa(bq, bgt, kcmp, vcmpt, bk, bvt, *bias_b, bsz, seq)
    c_outs = [_dilated_group(c_qkv[g], bias_c[g], dil, bsz, seq) for g, (_, dil) in enumerate(C_GROUPS)]

    x1 = _merge(x2d, ya, yb, c_outs, mixg, w_a.astype(BF16), w_b.astype(BF16), w_c.astype(BF16),
                w_out.astype(BF16), CLASS_TILE)
    return _ffn(x1, norm2_g, w_ffn_in, w_ffn_out, 256)


def kernel(x, norm1_g, norm2_g, w_in, qk_norm_g, nsa_cmp_pos, nsa_cmp_w, w_branch_a, w_branch_b, w_branch_c, w_out, w_ffn_in, w_ffn_out, rel_bias):
    bsz, seq, d = x.shape
    assert d == D_MODEL and seq % (TQ * max(dil for _, dil in C_GROUPS)) == 0 and seq % A_TQ == 0
    assert seq % (KB_TILES * TK) == 0 and seq % B_TQ == 0 and seq >= B_WINDOW + B_TQ
    for win, dil in C_GROUPS:
        assert win == TK * dil
    bias_a = _toeplitz_bias(rel_bias[:, :A_HEADS], seq, A_TQ, keys_on_rows=True, group=A_PAIR, scale=LOG2E)
    bias_b = _toeplitz_bias(rel_bias[:, A_HEADS:A_HEADS + B_HEADS], seq, TK, keys_on_rows=True, scale=LOG2E)
    rel_c = rel_bias[:, A_HEADS + B_HEADS:]
    bias_c = [_dilated_bias(rel_c[:, g * C_HEADS_PER_GROUP:(g + 1) * C_HEADS_PER_GROUP], dil)
              for g, (_, dil) in enumerate(C_GROUPS)]
    x2d = x.reshape(bsz * seq, d)
    for layer in range(norm1_g.shape[0]):
        x2d = _layer(x2d, bsz, seq, norm1_g[layer], norm2_g[layer], w_in[layer], qk_norm_g[layer],
                     nsa_cmp_pos[layer], nsa_cmp_w[layer], w_branch_a[layer], w_branch_b[layer],
                     w_branch_c[layer], w_out[layer], w_ffn_in[layer], w_ffn_out[layer],
                     bias_a, bias_b, bias_c)
    return x2d.reshape(bsz, seq, d)
```

```python
import functools
import math

import numpy as np
import jax
import jax.numpy as jnp
from jax import lax
from jax.experimental import pallas as pl
from jax.experimental.pallas import tpu as pltpu

F32 = jnp.float32
BF16 = jnp.bfloat16
I32 = jnp.int32

D_MODEL = 1024
HEAD_DIM = 64
NORM_EPS = 1e-6
REL_BUCKETS = 32
REL_MAX_DIST = 2048

A_HEADS = 8
A_IDX_HEADS = 4
A_TOPK_MAX = 256
B_HEADS = 8
B_KV_HEADS = 2
B_GROUP = B_HEADS // B_KV_HEADS
B_CMP_LEN = 32
B_CMP_STRIDE = 16
B_SEL_LEN = 64
B_SEL_TOPK_MAX = 16
B_WINDOW = 512
C_GROUPS = ((128, 1), (512, 4), (2048, 16))
C_HEADS_PER_GROUP = 4
C_HEADS = C_HEADS_PER_GROUP * len(C_GROUPS)
D_FF = ((8 * D_MODEL + 3 * 256 - 1) // (3 * 256)) * 256

_O_AQ = 0
_O_AK = _O_AQ + A_HEADS * HEAD_DIM
_O_AV = _O_AK + HEAD_DIM
_O_IQ = _O_AV + HEAD_DIM
_O_IK = _O_IQ + A_IDX_HEADS * HEAD_DIM
_O_IW = _O_IK + HEAD_DIM
_O_BQ = _O_IW + A_IDX_HEADS
_O_BKV = _O_BQ + B_HEADS * HEAD_DIM
_O_BG = _O_BKV + 6 * B_KV_HEADS * HEAD_DIM
_O_CQ = _O_BG + 3 * B_HEADS
_O_CK = _O_CQ + C_HEADS * HEAD_DIM
_O_CV = _O_CK + C_HEADS * HEAD_DIM
_O_MIX = _O_CV + C_HEADS * HEAD_DIM
_O_END = _O_MIX + 3 * D_MODEL

TQ = 128
TK = 128
KB_TILES = 4
A_TQ = 256
A_PAIR = 2
CLASS_TILE = 512
C_TILES_PER_STEP = 4
B_TQ = 256
B_PAIR = 2
LOG2E = 1.4426950408889634
LANES = 128
VMEM_LIMIT = 56 * 1024 * 1024
INT_MIN = -2 ** 31
NEG_INIT = -1e30
SCALE = HEAD_DIM ** -0.5


def _params(sem):
    return pltpu.CompilerParams(dimension_semantics=sem, vmem_limit_bytes=VMEM_LIMIT)


def _dot_t(a, b):
    return lax.dot_general(a, b, (((1,), (1,)), ((), ())), preferred_element_type=F32)


def _dot(a, b):
    return jnp.dot(a, b, preferred_element_type=F32)


def _split_dot_left(a_bf16, b):
    hi = b.astype(BF16)
    lo = (b - hi.astype(F32)).astype(BF16)
    return _dot(a_bf16, hi) + _dot(a_bf16, lo)


def _split_dot(a, b_bf16):
    hi = a.astype(BF16)
    lo = (a - hi.astype(F32)).astype(BF16)
    return _dot(hi, b_bf16) + _dot(lo, b_bf16)


def _bucket_table(n_max):
    n = np.arange(n_max, dtype=np.int64)
    exact = REL_BUCKETS // 2
    nf = np.maximum(n, 1).astype(np.float32)
    large = exact + (np.log(nf / np.float32(exact)) / np.float32(math.log(REL_MAX_DIST / exact))
                     * np.float32(REL_BUCKETS - exact)).astype(np.int32)
    return np.where(n < exact, n, np.minimum(large, REL_BUCKETS - 1)).astype(np.int32)


def _num_bias_tiles(seq, tq):
    bucket = _bucket_table(seq + tq)
    first_sat = int(np.min(np.nonzero(bucket == REL_BUCKETS - 1)[0]))
    assert np.all(bucket[first_sat:] == REL_BUCKETS - 1)
    nd = -(-(first_sat + TK - 1) // TK) + tq // TK
    return min(nd, seq // TK)


def _bias_tile_index(i, j, tq, nd):
    return jnp.clip((tq // TK) * (i + 1) - 1 - j, 0, nd - 1)


def _bias_kernel(nh, group, scale, idx_ref, rel_ref, o_ref):
    idx = idx_ref[0]
    c = idx.shape[1]
    acc = [jnp.zeros(idx.shape, F32) for _ in range(nh)]
    for b in range(REL_BUCKETS):
        hit = idx == b
        for h in range(nh):
            acc[h] = jnp.where(hit, rel_ref[b, h] * scale, acc[h])
    for h in range(nh):
        o_ref[h // group, :, (h % group) * c:(h % group + 1) * c] = acc[h]


def _bias_tiles(rel_cols, idx, group=1, scale=1.0):
    n, r, c = idx.shape
    nh = rel_cols.shape[1]
    return pl.pallas_call(
        functools.partial(_bias_kernel, nh, group, scale),
        grid=(n,),
        in_specs=[pl.BlockSpec((1, r, c), lambda k: (k, 0, 0)),
                  pl.BlockSpec(memory_space=pltpu.SMEM)],
        out_specs=pl.BlockSpec((nh // group, r, group * c), lambda k: (k, 0, 0)),
        out_shape=jax.ShapeDtypeStruct((n * nh // group, r, group * c), F32),
        compiler_params=_params(("parallel",)),
        name="bias_tiles",
    )(jnp.asarray(idx, I32), rel_cols.astype(F32))


def _toeplitz_bias(rel_cols, seq, tq=TQ, keys_on_rows=False, group=1, scale=1.0):
    nd = _num_bias_tiles(seq, tq)
    bucket = _bucket_table(seq + tq)
    d = ((np.arange(nd)[:, None, None] - (tq // TK - 1)) * TK
         + np.arange(tq)[None, :, None] - np.arange(TK)[None, None, :])
    idx = bucket[np.clip(d, 0, None)]
    return _bias_tiles(rel_cols, idx.transpose(0, 2, 1) if keys_on_rows else idx, group, scale), nd


def _dilated_bias(rel_cols, dil):
    bucket = _bucket_table(2 * TK * dil + 1)
    du = np.arange(TQ)[:, None] + TK - np.arange(2 * TK)[None, :]
    return _bias_tiles(rel_cols, bucket[np.clip(du, 0, None) * dil][None])


def _proj_kernel(kinds, *refs):
    n = len(kinds)
    x_ref, g_ref, gs_ref = refs[0], refs[1], refs[2]
    w_refs = refs[3:3 + 2 * n:2]
    aux_refs = refs[4:4 + 2 * n:2]
    n_out = sum(2 if isinstance(k, tuple) and k[0] == "trans" else 1 for k in kinds)
    out_refs = iter(refs[3 + 2 * n:3 + 2 * n + n_out])
    scratch_refs = refs[3 + 2 * n + n_out:]
    x = x_ref[...]
    ms = jnp.mean(x * x, axis=-1, keepdims=True)
    h = (x * lax.rsqrt(ms + NORM_EPS) * g_ref[...]).astype(BF16)
    for kind, w_ref, aux_ref in zip(kinds, w_refs, aux_refs):
        if isinstance(kind, tuple) and kind[0] == "trans":
            _, nv, nextra, act = kind
            vt_ref, ex_ref = next(out_refs), next(out_refs)
            yt = _dot_t(w_ref[...], h)
            rows = lax.broadcasted_iota(I32, (nv * LANES, yt.shape[1]), 0)
            vt = (yt[0:nv * LANES] + jnp.where(rows % LANES >= HEAD_DIM, 1.0, 0.0)).astype(vt_ref.dtype)
            for c in range(vt_ref.shape[0]):
                vt_ref[c] = vt[:, c * TK:(c + 1) * TK]
            ex = yt[nv * LANES:nv * LANES + nextra]
            ex_ref[...] = 1.0 / (1.0 + jnp.exp(-ex)) if act == "sigmoid" else ex
            continue
        o_ref = next(out_refs)
        classes = kind[1] if isinstance(kind, tuple) else 0
        width = w_ref.shape[1]
        cw = 256 if width % 256 == 0 else LANES
        ys = [_dot(h, w_ref[:, c0:c0 + cw]) for c0 in range(0, width, cw)]
        for c0, y in zip(range(0, width, cw), ys):
            if kind == "norm" or classes:
                gsum = _split_dot(y * y, gs_ref[:cw, :cw])
                r = lax.rsqrt(gsum * (1.0 / HEAD_DIM) + NORM_EPS)
                mask = aux_ref[0:1, c0:c0 + cw]
                fac = mask * (r * aux_ref[1:2, c0:c0 + cw]) + (1.0 - mask)
                y = y * fac + aux_ref[2:3, c0:c0 + cw]
            elif kind == "sigmoid":
                y = 1.0 / (1.0 + jnp.exp(-y))
            if classes:
                for c in range(0, cw, LANES):
                    scratch_refs[0][(c0 + c) // LANES] = y[:, c:c + LANES]
            else:
                o_ref[:, c0:c0 + cw] = y.astype(o_ref.dtype)
        for r in range(classes):
            rows = pl.ds(r, x.shape[0] // classes, stride=classes)
            o_ref[r] = jnp.concatenate([scratch_refs[0][c, rows, :] for c in range(width // LANES)],
                                       axis=1).astype(o_ref.dtype)


def _proj(x2d, gain, pieces, tm):
    m, d = x2d.shape
    kinds = tuple(p[0] for p in pieces)
    gs = (np.arange(256)[:, None] // HEAD_DIM == np.arange(256)[None, :] // HEAD_DIM)
    gs = jnp.asarray(gs, BF16)
    in_specs = [pl.BlockSpec((tm, d), lambda i: (i, 0)),
                pl.BlockSpec((1, d), lambda i: (0, 0)),
                pl.BlockSpec((256, 256), lambda i: (0, 0))]
    args = [x2d, gain.reshape(1, d).astype(F32), gs]
    out_specs, out_shapes, scratch = [], [], []
    for kind, w, aux, dt in pieces:
        in_specs += [pl.BlockSpec(w.shape, lambda i: (0, 0)), pl.BlockSpec(aux.shape, lambda i: (0, 0))]
        args += [w, aux]
        if isinstance(kind, tuple) and kind[0] == "trans":
            _, nv, nextra, _ = kind
            out_specs += [pl.BlockSpec((tm // TK, nv * LANES, TK), lambda i: (i, 0, 0)),
                          pl.BlockSpec((nextra, tm), lambda i: (0, i))]
            out_shapes += [jax.ShapeDtypeStruct((m // TK, nv * LANES, TK), dt),
                           jax.ShapeDtypeStruct((nextra, m), F32)]
            continue
        nw = w.shape[1]
        if isinstance(kind, tuple) and kind[0] == "classes":
            dil = kind[1]
            assert tm == CLASS_TILE
            out_specs.append(pl.BlockSpec((None, dil, tm // dil, nw), lambda i: (i, 0, 0, 0)))
            out_shapes.append(jax.ShapeDtypeStruct((m // tm, dil, tm // dil, nw), dt))
            scratch = [pltpu.VMEM((nw // LANES, tm, LANES), F32)]
            continue
        out_specs.append(pl.BlockSpec((tm, nw), lambda i: (i, 0)))
        out_shapes.append(jax.ShapeDtypeStruct((m, nw), dt))
    return pl.pallas_call(
        functools.partial(_proj_kernel, kinds),
        grid=(m // tm,),
        in_specs=in_specs, out_specs=out_specs, out_shape=out_shapes, scratch_shapes=scratch,
        compiler_params=_params(("parallel",)),
        name="proj",
    )(*args)


def _aux(width, mask=None, gain=None, add=None):
    z = jnp.zeros((width,), F32)
    return jnp.stack([z if mask is None else mask, z if gain is None else gain, z if add is None else add])


def _seg(*parts):
    ref = next(p for p in parts if not isinstance(p, int))
    return jnp.concatenate([jnp.zeros(ref.shape[:-1] + (p,), ref.dtype) if isinstance(p, int) else p
                            for p in parts], axis=-1)


def _proj_pieces(w_in, qk):
    w = w_in.astype(BF16)
    hd = HEAD_DIM
    ones, zeros = jnp.ones((hd,), F32), jnp.zeros((hd,), F32)
    cat = jnp.concatenate

    def cols(a, b):
        return w[:, a:b]

    pieces_ab = [
        ("norm", cols(_O_AQ, _O_AK), _aux(512, jnp.ones((512,), F32), jnp.tile(qk[0], A_HEADS) * (SCALE * LOG2E)),
         BF16),
        ("norm", _seg(cols(_O_AK, _O_AV), hd), _aux(LANES, cat([ones, zeros]), cat([qk[1], zeros])), BF16),
        (("trans", 1, 8, "none"), _seg(cols(_O_AV, _O_IQ), hd, cols(_O_IW, _O_BQ), 16 - A_IDX_HEADS).T,
         _aux(LANES), BF16),
        ("plain", cols(_O_IQ, _O_IK), _aux(256), BF16),
        ("plain", _seg(cols(_O_IK, _O_IW), LANES - hd), _aux(LANES), BF16),
        ("norm", cols(_O_BQ, _O_BKV), _aux(512, jnp.ones((512,), F32), jnp.tile(qk[2], B_HEADS) * (SCALE * LOG2E)),
         BF16),
        ("plain", cols(_O_BKV, _O_BKV + 256), _aux(256), F32),
    ]
    o = _O_BKV + 256
    ks0, ks1, vs0, vs1, kw0, kw1, vw0, vw1 = [cols(o + i * hd, o + (i + 1) * hd) for i in range(8)]
    pieces_ab += [
        ("norm", _seg(ks0, hd, ks1, hd, kw0, hd, kw1, hd),
         _aux(512, cat([ones, zeros] * 4), cat([qk[3], zeros] * 4)), BF16),
        (("trans", 4, 32, "sigmoid"),
         _seg(vs0, hd, vs1, hd, vw0, hd, vw1, hd, cols(_O_BG, _O_CQ), 32 - 3 * B_HEADS).T, _aux(LANES), BF16),
    ]
    gw = C_HEADS_PER_GROUP * hd
    pieces_c = []
    for g, (_, dil) in enumerate(C_GROUPS):
        wg = cat([cols(o0 + g * gw, o0 + (g + 1) * gw) for o0 in (_O_CQ, _O_CK, _O_CV)], axis=1)
        mask = cat([jnp.ones((2 * gw,), F32), jnp.zeros((gw,), F32)])
        gain = cat([jnp.tile(qk[4], C_HEADS_PER_GROUP) * SCALE, jnp.tile(qk[5], C_HEADS_PER_GROUP),
                    jnp.zeros((gw,), F32)])
        pieces_c.append((("classes", dil), wg, _aux(3 * gw, mask, gain), BF16))
    pieces_g = [("sigmoid", cols(_O_MIX, _O_END), _aux(3 * D_MODEL), BF16)]
    return pieces_ab, pieces_c, pieces_g


def _flash_init(m_ref, acc_ref):
    m_ref[...] = jnp.full(m_ref.shape, NEG_INIT, F32)
    acc_ref[...] = jnp.zeros(acc_ref.shape, F32)


def _flash_block(q_ref, heads, group_of, kt, v_aug, bias_fn, masks, m_ref, acc_ref):
    heads = list(heads)
    s_all = [_dot_t(q_ref[h], kt[group_of(h)]) for h in heads]
    m_old = [m_ref[h] for h in heads]
    ps, alphas = [], []
    for k, h in enumerate(heads):
        s, mk = s_all[k], masks[group_of(h)]
        sc = [jnp.where(mk[c], s[:, c * TK:(c + 1) * TK] + bias_fn(h, c), -jnp.inf) for c in range(len(mk))]
        m_new = jnp.maximum(m_old[k], jnp.max(functools.reduce(jnp.maximum, sc), axis=-1, keepdims=True))
        ps.append(jnp.concatenate([jnp.exp(x - m_new).astype(BF16) for x in sc], axis=1))
        alphas.append(jnp.exp(m_old[k] - m_new))
        m_ref[h] = m_new
    for k, h in enumerate(heads):
        acc_ref[h] = alphas[k] * acc_ref[h] + _dot(ps[k], v_aug[group_of(h)])


def _flash_block_t(q_ref, slots, group_of, kt, v_aug_t, bias_fn, masks, m_ref, acc_ref):
    slots = list(slots)
    s_all = [_dot_t(kt[group_of(p)], q_ref[p]) for p in slots]
    m_old = [m_ref[p][0:1] for p in slots]
    ps, alphas = [], []
    for k, p in enumerate(slots):
        s, mk = s_all[k], masks[group_of(p)]
        sc = [jnp.where(mk[c], s[c * TK:(c + 1) * TK] + bias_fn(p, c), -jnp.inf) for c in range(len(mk))]
        m_new = jnp.maximum(m_old[k], jnp.max(functools.reduce(jnp.maximum, sc), axis=0, keepdims=True))
        ps.append(jnp.concatenate([jnp.exp2(x - m_new).astype(BF16) for x in sc], axis=0))
        alphas.append(jnp.exp2(m_old[k] - m_new))
        m_ref[p] = jnp.broadcast_to(m_new, m_ref.shape[1:])
    for k, p in enumerate(slots):
        acc_ref[p] = alphas[k] * acc_ref[p] + _dot(v_aug_t[group_of(p)], ps[k])


def _flash_out(acc):
    return acc[:, :HEAD_DIM] / jnp.maximum(acc[:, HEAD_DIM:HEAD_DIM + 1], 1e-30)


def _split_heads(q_ref, qs_ref, nh):
    for h in range(nh):
        qs_ref[h] = q_ref[:, h * HEAD_DIM:(h + 1) * HEAD_DIM]


def _dsa_kernel(topk, nd, pbits, iq_ref, iwt_ref, ik_ref, aq_ref, ak_ref, avt_ref, bias_ref, o_ref,
                keys_ref, half_ref, iqs_ref, qs_ref, m_ref, acc_ref):
    TQ = A_TQ
    i = pl.program_id(1)
    nk = (i + 1) * (TQ // TK)
    nkb = (nk + KB_TILES - 1) // KB_TILES
    kb = KB_TILES * TK
    krow = lax.broadcasted_iota(I32, (TK, TQ), 0)
    qpos = i * TQ + lax.broadcasted_iota(I32, (TK, TQ), 1)
    i16 = jnp.int16
    last = keys_ref.shape[0] * TK - 1

    iwt = iwt_ref[...]
    for h in range(A_IDX_HEADS):
        iqs_ref[h * TQ:(h + 1) * TQ] = iq_ref[:, h * HEAD_DIM:(h + 1) * HEAD_DIM]

    def score_block(jb, c):
        r0 = pl.multiple_of(jb * kb, kb)
        d = _dot_t(ik_ref[pl.ds(r0, kb), 0:HEAD_DIM], iqs_ref[...])
        acc = jnp.zeros((kb, TQ), F32)
        for h in range(A_IDX_HEADS):
            acc = acc + jnp.maximum(d[:, h * TQ:(h + 1) * TQ], 0.0) * iwt[h:h + 1]
        bits = lax.bitcast_convert_type(acc, I32)
        key = jnp.where(bits < 0, bits ^ 0x7FFFFFFF, bits + (last + 1))
        for t in range(KB_TILES):
            j = jb * KB_TILES + t
            kidx = j * TK + krow
            kj = jnp.where(acc[t * TK:(t + 1) * TK] == 0.0, last - kidx, key[t * TK:(t + 1) * TK])
            kj = jnp.where(kidx <= qpos, kj, INT_MIN)
            keys_ref[j] = kj
            half_ref[j] = jnp.right_shift(kj, 16).astype(i16)
        return c

    lax.fori_loop(0, nkb, score_block, 0)

    kf = float(topk)

    nb_max = keys_ref.shape[0] // KB_TILES
    sub = 16

    def search_half(nbits, u0):
        def run(nblk, u_init):
            def step(b, u):
                cand = u | jnp.left_shift(jnp.int32(1), nbits - 1 - b)
                cb = jnp.broadcast_to((cand - 32768).astype(i16), (TK, TQ))
                acc = jnp.zeros((sub, TQ), i16)
                for j in range(nblk * KB_TILES):
                    hit = jnp.where(half_ref[j] >= cb, jnp.ones((), i16), jnp.zeros((), i16))
                    acc = acc + functools.reduce(lambda a, b2: a + b2,
                                                 [hit[r:r + sub] for r in range(0, TK, sub)])
                cnt = jnp.sum(acc.astype(F32), axis=0, keepdims=True)
                return jnp.where(cnt >= kf, cand, u)

            return lax.fori_loop(0, nbits, step, u_init)

        return lax.switch(nkb - 1, [functools.partial(run, n) for n in range(1, nb_max + 1)], u0)

    def for_tiles(fn):
        def body(jb, c):
            for t in range(KB_TILES):
                fn(jb * KB_TILES + t)
            return c
        lax.fori_loop(0, nkb, body, 0)

    zero = jnp.zeros((1, TQ), I32)
    t_hi = search_half(16, zero) - 32768

    def low_tile(j):
        k = keys_ref[j]
        hi = jnp.right_shift(k, 16)
        lo = (k & 0xFFFF) - 32768
        half_ref[j] = jnp.where(hi > t_hi, 32767, jnp.where(hi < t_hi, -32768, lo)).astype(i16)

    for_tiles(low_tile)
    thr = t_hi * 65536 + search_half(16, zero)

    def count_ge(jb, acc):
        for t in range(KB_TILES):
            acc = acc + jnp.where(keys_ref[jb * KB_TILES + t] >= thr, 1.0, 0.0)
        return acc

    n_ge = jnp.sum(lax.fori_loop(0, nkb, count_ge, jnp.zeros((TK, TQ), F32)), axis=0, keepdims=True)

    @pl.when(jnp.max(jnp.where(thr > INT_MIN, n_ge, 0.0)) > kf)
    def _():
        def tie_tile(j):
            k = keys_ref[j]
            rev = last - (j * TK + krow)
            half_ref[j] = jnp.where(k > thr, 32767, jnp.where(k == thr, rev, -32768)).astype(i16)

        for_tiles(tie_tile)
        keep = search_half(pbits, zero + 32768) - 32768

        def demote(j):
            k = keys_ref[j]
            rev = last - (j * TK + krow)
            keys_ref[j] = jnp.where((k == thr) & (rev < keep) & (thr > INT_MIN), k - 1, k)

        for_tiles(demote)

    sel_thr = jnp.maximum(thr, INT_MIN + 1)

    nslot = A_HEADS // A_PAIR
    for h in range(A_HEADS):
        qs_ref[h // A_PAIR, (h % A_PAIR) * TQ:(h % A_PAIR + 1) * TQ] = aq_ref[:, h * HEAD_DIM:(h + 1) * HEAD_DIM]
    _flash_init(m_ref, acc_ref)

    def att_block(jb, c):
        r0 = pl.multiple_of(jb * kb, kb)
        kt = ak_ref[pl.ds(r0, kb), 0:HEAD_DIM]
        j0 = jb * KB_TILES
        masks = []
        for t in range(KB_TILES):
            mk = keys_ref[j0 + t] >= sel_thr
            masks.append(jnp.concatenate([mk] * A_PAIR, axis=1))
        d = [_bias_tile_index(i, j0 + t, TQ, nd) * nslot for t in range(KB_TILES)]
        vat = jnp.concatenate([avt_ref[j0 + t] for t in range(KB_TILES)], axis=1)
        _flash_block_t(qs_ref, range(nslot), lambda p: 0, [kt], [vat], lambda p, t: bias_ref[d[t] + p], [masks],
                       m_ref, acc_ref)
        return c

    lax.fori_loop(0, nkb, att_block, 0)
    outs = []
    for h in range(A_HEADS):
        acc = acc_ref[h // A_PAIR][:, (h % A_PAIR) * TQ:(h % A_PAIR + 1) * TQ]
        outs.append(acc[:HEAD_DIM] / jnp.maximum(acc[HEAD_DIM:HEAD_DIM + 1], 1e-30))
    o_ref[...] = jnp.concatenate(outs, axis=0).T.astype(o_ref.dtype)


def _dsa(iq, iwt, ik, aq, ak, avt, bias, nd, bsz, seq):
    TQ = A_TQ
    nq = seq // TQ
    kb = KB_TILES * TK
    topk = min(A_TOPK_MAX, seq // 4)
    pbits = max(1, (seq - 1).bit_length())
    return pl.pallas_call(
        functools.partial(_dsa_kernel, topk, nd, pbits),
        grid=(bsz, nq),
        in_specs=[
            pl.BlockSpec((TQ, 256), lambda b, i: (b * nq + i, 0)),
            pl.BlockSpec((8, TQ), lambda b, i: (0, b * nq + i)),
            pl.BlockSpec((seq, LANES), lambda b, i: (b, 0)),
            pl.BlockSpec((TQ, 512), lambda b, i: (b * nq + i, 0)),
            pl.BlockSpec((seq, LANES), lambda b, i: (b, 0)),
            pl.BlockSpec((seq // TK, LANES, TK), lambda b, i: (b, 0, 0)),
            pl.BlockSpec(bias.shape, lambda b, i: (0, 0, 0), pipeline_mode=pl.Buffered(1)),
        ],
        out_specs=pl.BlockSpec((TQ, 512), lambda b, i: (b * nq + i, 0)),
        out_shape=jax.ShapeDtypeStruct((bsz * seq, 512), BF16),
        scratch_shapes=[pltpu.VMEM((seq // TK, TK, TQ), I32),
                        pltpu.VMEM((seq // TK, TK, TQ), jnp.int16),
                        pltpu.VMEM((A_IDX_HEADS * TQ, HEAD_DIM), BF16),
                        pltpu.VMEM((A_HEADS // A_PAIR, A_PAIR * TQ, HEAD_DIM), BF16),
                        pltpu.VMEM((A_HEADS // A_PAIR, 8, A_PAIR * TQ), F32),
                        pltpu.VMEM((A_HEADS // A_PAIR, LANES, A_PAIR * TQ), F32)],
        compiler_params=_params(("parallel", "arbitrary")),
        name="dsa",
    )(iq, iwt, ik, aq, ak, avt, bias)


def _nsa_cmp_kernel(x_ref, pos_ref, wlo_ref, whi_ref, gain_ref, k_ref, v_ref):
    x = x_ref[...]
    lo = _dot((x + pos_ref[0:1, :]).astype(BF16), wlo_ref[...])
    hi = _dot((x + pos_ref[1:2, :]).astype(BF16), whi_ref[...])
    nrow = x.shape[0]
    pre = lo + pltpu.roll(hi, nrow - 1, 0)
    ks = []
    for g in range(B_KV_HEADS):
        kg = pre[:, g * HEAD_DIM:(g + 1) * HEAD_DIM]
        ms = jnp.mean(kg * kg, axis=-1, keepdims=True)
        ks.append(kg * lax.rsqrt(ms + NORM_EPS) * gain_ref[...])
    k_ref[...] = jnp.concatenate(ks, axis=1).astype(k_ref.dtype)
    v_ref[...] = pre[:, LANES:2 * LANES].T.astype(v_ref.dtype)


def _nsa_cmp(bcmp, cmp_pos, cmp_w, k_gain, bsz, seq):
    nch = seq // B_CMP_STRIDE
    half = B_CMP_LEN // 2
    width = half * 256
    x = bcmp.reshape(bsz * nch, width)

    def wmat(l0):
        w = jnp.zeros((half, 4, HEAD_DIM, 4, HEAD_DIM), F32)
        for j in range(4):
            w = w.at[:, j, :, j, :].set(cmp_w[j // 2, l0:l0 + half])
        return w.reshape(width, 256).astype(BF16)

    def prow(l0):
        p = jnp.stack([cmp_pos[0, l0:l0 + half], cmp_pos[0, l0:l0 + half],
                       cmp_pos[1, l0:l0 + half], cmp_pos[1, l0:l0 + half]], axis=1)
        return p.reshape(width)

    pos = jnp.stack([prow(0), prow(half)]).astype(F32)
    return pl.pallas_call(
        _nsa_cmp_kernel,
        grid=(bsz,),
        in_specs=[
            pl.BlockSpec((nch, width), lambda b: (b, 0)),
            pl.BlockSpec((2, width), lambda b: (0, 0)),
            pl.BlockSpec((width, 256), lambda b: (0, 0)),
            pl.BlockSpec((width, 256), lambda b: (0, 0)),
            pl.BlockSpec((1, HEAD_DIM), lambda b: (0, 0)),
        ],
        out_specs=[pl.BlockSpec((nch, LANES), lambda b: (b, 0)), pl.BlockSpec((LANES, nch), lambda b: (b, 0))],
        out_shape=[jax.ShapeDtypeStruct((bsz * nch, LANES), BF16), jax.ShapeDtypeStruct((bsz * LANES, nch), BF16)],
        compiler_params=_params(("parallel",)),
        name="nsa_cmp",
    )(x, pos, wmat(0), wmat(half), k_gain.reshape(1, HEAD_DIM).astype(F32))


def _nsa_kernel(seq, nd, q_ref, gt_ref, kc_ref, vct_ref, k_ref, vt_ref, bias_ref, o_ref,
                imp_ref, sel_ref, qs_ref, m_ref, acc_ref):
    TQ = B_TQ
    rq = TQ // TK
    i = pl.program_id(1)
    ncp = seq // B_CMP_STRIDE
    ns = seq // B_SEL_LEN
    n_top = min(B_SEL_TOPK_MAX, ns)
    nslot = B_HEADS // B_PAIR
    spg = B_GROUP // B_PAIR
    krow = lax.broadcasted_iota(I32, (TK, TQ), 0)
    qpos = i * TQ + lax.broadcasted_iota(I32, (TK, TQ), 1)

    def lanes(h):
        return slice((h % B_PAIR) * TQ, (h % B_PAIR + 1) * TQ)

    def dup(x):
        return jnp.concatenate([x] * B_PAIR, axis=1)

    for h in range(B_HEADS):
        qs_ref[h // B_PAIR, lanes(h)] = q_ref[:, h * HEAD_DIM:(h + 1) * HEAD_DIM]

    n_idx = lax.broadcasted_iota(I32, (ncp, TQ), 0)
    t_c = i * TQ + lax.broadcasted_iota(I32, (ncp, TQ), 1)
    cmask = dup(n_idx * B_CMP_STRIDE + (B_CMP_LEN - 1) <= t_c)
    om = lax.broadcasted_iota(I32, (ns, ncp), 0) * B_SEL_LEN
    on = lax.broadcasted_iota(I32, (ns, ncp), 1) * B_CMP_STRIDE
    ovt = jnp.where((on < om + B_SEL_LEN) & (on + B_CMP_LEN > om), 1.0, 0.0).astype(BF16)
    m_idx = lax.broadcasted_iota(I32, (ns, TQ), 0)
    jt = (i * TQ + lax.broadcasted_iota(I32, (ns, TQ), 1)) // B_SEL_LEN
    forced = (m_idx == 0) | (m_idx == jt) | (m_idx == jt - 1)

    kc = [kc_ref[:, g * HEAD_DIM:(g + 1) * HEAD_DIM] for g in range(B_KV_HEADS)]
    vct = [vct_ref[g * HEAD_DIM:(g + 1) * HEAD_DIM, :] for g in range(B_KV_HEADS)]
    st_all = [_dot_t(kc[p // spg], qs_ref[p]) for p in range(nslot)]
    pts = []
    for p in range(nslot):
        st = jnp.where(cmask, st_all[p], -jnp.inf)
        mx = jnp.max(st, axis=0, keepdims=True)
        mx = jnp.where(mx == -jnp.inf, 0.0, mx)
        e = jnp.exp2(st - mx)
        pts.append(e / jnp.maximum(jnp.sum(e, axis=0, keepdims=True), 1e-30))
    oc = [_dot(vct[p // spg], pts[p].astype(BF16)) for p in range(nslot)]

    for g in range(B_KV_HEADS):
        psum = functools.reduce(lambda a, b: a + b, [pts[h // B_PAIR][:, lanes(h)]
                                                     for h in range(g * B_GROUP, (g + 1) * B_GROUP)])
        imp = _split_dot_left(ovt, psum)
        imp = jnp.where(forced, jnp.inf, jnp.where(m_idx <= jt, imp, -jnp.inf))

        sub = 8
        imp_ref[g] = imp
        grp = [imp[r:r + sub] for r in range(0, ns, sub)]
        rank = [jnp.zeros((sub, TQ), F32) for _ in grp]
        rsub = lax.broadcasted_iota(I32, (sub, TQ), 0)
        for mp in range(ns):
            vp = jnp.broadcast_to(imp_ref[g, mp:mp + 1, :], (sub, TQ))
            for r in range(len(grp)):
                if r * sub + sub - 1 < mp:
                    before = vp > grp[r]
                elif r * sub > mp:
                    before = vp >= grp[r]
                else:
                    before = (vp > grp[r]) | ((vp == grp[r]) & (rsub + r * sub > mp))
                rank[r] = rank[r] + jnp.where(before, 1.0, 0.0)
        rank = jnp.concatenate(rank, axis=0)
        sel_ref[g] = jnp.where((rank < float(n_top)) & (m_idx <= jt), 1.0, 0.0)

    def slot_bias(p, j):
        parts = []
        for u in range(B_PAIR):
            for sub in range(rq):
                k = jnp.clip(rq * i + sub - j, 0, nd - 1)
                parts.append(bias_ref[k * B_HEADS + p * B_PAIR + u])
        return jnp.concatenate(parts, axis=1)

    def branch_block(j0, ntiles, koff, vrow, mask_fn):
        rows = pl.ds(pl.multiple_of(j0 * TK, TK), ntiles * TK)
        kt = [k_ref[rows, koff + g * LANES:koff + g * LANES + HEAD_DIM] for g in range(B_KV_HEADS)]
        vt = [jnp.concatenate([vt_ref[j0 + t, vrow + g * LANES:vrow + (g + 1) * LANES, :] for t in range(ntiles)],
                              axis=1) for g in range(B_KV_HEADS)]
        masks = [[dup(mask_fn(g, j0 + t)) for t in range(ntiles)] for g in range(B_KV_HEADS)]
        _flash_block_t(qs_ref, range(nslot), lambda p: p // spg, kt, vt, lambda p, t: slot_bias(p, j0 + t),
                       masks, m_ref, acc_ref)

    bpt = TK // B_SEL_LEN

    def sel_mask(g, j):
        chosen = jnp.concatenate([jnp.broadcast_to(sel_ref[g, pl.ds(j * bpt + b, 1), :], (B_SEL_LEN, TQ))
                                  for b in range(bpt)], axis=0) > 0.5
        return chosen & (j * TK + krow <= qpos)

    def win_mask(g, j):
        dist = qpos - (j * TK + krow)
        return (dist >= 0) & (dist < B_WINDOW)

    def sel_body(jb, c):
        branch_block(jb * KB_TILES, KB_TILES, 0, 0, sel_mask)
        return c

    gt = gt_ref[...]

    def gate(h, br):
        return gt[3 * h + br:3 * h + br + 1]

    def head_out(h):
        a = acc_ref[h // B_PAIR][:, lanes(h)]
        return a[:HEAD_DIM] / jnp.maximum(a[HEAD_DIM:HEAD_DIM + 1], 1e-30)

    _flash_init(m_ref, acc_ref)
    lax.fori_loop(0, (rq * (i + 1) + KB_TILES - 1) // KB_TILES, sel_body, 0)
    part = [gate(h, 0) * oc[h // B_PAIR][:, lanes(h)] + gate(h, 1) * head_out(h) for h in range(B_HEADS)]

    wt = B_WINDOW // TK + rq
    _flash_init(m_ref, acc_ref)
    branch_block(jnp.maximum(rq * (i + 1) - wt, 0), wt, 2 * LANES, 2 * LANES, win_mask)
    outs = [part[h] + gate(h, 2) * head_out(h) for h in range(B_HEADS)]
    o_ref[...] = jnp.concatenate(outs, axis=0).T.astype(o_ref.dtype)


def _nsa(bq, bgt, kcmp, vcmpt, bk, bvt, bias, nd, bsz, seq):
    TQ = B_TQ
    nq = seq // TQ
    ncp = seq // B_CMP_STRIDE
    ns = seq // B_SEL_LEN
    nslot, w = B_HEADS // B_PAIR, B_PAIR * TQ
    return pl.pallas_call(
        functools.partial(_nsa_kernel, seq, nd),
        grid=(bsz, nq),
        in_specs=[
            pl.BlockSpec((TQ, 512), lambda b, i: (b * nq + i, 0)),
            pl.BlockSpec((32, TQ), lambda b, i: (0, b * nq + i)),
            pl.BlockSpec((ncp, LANES), lambda b, i: (b, 0)),
            pl.BlockSpec((LANES, ncp), lambda b, i: (b, 0)),
            pl.BlockSpec((seq, 512), lambda b, i: (b, 0)),
            pl.BlockSpec((seq // TK, 4 * LANES, TK), lambda b, i: (b, 0, 0)),
            pl.BlockSpec(bias.shape, lambda b, i: (0, 0, 0), pipeline_mode=pl.Buffered(1)),
        ],
        out_specs=pl.BlockSpec((TQ, 512), lambda b, i: (b * nq + i, 0)),
        out_shape=jax.ShapeDtypeStruct((bsz * seq, 512), BF16),
        scratch_shapes=[pltpu.VMEM((B_KV_HEADS, ns, TQ), F32), pltpu.VMEM((B_KV_HEADS, ns, TQ), F32),
                        pltpu.VMEM((nslot, w, HEAD_DIM), BF16),
                        pltpu.VMEM((nslot, 8, w), F32),
                        pltpu.VMEM((nslot, LANES, w), F32)],
        compiler_params=_params(("parallel", "arbitrary")),
        name="nsa",
    )(bq, bgt, kcmp, vcmpt, bk, bvt, bias)


def _nsa_kernel_rows(seq, nd, q_ref, g_ref, kc_ref, vc_ref, kv_ref, bias_ref, o_ref,
                     imp_ref, sel_ref, qs_ref, m_ref, acc_ref):
    i = pl.program_id(1)
    ncp = seq // B_CMP_STRIDE
    ns = seq // B_SEL_LEN
    n_top = min(B_SEL_TOPK_MAX, ns)
    hg = B_GROUP
    row = lax.broadcasted_iota(I32, (TQ, TK), 0)
    col = lax.broadcasted_iota(I32, (TQ, TK), 1)
    tpos = i * TQ + row
    _split_heads(q_ref, qs_ref, B_HEADS)

    n_idx = lax.broadcasted_iota(I32, (ncp, TQ), 0)
    t_c = i * TQ + lax.broadcasted_iota(I32, (ncp, TQ), 1)
    cmask = n_idx * B_CMP_STRIDE + (B_CMP_LEN - 1) <= t_c
    om = lax.broadcasted_iota(I32, (ns, ncp), 0) * B_SEL_LEN
    on = lax.broadcasted_iota(I32, (ns, ncp), 1) * B_CMP_STRIDE
    ovt = jnp.where((on < om + B_SEL_LEN) & (on + B_CMP_LEN > om), 1.0, 0.0).astype(BF16)
    m_idx = lax.broadcasted_iota(I32, (ns, TQ), 0)
    jt = (i * TQ + lax.broadcasted_iota(I32, (ns, TQ), 1)) // B_SEL_LEN
    forced = (m_idx == 0) | (m_idx == jt) | (m_idx == jt - 1)

    kc = [kc_ref[:, g * HEAD_DIM:(g + 1) * HEAD_DIM] for g in range(B_KV_HEADS)]
    vc = [vc_ref[:, g * HEAD_DIM:(g + 1) * HEAD_DIM] for g in range(B_KV_HEADS)]
    st_all = [_dot_t(kc[h // hg], qs_ref[h]) for h in range(B_HEADS)]
    pts = []
    for h in range(B_HEADS):
        st = jnp.where(cmask, st_all[h], -jnp.inf)
        mx = jnp.max(st, axis=0, keepdims=True)
        mx = jnp.where(mx == -jnp.inf, 0.0, mx)
        e = jnp.exp(st - mx)
        pts.append(e / jnp.maximum(jnp.sum(e, axis=0, keepdims=True), 1e-30))
    oc = [_dot(pts[h].T.astype(BF16), vc[h // hg]) for h in range(B_HEADS)]

    for g in range(B_KV_HEADS):
        psum = functools.reduce(lambda a, b: a + b, pts[g * hg:(g + 1) * hg])
        hi = psum.astype(BF16)
        lo = (psum - hi.astype(F32)).astype(BF16)
        imp = _dot(ovt, hi) + _dot(ovt, lo)
        imp = jnp.where(forced, jnp.inf, jnp.where(m_idx <= jt, imp, -jnp.inf))
        imp_ref[g] = imp

        def rank_step(mp, rank, g=g, imp=imp):
            vp = jnp.broadcast_to(imp_ref[g, pl.ds(mp, 1), :], (ns, TQ))
            before = (vp > imp) | ((vp == imp) & (mp < m_idx))
            return rank + jnp.where(before, 1.0, 0.0)

        rank = lax.fori_loop(0, ns, rank_step, jnp.zeros((ns, TQ), F32), unroll=8)
        selt = jnp.where((rank < float(n_top)) & (m_idx <= jt), 1.0, 0.0)
        selt = jnp.concatenate([selt, jnp.zeros((LANES - ns, TQ), F32)], axis=0)
        sel_ref[g] = selt.T.astype(BF16)

    def branch_block(j0, ntiles, koff, voff, mask_fn):
        r0 = pl.multiple_of(j0 * TK, TK)
        d = [_bias_tile_index(i, j0 + t, TQ, nd) * B_HEADS for t in range(ntiles)]
        rows = pl.ds(r0, ntiles * TK)
        kt = [kv_ref[rows, koff + g * LANES:koff + g * LANES + HEAD_DIM] for g in range(B_KV_HEADS)]
        va = [kv_ref[rows, voff + g * LANES:voff + (g + 1) * LANES] for g in range(B_KV_HEADS)]
        masks = [mask_fn(g, j0) for g in range(B_KV_HEADS)]
        _flash_block(qs_ref, range(B_HEADS), lambda h: h // hg, kt, va, lambda h, t: bias_ref[d[t] + h],
                     masks, m_ref, acc_ref)

    kb = KB_TILES * TK
    e_m = lax.broadcasted_iota(I32, (LANES, kb), 0)
    e_c = lax.broadcasted_iota(I32, (LANES, kb), 1) // B_SEL_LEN

    def sel_masks(g, j0):
        expand = jnp.where(e_m == j0 * (TK // B_SEL_LEN) + e_c, 1.0, 0.0).astype(BF16)
        chosen = _dot(sel_ref[g], expand) > 0.5
        return [chosen[:, t * TK:(t + 1) * TK] & ((j0 + t) * TK + col <= tpos) for t in range(KB_TILES)]

    def sel_body(jb, c):
        branch_block(jb * KB_TILES, KB_TILES, 0, 4 * LANES, sel_masks)
        return c

    gates = g_ref[...]

    def gate(h, br):
        return gates[:, 3 * h + br:3 * h + br + 1]

    _flash_init(m_ref, acc_ref)
    lax.fori_loop(0, (i + KB_TILES) // KB_TILES, sel_body, 0)
    part = [gate(h, 0) * oc[h] + gate(h, 1) * _flash_out(acc_ref[h]) for h in range(B_HEADS)]

    wt = B_WINDOW // TK + 1
    w0 = jnp.maximum(i + 1 - wt, 0)

    def win_masks(g, j0):
        out = []
        for t in range(wt):
            dist = tpos - ((j0 + t) * TK + col)
            out.append((dist >= 0) & (dist < B_WINDOW))
        return out

    _flash_init(m_ref, acc_ref)
    branch_block(w0, wt, 2 * LANES, 6 * LANES, win_masks)
    outs = [part[h] + gate(h, 2) * _flash_out(acc_ref[h]) for h in range(B_HEADS)]
    o_ref[...] = jnp.concatenate(outs, axis=1).astype(o_ref.dtype)


def _nsa_rows(bq, bg, kcmp, vcmp, bsw, bias, nd, bsz, seq):
    nq = seq // TQ
    ncp = seq // B_CMP_STRIDE
    ns = seq // B_SEL_LEN
    return pl.pallas_call(
        functools.partial(_nsa_kernel_rows, seq, nd),
        grid=(bsz, nq),
        in_specs=[
            pl.BlockSpec((TQ, 512), lambda b, i: (b * nq + i, 0)),
            pl.BlockSpec((TQ, LANES), lambda b, i: (b * nq + i, 0)),
            pl.BlockSpec((ncp, LANES), lambda b, i: (b, 0)),
            pl.BlockSpec((ncp, LANES), lambda b, i: (b, 0)),
            pl.BlockSpec((seq, 1024), lambda b, i: (b, 0)),
            pl.BlockSpec(bias.shape, lambda b, i: (0, 0, 0)),
        ],
        out_specs=pl.BlockSpec((TQ, 512), lambda b, i: (b * nq + i, 0)),
        out_shape=jax.ShapeDtypeStruct((bsz * seq, 512), BF16),
        scratch_shapes=[pltpu.VMEM((B_KV_HEADS, ns, TQ), F32), pltpu.VMEM((B_KV_HEADS, TQ, LANES), BF16),
                        pltpu.VMEM((B_HEADS, TQ, HEAD_DIM), BF16),
                        pltpu.VMEM((B_HEADS, TQ, LANES), F32),
                        pltpu.VMEM((B_HEADS, TQ, LANES), F32)],
        compiler_params=_params(("parallel", "arbitrary")),
        name="nsa",
    )(bq, bg, kcmp, vcmp, bsw, bias)


def _dil_kernel(q_ref, kp_ref, kc_ref, vp_ref, vc_ref, bias_ref, o_ref, lse_ref):
    i = pl.program_id(2)

    def rows(ref):
        return ref[...].reshape(-1, ref.shape[-1])

    q = rows(q_ref)
    nt = q.shape[0] // TQ
    k = jnp.concatenate([rows(kp_ref), rows(kc_ref)], axis=0)
    v = jnp.concatenate([rows(vp_ref), rows(vc_ref)], axis=0)
    row = lax.broadcasted_iota(I32, (TQ, 2 * TK), 0)
    col = lax.broadcasted_iota(I32, (TQ, 2 * TK), 1)
    du = row + TK - col
    window = (du >= 0) & (du <= TK)
    units = [(u, hh) for u in range(nt) for hh in range(C_HEADS_PER_GROUP)]

    def head(x, u, n, hh):
        return x[u * TQ:(u + n) * TQ, hh * HEAD_DIM:(hh + 1) * HEAD_DIM]

    scores = [_dot_t(head(q, u, 1, hh), head(k, u, 2, hh)) for u, hh in units]
    es, dens, lses = [], [], []
    for (u, hh), s in zip(units, scores):
        valid = window & ((col >= TK) | (i * nt + u > 0))
        s = jnp.where(valid, s + bias_ref[hh], -jnp.inf)
        m = jnp.max(s, axis=-1, keepdims=True)
        e = jnp.exp(s - m)
        den = jnp.sum(e, axis=-1, keepdims=True)
        es.append(e.astype(BF16))
        dens.append(den)
        lses.append(jnp.broadcast_to(m + jnp.log(den), (TQ, HEAD_DIM)))
    outs = [_dot(e, head(v, u, 2, hh)) / den for (u, hh), e, den in zip(units, es, dens)]

    def assemble(parts):
        tiles = [jnp.concatenate(parts[u * C_HEADS_PER_GROUP:(u + 1) * C_HEADS_PER_GROUP], axis=1)
                 for u in range(nt)]
        return jnp.concatenate(tiles, axis=0)

    o_ref[...] = assemble(outs).reshape(o_ref.shape)
    lse_ref[...] = assemble(lses).reshape(lse_ref.shape)


def _dilated_group(qkv, bias, dil, bsz, seq):
    ln = seq // dil
    nt = min(C_TILES_PER_STEP, ln // TQ)
    gw = C_HEADS_PER_GROUP * HEAD_DIM
    rpc = CLASS_TILE // dil
    tiles_per_seq = seq // CLASS_TILE

    def spec(col, nrows, start):
        if rpc >= nrows:
            def index(b, r, i):
                s = start(i)
                return (b * tiles_per_seq + s // rpc, r, (s % rpc) // nrows, col)
            return pl.BlockSpec((None, None, nrows, gw), index)
        per = nrows // rpc
        assert tiles_per_seq % per == 0
        return pl.BlockSpec((per, None, rpc, gw),
                            lambda b, r, i: (b * (tiles_per_seq // per) + start(i) // nrows, r, 0, col))

    def cur(col):
        return spec(col, nt * TQ, lambda i: i * (nt * TQ))

    def prev(col):
        return spec(col, TQ, lambda i: jnp.maximum(i * nt - 1, 0) * TQ)

    return pl.pallas_call(
        _dil_kernel,
        grid=(bsz, dil, ln // (nt * TQ)),
        in_specs=[cur(0), prev(1), cur(1), prev(2), cur(2),
                  pl.BlockSpec(bias.shape, lambda b, r, i: (0, 0, 0))],
        out_specs=[cur(0)] * 2,
        out_shape=[jax.ShapeDtypeStruct(qkv.shape[:3] + (gw,), F32)] * 2,
        compiler_params=_params(("parallel", "parallel", "arbitrary")),
        name=f"dilated_d{dil}",
    )(qkv, qkv, qkv, qkv, qkv, bias)


def _merge_kernel(x_ref, ya_ref, yb_ref, o0_ref, l0_ref, o1_ref, l1_ref, o2_ref, l2_ref, g_ref,
                  wa_ref, wb_ref, wc_ref, wo_ref, out_ref, nat_ref):
    def natural(k, ref):
        dil, rpc = ref.shape[0], ref.shape[1]
        if dil == 1:
            return ref[0]
        ntile = ref.shape[2] // LANES
        for r in range(dil):
            blk = ref[r]
            for c in range(ntile):
                nat_ref[k * ntile + c, pl.ds(r, rpc, stride=dil), :] = blk[:, c * LANES:(c + 1) * LANES]
        return jnp.concatenate([nat_ref[k * ntile + c] for c in range(ntile)], axis=1)

    o0, l0 = natural(0, o0_ref), natural(1, l0_ref)
    o1, l1 = natural(2, o1_ref), natural(3, l1_ref)
    o2, l2 = natural(4, o2_ref), natural(5, l2_ref)
    mx = jnp.maximum(jnp.maximum(l0, l1), l2)
    e0, e1, e2 = jnp.exp(l0 - mx), jnp.exp(l1 - mx), jnp.exp(l2 - mx)
    yc = (e0 * o0 + e1 * o1 + e2 * o2) / (e0 + e1 + e2)
    ya = _dot(ya_ref[...], wa_ref[...])
    yb = _dot(yb_ref[...], wb_ref[...])
    yc = _dot(yc.astype(BF16), wc_ref[...])
    d = D_MODEL
    z = g_ref[:, 0:d] * ya + g_ref[:, d:2 * d] * yb + g_ref[:, 2 * d:3 * d] * yc
    out_ref[...] = x_ref[...] + _dot(z.astype(BF16), wo_ref[...])


def _merge(x2d, ya, yb, c_outs, mixg, wa, wb, wc, wo, tm):
    m = x2d.shape[0]

    def rows(w):
        return pl.BlockSpec((tm, w), lambda i: (i, 0))

    def full(a):
        return pl.BlockSpec(a.shape, lambda i: (0, 0))

    assert tm == CLASS_TILE
    c_flat = [a for pair in c_outs for a in pair]
    c_specs = [pl.BlockSpec((None,) + a.shape[1:], lambda i: (i, 0, 0, 0)) for a in c_flat]
    gw = C_HEADS_PER_GROUP * HEAD_DIM
    return pl.pallas_call(
        _merge_kernel,
        grid=(m // tm,),
        in_specs=[rows(D_MODEL), rows(512), rows(512)] + c_specs + [rows(3 * D_MODEL)]
                 + [full(wa), full(wb), full(wc), full(wo)],
        out_specs=rows(D_MODEL),
        out_shape=jax.ShapeDtypeStruct((m, D_MODEL), F32),
        scratch_shapes=[pltpu.VMEM((len(c_flat) * gw // LANES, tm, LANES), F32)],
        compiler_params=_params(("parallel",)),
        name="merge",
    )(x2d, ya, yb, *c_flat, mixg, wa, wb, wc, wo)


def _ffn_kernel(x_ref, g_ref, wg_ref, wu_ref, wd_ref, out_ref):
    x = x_ref[...]
    ms = jnp.mean(x * x, axis=-1, keepdims=True)
    h = (x * lax.rsqrt(ms + NORM_EPS) * g_ref[...]).astype(BF16)
    gate = _dot(h, wg_ref[...])
    up = _dot(h, wu_ref[...])
    act = gate / (1.0 + jnp.exp(-gate)) * up
    out_ref[...] = x + _dot(act.astype(BF16), wd_ref[...])


def _ffn(x2d, gain, w_in, w_out, tm):
    m = x2d.shape[0]
    wg = w_in[:, :D_FF].astype(BF16)
    wu = w_in[:, D_FF:].astype(BF16)
    wd = w_out.astype(BF16)

    def full(a):
        return pl.BlockSpec(a.shape, lambda i: (0, 0))

    return pl.pallas_call(
        _ffn_kernel,
        grid=(m // tm,),
        in_specs=[pl.BlockSpec((tm, D_MODEL), lambda i: (i, 0)), pl.BlockSpec((1, D_MODEL), lambda i: (0, 0)),
                  full(wg), full(wu), full(wd)],
        out_specs=pl.BlockSpec((tm, D_MODEL), lambda i: (i, 0)),
        out_shape=jax.ShapeDtypeStruct((m, D_MODEL), F32),
        compiler_params=_params(("parallel",)),
        name="ffn",
    )(x2d, gain.reshape(1, D_MODEL).astype(F32), wg, wu, wd)


def _layer(x2d, bsz, seq, norm1_g, norm2_g, w_in, qk, cmp_pos, cmp_w, w_a, w_b, w_c, w_out, w_ffn_in, w_ffn_out,
           bias_a, bias_b, bias_c):
    pieces_ab, pieces_c, pieces_g = _proj_pieces(w_in, qk)
    aq, ak, avt, iwt, iq, ik, bq, bcmp, bk, bvt, bgt = _proj(x2d, norm1_g, pieces_ab, 512)
    c_qkv = _proj(x2d, norm1_g, pieces_c, CLASS_TILE)
    (mixg,) = _proj(x2d, norm1_g, pieces_g, 512)

    ya = _dsa(iq, iwt, ik, aq, ak, avt, *bias_a, bsz, seq)
    kcmp, vcmpt = _nsa_cmp(bcmp, cmp_pos, cmp_w, qk[3], bsz, seq)
    yb = _nsa(bq, bgt, kcmp, vcmpt, bk, bvt, *bias_b, bsz, seq)
    c_outs = [_dilated_group(c_qkv[g], bias_c[g], dil, bsz, seq) for g, (_, dil) in enumerate(C_GROUPS)]

    x1 = _merge(x2d, ya, yb, c_outs, mixg, w_a.astype(BF16), w_b.astype(BF16), w_c.astype(BF16),
                w_out.astype(BF16), CLASS_TILE)
    return _ffn(x1, norm2_g, w_ffn_in, w_ffn_out, 256)


def kernel(x, norm1_g, norm2_g, w_in, qk_norm_g, nsa_cmp_pos, nsa_cmp_w, w_branch_a, w_branch_b, w_branch_c, w_out, w_ffn_in, w_ffn_out, rel_bias):
    bsz, seq, d = x.shape
    assert d == D_MODEL and seq % (TQ * max(dil for _, dil in C_GROUPS)) == 0 and seq % A_TQ == 0
    assert seq % (KB_TILES * TK) == 0 and seq % B_TQ == 0 and seq >= B_WINDOW + B_TQ
    for win, dil in C_GROUPS:
        assert win == TK * dil
    bias_a = _toeplitz_bias(rel_bias[:, :A_HEADS], seq, A_TQ, keys_on_rows=True, group=A_PAIR, scale=LOG2E)
    bias_b = _toeplitz_bias(rel_bias[:, A_HEADS:A_HEADS + B_HEADS], seq, TK, keys_on_rows=True, scale=LOG2E)
    rel_c = rel_bias[:, A_HEADS + B_HEADS:]
    bias_c = [_dilated_bias(rel_c[:, g * C_HEADS_PER_GROUP:(g + 1) * C_HEADS_PER_GROUP], dil)
              for g, (_, dil) in enumerate(C_GROUPS)]
    x2d = x.reshape(bsz * seq, d)
    for layer in range(norm1_g.shape[0]):
        x2d = _layer(x2d, bsz, seq, norm1_g[layer], norm2_g[layer], w_in[layer], qk_norm_g[layer],
                     nsa_cmp_pos[layer], nsa_cmp_w[layer], w_branch_a[layer], w_branch_b[layer],
                     w_branch_c[layer], w_out[layer], w_ffn_in[layer], w_ffn_out[layer],
                     bias_a, bias_b, bias_c)
    return x2d.reshape(bsz, seq, d)
```

```python
import functools
import math

import numpy as np
import jax
import jax.numpy as jnp
from jax import lax
from jax.experimental import pallas as pl
from jax.experimental.pallas import tpu as pltpu

F32 = jnp.float32
BF16 = jnp.bfloat16
I32 = jnp.int32

D_MODEL = 1024
HEAD_DIM = 64
NORM_EPS = 1e-6
REL_BUCKETS = 32
REL_MAX_DIST = 2048

A_HEADS = 8
A_IDX_HEADS = 4
A_TOPK_MAX = 256
B_HEADS = 8
B_KV_HEADS = 2
B_GROUP = B_HEADS // B_KV_HEADS
B_CMP_LEN = 32
B_CMP_STRIDE = 16
B_SEL_LEN = 64
B_SEL_TOPK_MAX = 16
B_WINDOW = 512
C_GROUPS = ((128, 1), (512, 4), (2048, 16))
C_HEADS_PER_GROUP = 4
C_HEADS = C_HEADS_PER_GROUP * len(C_GROUPS)
D_FF = ((8 * D_MODEL + 3 * 256 - 1) // (3 * 256)) * 256

_O_AQ = 0
_O_AK = _O_AQ + A_HEADS * HEAD_DIM
_O_AV = _O_AK + HEAD_DIM
_O_IQ = _O_AV + HEAD_DIM
_O_IK = _O_IQ + A_IDX_HEADS * HEAD_DIM
_O_IW = _O_IK + HEAD_DIM
_O_BQ = _O_IW + A_IDX_HEADS
_O_BKV = _O_BQ + B_HEADS * HEAD_DIM
_O_BG = _O_BKV + 6 * B_KV_HEADS * HEAD_DIM
_O_CQ = _O_BG + 3 * B_HEADS
_O_CK = _O_CQ + C_HEADS * HEAD_DIM
_O_CV = _O_CK + C_HEADS * HEAD_DIM
_O_MIX = _O_CV + C_HEADS * HEAD_DIM
_O_END = _O_MIX + 3 * D_MODEL

TQ = 128
TK = 128
KB_TILES = 4
A_TQ = 256
A_PAIR = 2
CLASS_TILE = 512
C_TILES_PER_STEP = 4
B_TQ = 256
B_PAIR = 2
LOG2E = 1.4426950408889634
LANES = 128
VMEM_LIMIT = 56 * 1024 * 1024
INT_MIN = -2 ** 31
NEG_INIT = -1e30
SCALE = HEAD_DIM ** -0.5


def _params(sem):
    return pltpu.CompilerParams(dimension_semantics=sem, vmem_limit_bytes=VMEM_LIMIT)


def _dot_t(a, b):
    return lax.dot_general(a, b, (((1,), (1,)), ((), ())), preferred_element_type=F32)


def _dot(a, b):
    return jnp.dot(a, b, preferred_element_type=F32)


def _split_dot_left(a_bf16, b):
    hi = b.astype(BF16)
    lo = (b - hi.astype(F32)).astype(BF16)
    return _dot(a_bf16, hi) + _dot(a_bf16, lo)


def _split_dot(a, b_bf16):
    hi = a.astype(BF16)
    lo = (a - hi.astype(F32)).astype(BF16)
    return _dot(hi, b_bf16) + _dot(lo, b_bf16)


def _bucket_table(n_max):
    n = np.arange(n_max, dtype=np.int64)
    exact = REL_BUCKETS // 2
    nf = np.maximum(n, 1).astype(np.float32)
    large = exact + (np.log(nf / np.float32(exact)) / np.float32(math.log(REL_MAX_DIST / exact))
                     * np.float32(REL_BUCKETS - exact)).astype(np.int32)
    return np.where(n < exact, n, np.minimum(large, REL_BUCKETS - 1)).astype(np.int32)


def _num_bias_tiles(seq, tq):
    bucket = _bucket_table(seq + tq)
    first_sat = int(np.min(np.nonzero(bucket == REL_BUCKETS - 1)[0]))
    assert np.all(bucket[first_sat:] == REL_BUCKETS - 1)
    nd = -(-(first_sat + TK - 1) // TK) + tq // TK
    return min(nd, seq // TK)


def _bias_tile_index(i, j, tq, nd):
    return jnp.clip((tq // TK) * (i + 1) - 1 - j, 0, nd - 1)


def _bias_kernel(nh, group, scale, idx_ref, rel_ref, o_ref):
    idx = idx_ref[0]
    c = idx.shape[1]
    acc = [jnp.zeros(idx.shape, F32) for _ in range(nh)]
    for b in range(REL_BUCKETS):
        hit = idx == b
        for h in range(nh):
            acc[h] = jnp.where(hit, rel_ref[b, h] * scale, acc[h])
    for h in range(nh):
        o_ref[h // group, :, (h % group) * c:(h % group + 1) * c] = acc[h]


def _bias_tiles(rel_cols, idx, group=1, scale=1.0):
    n, r, c = idx.shape
    nh = rel_cols.shape[1]
    return pl.pallas_call(
        functools.partial(_bias_kernel, nh, group, scale),
        grid=(n,),
        in_specs=[pl.BlockSpec((1, r, c), lambda k: (k, 0, 0)),
                  pl.BlockSpec(memory_space=pltpu.SMEM)],
        out_specs=pl.BlockSpec((nh // group, r, group * c), lambda k: (k, 0, 0)),
        out_shape=jax.ShapeDtypeStruct((n * nh // group, r, group * c), F32),
        compiler_params=_params(("parallel",)),
        name="bias_tiles",
    )(jnp.asarray(idx, I32), rel_cols.astype(F32))


def _toeplitz_bias(rel_cols, seq, tq=TQ, keys_on_rows=False, group=1, scale=1.0):
    nd = _num_bias_tiles(seq, tq)
    bucket = _bucket_table(seq + tq)
    d = ((np.arange(nd)[:, None, None] - (tq // TK - 1)) * TK
         + np.arange(tq)[None, :, None] - np.arange(TK)[None, None, :])
    idx = bucket[np.clip(d, 0, None)]
    return _bias_tiles(rel_cols, idx.transpose(0, 2, 1) if keys_on_rows else idx, group, scale), nd


def _dilated_bias(rel_cols, dil):
    bucket = _bucket_table(2 * TK * dil + 1)
    du = np.arange(TQ)[:, None] + TK - np.arange(2 * TK)[None, :]
    return _bias_tiles(rel_cols, bucket[np.clip(du, 0, None) * dil][None])


def _proj_kernel(kinds, *refs):
    n = len(kinds)
    x_ref, g_ref, gs_ref = refs[0], refs[1], refs[2]
    w_refs = refs[3:3 + 2 * n:2]
    aux_refs = refs[4:4 + 2 * n:2]
    n_out = sum(2 if isinstance(k, tuple) and k[0] == "trans" else 1 for k in kinds)
    out_refs = iter(refs[3 + 2 * n:3 + 2 * n + n_out])
    scratch_refs = refs[3 + 2 * n + n_out:]
    x = x_ref[...]
    ms = jnp.mean(x * x, axis=-1, keepdims=True)
    h = (x * lax.rsqrt(ms + NORM_EPS) * g_ref[...]).astype(BF16)
    for kind, w_ref, aux_ref in zip(kinds, w_refs, aux_refs):
        if isinstance(kind, tuple) and kind[0] == "trans":
            _, nv, nextra, act = kind
            vt_ref, ex_ref = next(out_refs), next(out_refs)
            yt = _dot_t(w_ref[...], h)
            rows = lax.broadcasted_iota(I32, (nv * LANES, yt.shape[1]), 0)
            vt = (yt[0:nv * LANES] + jnp.where(rows % LANES >= HEAD_DIM, 1.0, 0.0)).astype(vt_ref.dtype)
            for c in range(vt_ref.shape[0]):
                vt_ref[c] = vt[:, c * TK:(c + 1) * TK]
            ex = yt[nv * LANES:nv * LANES + nextra]
            ex_ref[...] = 1.0 / (1.0 + jnp.exp(-ex)) if act == "sigmoid" else ex
            continue
        o_ref = next(out_refs)
        classes = kind[1] if isinstance(kind, tuple) else 0
        width = w_ref.shape[1]
        cw = 256 if width % 256 == 0 else LANES
        ys = [_dot(h, w_ref[:, c0:c0 + cw]) for c0 in range(0, width, cw)]
        for c0, y in zip(range(0, width, cw), ys):
            if kind == "norm" or classes:
                gsum = _split_dot(y * y, gs_ref[:cw, :cw])
                r = lax.rsqrt(gsum * (1.0 / HEAD_DIM) + NORM_EPS)
                mask = aux_ref[0:1, c0:c0 + cw]
                fac = mask * (r * aux_ref[1:2, c0:c0 + cw]) + (1.0 - mask)
                y = y * fac + aux_ref[2:3, c0:c0 + cw]
            elif kind == "sigmoid":
                y = 1.0 / (1.0 + jnp.exp(-y))
            if classes:
                for c in range(0, cw, LANES):
                    scratch_refs[0][(c0 + c) // LANES] = y[:, c:c + LANES]
            else:
                o_ref[:, c0:c0 + cw] = y.astype(o_ref.dtype)
        for r in range(classes):
            rows = pl.ds(r, x.shape[0] // classes, stride=classes)
            o_ref[r] = jnp.concatenate([scratch_refs[0][c, rows, :] for c in range(width // LANES)],
                                       axis=1).astype(o_ref.dtype)


def _proj(x2d, gain, pieces, tm):
    m, d = x2d.shape
    kinds = tuple(p[0] for p in pieces)
    gs = (np.arange(256)[:, None] // HEAD_DIM == np.arange(256)[None, :] // HEAD_DIM)
    gs = jnp.asarray(gs, BF16)
    in_specs = [pl.BlockSpec((tm, d), lambda i: (i, 0)),
                pl.BlockSpec((1, d), lambda i: (0, 0)),
                pl.BlockSpec((256, 256), lambda i: (0, 0))]
    args = [x2d, gain.reshape(1, d).astype(F32), gs]
    out_specs, out_shapes, scratch = [], [], []
    for kind, w, aux, dt in pieces:
        in_specs += [pl.BlockSpec(w.shape, lambda i: (0, 0)), pl.BlockSpec(aux.shape, lambda i: (0, 0))]
        args += [w, aux]
        if isinstance(kind, tuple) and kind[0] == "trans":
            _, nv, nextra, _ = kind
            out_specs += [pl.BlockSpec((tm // TK, nv * LANES, TK), lambda i: (i, 0, 0)),
                          pl.BlockSpec((nextra, tm), lambda i: (0, i))]
            out_shapes += [jax.ShapeDtypeStruct((m // TK, nv * LANES, TK), dt),
                           jax.ShapeDtypeStruct((nextra, m), F32)]
            continue
        nw = w.shape[1]
        if isinstance(kind, tuple) and kind[0] == "classes":
            dil = kind[1]
            assert tm == CLASS_TILE
            out_specs.append(pl.BlockSpec((None, dil, tm // dil, nw), lambda i: (i, 0, 0, 0)))
            out_shapes.append(jax.ShapeDtypeStruct((m // tm, dil, tm // dil, nw), dt))
            scratch = [pltpu.VMEM((nw // LANES, tm, LANES), F32)]
            continue
        out_specs.append(pl.BlockSpec((tm, nw), lambda i: (i, 0)))
        out_shapes.append(jax.ShapeDtypeStruct((m, nw), dt))
    return pl.pallas_call(
        functools.partial(_proj_kernel, kinds),
        grid=(m // tm,),
        in_specs=in_specs, out_specs=out_specs, out_shape=out_shapes, scratch_shapes=scratch,
        compiler_params=_params(("parallel",)),
        name="proj",
    )(*args)


def _aux(width, mask=None, gain=None, add=None):
    z = jnp.zeros((width,), F32)
    return jnp.stack([z if mask is None else mask, z if gain is None else gain, z if add is None else add])


def _seg(*parts):
    ref = next(p for p in parts if not isinstance(p, int))
    return jnp.concatenate([jnp.zeros(ref.shape[:-1] + (p,), ref.dtype) if isinstance(p, int) else p
                            for p in parts], axis=-1)


def _proj_pieces(w_in, qk):
    w = w_in.astype(BF16)
    hd = HEAD_DIM
    ones, zeros = jnp.ones((hd,), F32), jnp.zeros((hd,), F32)
    cat = jnp.concatenate

    def cols(a, b):
        return w[:, a:b]

    pieces_ab = [
        ("norm", cols(_O_AQ, _O_AK), _aux(512, jnp.ones((512,), F32), jnp.tile(qk[0], A_HEADS) * (SCALE * LOG2E)),
         BF16),
        ("norm", _seg(cols(_O_AK, _O_AV), hd), _aux(LANES, cat([ones, zeros]), cat([qk[1], zeros])), BF16),
        (("trans", 1, 8, "none"), _seg(cols(_O_AV, _O_IQ), hd, cols(_O_IW, _O_BQ), 16 - A_IDX_HEADS).T,
         _aux(LANES), BF16),
        ("plain", cols(_O_IQ, _O_IK), _aux(256), BF16),
        ("plain", _seg(cols(_O_IK, _O_IW), LANES - hd), _aux(LANES), BF16),
        ("norm", cols(_O_BQ, _O_BKV), _aux(512, jnp.ones((512,), F32), jnp.tile(qk[2], B_HEADS) * (SCALE * LOG2E)),
         BF16),
        ("plain", cols(_O_BKV, _O_BKV + 256), _aux(256), F32),
    ]
    o = _O_BKV + 256
    ks0, ks1, vs0, vs1, kw0, kw1, vw0, vw1 = [cols(o + i * hd, o + (i + 1) * hd) for i in range(8)]
    pieces_ab += [
        ("norm", _seg(ks0, hd, ks1, hd, kw0, hd, kw1, hd),
         _aux(512, cat([ones, zeros] * 4), cat([qk[3], zeros] * 4)), BF16),
        (("trans", 4, 32, "sigmoid"),
         _seg(vs0, hd, vs1, hd, vw0, hd, vw1, hd, cols(_O_BG, _O_CQ), 32 - 3 * B_HEADS).T, _aux(LANES), BF16),
    ]
    gw = C_HEADS_PER_GROUP * hd
    pieces_c = []
    for g, (_, dil) in enumerate(C_GROUPS):
        wg = cat([cols(o0 + g * gw, o0 + (g + 1) * gw) for o0 in (_O_CQ, _O_CK, _O_CV)], axis=1)
        mask = cat([jnp.ones((2 * gw,), F32), jnp.zeros((gw,), F32)])
        gain = cat([jnp.tile(qk[4], C_HEADS_PER_GROUP) * SCALE, jnp.tile(qk[5], C_HEADS_PER_GROUP),
                    jnp.zeros((gw,), F32)])
        pieces_c.append((("classes", dil), wg, _aux(3 * gw, mask, gain), BF16))
    pieces_g = [("sigmoid", cols(_O_MIX, _O_END), _aux(3 * D_MODEL), BF16)]
    return pieces_ab, pieces_c, pieces_g


def _flash_init(m_ref, acc_ref):
    m_ref[...] = jnp.full(m_ref.shape, NEG_INIT, F32)
    acc_ref[...] = jnp.zeros(acc_ref.shape, F32)


def _flash_block_t(q_ref, slots, group_of, kt, v_aug_t, bias_fn, masks, m_ref, acc_ref):
    slots = list(slots)
    s_all = [_dot_t(kt[group_of(p)], q_ref[p]) for p in slots]
    m_old = [m_ref[p][0:1] for p in slots]
    ps, alphas = [], []
    for k, p in enumerate(slots):
        s, mk = s_all[k], masks[group_of(p)]
        sc = [jnp.where(mk[c], s[c * TK:(c + 1) * TK] + bias_fn(p, c), -jnp.inf).astype(BF16)
              for c in range(len(mk))]
        m_blk = jnp.max(functools.reduce(jnp.maximum, sc).astype(F32), axis=0, keepdims=True)
        m_new = jnp.maximum(m_old[k], m_blk)
        m16 = m_new.astype(BF16)
        ps.append(jnp.concatenate([jnp.exp2(x - m16) for x in sc], axis=0))
        alphas.append(jnp.exp2(m_old[k] - m_new))
        m_ref[p] = jnp.broadcast_to(m_new, m_ref.shape[1:])
    for k, p in enumerate(slots):
        acc_ref[p] = alphas[k] * acc_ref[p] + _dot(v_aug_t[group_of(p)], ps[k])


def _dsa_kernel(topk, nd, pbits, iq_ref, iwt_ref, ik_ref, aq_ref, ak_ref, avt_ref, bias_ref, o_ref,
                keys_ref, half_ref, iqs_ref, qs_ref, m_ref, acc_ref):
    TQ = A_TQ
    i = pl.program_id(1)
    nk = (i + 1) * (TQ // TK)
    nkb = (nk + KB_TILES - 1) // KB_TILES
    kb = KB_TILES * TK
    krow = lax.broadcasted_iota(I32, (TK, TQ), 0)
    qpos = i * TQ + lax.broadcasted_iota(I32, (TK, TQ), 1)
    i16 = jnp.int16
    last = keys_ref.shape[0] * TK - 1

    iwt = iwt_ref[...]
    for h in range(A_IDX_HEADS):
        iqs_ref[h * TQ:(h + 1) * TQ] = iq_ref[:, h * HEAD_DIM:(h + 1) * HEAD_DIM]

    def score_block(jb, c):
        r0 = pl.multiple_of(jb * kb, kb)
        d = _dot_t(ik_ref[pl.ds(r0, kb), 0:HEAD_DIM], iqs_ref[...])
        acc = jnp.zeros((kb, TQ), F32)
        for h in range(A_IDX_HEADS):
            acc = acc + jnp.maximum(d[:, h * TQ:(h + 1) * TQ], 0.0) * iwt[h:h + 1]
        bits = lax.bitcast_convert_type(acc, I32)
        key = jnp.where(bits < 0, bits ^ 0x7FFFFFFF, bits + (last + 1))
        for t in range(KB_TILES):
            j = jb * KB_TILES + t
            kidx = j * TK + krow
            kj = jnp.where(acc[t * TK:(t + 1) * TK] == 0.0, last - kidx, key[t * TK:(t + 1) * TK])
            kj = jnp.where(kidx <= qpos, kj, INT_MIN)
            keys_ref[j] = kj
            half_ref[j] = jnp.right_shift(kj, 16).astype(i16)
        return c

    lax.fori_loop(0, nkb, score_block, 0)

    kf = float(topk)

    nb_max = keys_ref.shape[0] // KB_TILES
    sub = 16

    def search_half(nbits, u0):
        def run(nblk, u_init):
            def step(b, u):
                cand = u | jnp.left_shift(jnp.int32(1), nbits - 1 - b)
                cb = jnp.broadcast_to((cand - 32768).astype(i16), (TK, TQ))
                acc = jnp.zeros((sub, TQ), i16)
                for j in range(nblk * KB_TILES):
                    hit = jnp.where(half_ref[j] >= cb, jnp.ones((), i16), jnp.zeros((), i16))
                    acc = acc + functools.reduce(lambda a, b2: a + b2,
                                                 [hit[r:r + sub] for r in range(0, TK, sub)])
                cnt = jnp.sum(acc.astype(F32), axis=0, keepdims=True)
                return jnp.where(cnt >= kf, cand, u)

            return lax.fori_loop(0, nbits, step, u_init)

        return lax.switch(nkb - 1, [functools.partial(run, n) for n in range(1, nb_max + 1)], u0)

    def for_tiles(fn):
        def body(jb, c):
            for t in range(KB_TILES):
                fn(jb * KB_TILES + t)
            return c
        lax.fori_loop(0, nkb, body, 0)

    zero = jnp.zeros((1, TQ), I32)
    t_hi = search_half(16, zero) - 32768

    def low_tile(j):
        k = keys_ref[j]
        hi = jnp.right_shift(k, 16)
        lo = (k & 0xFFFF) - 32768
        half_ref[j] = jnp.where(hi > t_hi, 32767, jnp.where(hi < t_hi, -32768, lo)).astype(i16)

    for_tiles(low_tile)
    thr = t_hi * 65536 + search_half(16, zero)

    def count_ge(jb, acc):
        for t in range(KB_TILES):
            acc = acc + jnp.where(keys_ref[jb * KB_TILES + t] >= thr, 1.0, 0.0)
        return acc

    n_ge = jnp.sum(lax.fori_loop(0, nkb, count_ge, jnp.zeros((TK, TQ), F32)), axis=0, keepdims=True)

    @pl.when(jnp.max(jnp.where(thr > INT_MIN, n_ge, 0.0)) > kf)
    def _():
        def tie_tile(j):
            k = keys_ref[j]
            rev = last - (j * TK + krow)
            half_ref[j] = jnp.where(k > thr, 32767, jnp.where(k == thr, rev, -32768)).astype(i16)

        for_tiles(tie_tile)
        keep = search_half(pbits, zero + 32768) - 32768

        def demote(j):
            k = keys_ref[j]
            rev = last - (j * TK + krow)
            keys_ref[j] = jnp.where((k == thr) & (rev < keep) & (thr > INT_MIN), k - 1, k)

        for_tiles(demote)

    sel_thr = jnp.maximum(thr, INT_MIN + 1)

    nslot = A_HEADS // A_PAIR
    for h in range(A_HEADS):
        qs_ref[h // A_PAIR, (h % A_PAIR) * TQ:(h % A_PAIR + 1) * TQ] = aq_ref[:, h * HEAD_DIM:(h + 1) * HEAD_DIM]
    _flash_init(m_ref, acc_ref)

    def att_block(jb, c):
        r0 = pl.multiple_of(jb * kb, kb)
        kt = ak_ref[pl.ds(r0, kb), 0:HEAD_DIM]
        j0 = jb * KB_TILES
        masks = []
        for t in range(KB_TILES):
            mk = keys_ref[j0 + t] >= sel_thr
            masks.append(jnp.concatenate([mk] * A_PAIR, axis=1))
        d = [_bias_tile_index(i, j0 + t, TQ, nd) * nslot for t in range(KB_TILES)]
        vat = jnp.concatenate([avt_ref[j0 + t] for t in range(KB_TILES)], axis=1)
        _flash_block_t(qs_ref, range(nslot), lambda p: 0, [kt], [vat], lambda p, t: bias_ref[d[t] + p], [masks],
                       m_ref, acc_ref)
        return c

    lax.fori_loop(0, nkb, att_block, 0)
    outs = []
    for h in range(A_HEADS):
        acc = acc_ref[h // A_PAIR][:, (h % A_PAIR) * TQ:(h % A_PAIR + 1) * TQ]
        outs.append(acc[:HEAD_DIM] / jnp.maximum(acc[HEAD_DIM:HEAD_DIM + 1], 1e-30))
    o_ref[...] = jnp.concatenate(outs, axis=0).T.astype(o_ref.dtype)


def _dsa(iq, iwt, ik, aq, ak, avt, bias, nd, bsz, seq):
    TQ = A_TQ
    nq = seq // TQ
    kb = KB_TILES * TK
    topk = min(A_TOPK_MAX, seq // 4)
    pbits = max(1, (seq - 1).bit_length())
    return pl.pallas_call(
        functools.partial(_dsa_kernel, topk, nd, pbits),
        grid=(bsz, nq),
        in_specs=[
            pl.BlockSpec((TQ, 256), lambda b, i: (b * nq + i, 0)),
            pl.BlockSpec((8, TQ), lambda b, i: (0, b * nq + i)),
            pl.BlockSpec((seq, LANES), lambda b, i: (b, 0)),
            pl.BlockSpec((TQ, 512), lambda b, i: (b * nq + i, 0)),
            pl.BlockSpec((seq, LANES), lambda b, i: (b, 0)),
            pl.BlockSpec((seq // TK, LANES, TK), lambda b, i: (b, 0, 0)),
            pl.BlockSpec(bias.shape, lambda b, i: (0, 0, 0), pipeline_mode=pl.Buffered(1)),
        ],
        out_specs=pl.BlockSpec((TQ, 512), lambda b, i: (b * nq + i, 0)),
        out_shape=jax.ShapeDtypeStruct((bsz * seq, 512), BF16),
        scratch_shapes=[pltpu.VMEM((seq // TK, TK, TQ), I32),
                        pltpu.VMEM((seq // TK, TK, TQ), jnp.int16),
                        pltpu.VMEM((A_IDX_HEADS * TQ, HEAD_DIM), BF16),
                        pltpu.VMEM((A_HEADS // A_PAIR, A_PAIR * TQ, HEAD_DIM), BF16),
                        pltpu.VMEM((A_HEADS // A_PAIR, 8, A_PAIR * TQ), F32),
                        pltpu.VMEM((A_HEADS // A_PAIR, LANES, A_PAIR * TQ), F32)],
        compiler_params=_params(("parallel", "arbitrary")),
        name="dsa",
    )(iq, iwt, ik, aq, ak, avt, bias)


def _nsa_cmp_kernel(x_ref, pos_ref, wlo_ref, whi_ref, gain_ref, k_ref, v_ref):
    x = x_ref[...]
    lo = _dot((x + pos_ref[0:1, :]).astype(BF16), wlo_ref[...])
    hi = _dot((x + pos_ref[1:2, :]).astype(BF16), whi_ref[...])
    nrow = x.shape[0]
    pre = lo + pltpu.roll(hi, nrow - 1, 0)
    ks = []
    for g in range(B_KV_HEADS):
        kg = pre[:, g * HEAD_DIM:(g + 1) * HEAD_DIM]
        ms = jnp.mean(kg * kg, axis=-1, keepdims=True)
        ks.append(kg * lax.rsqrt(ms + NORM_EPS) * gain_ref[...])
    k_ref[...] = jnp.concatenate(ks, axis=1).astype(k_ref.dtype)
    v_ref[...] = pre[:, LANES:2 * LANES].T.astype(v_ref.dtype)


def _nsa_cmp(bcmp, cmp_pos, cmp_w, k_gain, bsz, seq):
    nch = seq // B_CMP_STRIDE
    half = B_CMP_LEN // 2
    width = half * 256
    x = bcmp.reshape(bsz * nch, width)

    def wmat(l0):
        w = jnp.zeros((half, 4, HEAD_DIM, 4, HEAD_DIM), F32)
        for j in range(4):
            w = w.at[:, j, :, j, :].set(cmp_w[j // 2, l0:l0 + half])
        return w.reshape(width, 256).astype(BF16)

    def prow(l0):
        p = jnp.stack([cmp_pos[0, l0:l0 + half], cmp_pos[0, l0:l0 + half],
                       cmp_pos[1, l0:l0 + half], cmp_pos[1, l0:l0 + half]], axis=1)
        return p.reshape(width)

    pos = jnp.stack([prow(0), prow(half)]).astype(F32)
    return pl.pallas_call(
        _nsa_cmp_kernel,
        grid=(bsz,),
        in_specs=[
            pl.BlockSpec((nch, width), lambda b: (b, 0)),
            pl.BlockSpec((2, width), lambda b: (0, 0)),
            pl.BlockSpec((width, 256), lambda b: (0, 0)),
            pl.BlockSpec((width, 256), lambda b: (0, 0)),
            pl.BlockSpec((1, HEAD_DIM), lambda b: (0, 0)),
        ],
        out_specs=[pl.BlockSpec((nch, LANES), lambda b: (b, 0)), pl.BlockSpec((LANES, nch), lambda b: (b, 0))],
        out_shape=[jax.ShapeDtypeStruct((bsz * nch, LANES), BF16), jax.ShapeDtypeStruct((bsz * LANES, nch), BF16)],
        compiler_params=_params(("parallel",)),
        name="nsa_cmp",
    )(x, pos, wmat(0), wmat(half), k_gain.reshape(1, HEAD_DIM).astype(F32))


def _nsa_kernel(seq, nd, q_ref, gt_ref, kc_ref, vct_ref, k_ref, vt_ref, bias_ref, o_ref,
                imp_ref, sel_ref, qs_ref, m_ref, acc_ref):
    TQ = B_TQ
    rq = TQ // TK
    i = pl.program_id(1)
    ncp = seq // B_CMP_STRIDE
    ns = seq // B_SEL_LEN
    n_top = min(B_SEL_TOPK_MAX, ns)
    nslot = B_HEADS // B_PAIR
    spg = B_GROUP // B_PAIR
    krow = lax.broadcasted_iota(I32, (TK, TQ), 0)
    qpos = i * TQ + lax.broadcasted_iota(I32, (TK, TQ), 1)

    def lanes(h):
        return slice((h % B_PAIR) * TQ, (h % B_PAIR + 1) * TQ)

    def dup(x):
        return jnp.concatenate([x] * B_PAIR, axis=1)

    for h in range(B_HEADS):
        qs_ref[h // B_PAIR, lanes(h)] = q_ref[:, h * HEAD_DIM:(h + 1) * HEAD_DIM]

    n_idx = lax.broadcasted_iota(I32, (ncp, TQ), 0)
    t_c = i * TQ + lax.broadcasted_iota(I32, (ncp, TQ), 1)
    cmask = dup(n_idx * B_CMP_STRIDE + (B_CMP_LEN - 1) <= t_c)
    om = lax.broadcasted_iota(I32, (ns, ncp), 0) * B_SEL_LEN
    on = lax.broadcasted_iota(I32, (ns, ncp), 1) * B_CMP_STRIDE
    ovt = jnp.where((on < om + B_SEL_LEN) & (on + B_CMP_LEN > om), 1.0, 0.0).astype(BF16)
    m_idx = lax.broadcasted_iota(I32, (ns, TQ), 0)
    jt = (i * TQ + lax.broadcasted_iota(I32, (ns, TQ), 1)) // B_SEL_LEN
    forced = (m_idx == 0) | (m_idx == jt) | (m_idx == jt - 1)

    kc = [kc_ref[:, g * HEAD_DIM:(g + 1) * HEAD_DIM] for g in range(B_KV_HEADS)]
    vct = [vct_ref[g * HEAD_DIM:(g + 1) * HEAD_DIM, :] for g in range(B_KV_HEADS)]
    st_all = [_dot_t(kc[p // spg], qs_ref[p]) for p in range(nslot)]
    pts = []
    for p in range(nslot):
        st = jnp.where(cmask, st_all[p], -jnp.inf)
        mx = jnp.max(st, axis=0, keepdims=True)
        mx = jnp.where(mx == -jnp.inf, 0.0, mx)
        e = jnp.exp2(st - mx)
        pts.append(e / jnp.maximum(jnp.sum(e, axis=0, keepdims=True), 1e-30))
    oc = [_dot(vct[p // spg], pts[p].astype(BF16)) for p in range(nslot)]

    for g in range(B_KV_HEADS):
        psum = functools.reduce(lambda a, b: a + b, [pts[h // B_PAIR][:, lanes(h)]
                                                     for h in range(g * B_GROUP, (g + 1) * B_GROUP)])
        imp = _split_dot_left(ovt, psum)
        imp = jnp.where(forced, jnp.inf, jnp.where(m_idx <= jt, imp, -jnp.inf))

        sub = 8
        imp_ref[g] = imp
        grp = [imp[r:r + sub] for r in range(0, ns, sub)]
        rank = [jnp.zeros((sub, TQ), F32) for _ in grp]
        rsub = lax.broadcasted_iota(I32, (sub, TQ), 0)
        for mp in range(ns):
            vp = jnp.broadcast_to(imp_ref[g, mp:mp + 1, :], (sub, TQ))
            for r in range(len(grp)):
                if r * sub + sub - 1 < mp:
                    before = vp > grp[r]
                elif r * sub > mp:
                    before = vp >= grp[r]
                else:
                    before = (vp > grp[r]) | ((vp == grp[r]) & (rsub + r * sub > mp))
                rank[r] = rank[r] + jnp.where(before, 1.0, 0.0)
        rank = jnp.concatenate(rank, axis=0)
        sel_ref[g] = jnp.where((rank < float(n_top)) & (m_idx <= jt), 1.0, 0.0)

    def slot_bias(p, j):
        parts = []
        for u in range(B_PAIR):
            for sub in range(rq):
                k = jnp.clip(rq * i + sub - j, 0, nd - 1)
                parts.append(bias_ref[k * B_HEADS + p * B_PAIR + u])
        return jnp.concatenate(parts, axis=1)

    def branch_block(j0, ntiles, koff, vrow, mask_fn):
        rows = pl.ds(pl.multiple_of(j0 * TK, TK), ntiles * TK)
        kt = [k_ref[rows, koff + g * LANES:koff + g * LANES + HEAD_DIM] for g in range(B_KV_HEADS)]
        vt = [jnp.concatenate([vt_ref[j0 + t, vrow + g * LANES:vrow + (g + 1) * LANES, :] for t in range(ntiles)],
                              axis=1) for g in range(B_KV_HEADS)]
        masks = [[dup(mask_fn(g, j0 + t)) for t in range(ntiles)] for g in range(B_KV_HEADS)]
        _flash_block_t(qs_ref, range(nslot), lambda p: p // spg, kt, vt, lambda p, t: slot_bias(p, j0 + t),
                       masks, m_ref, acc_ref)

    bpt = TK // B_SEL_LEN

    def sel_mask(g, j):
        chosen = jnp.concatenate([jnp.broadcast_to(sel_ref[g, pl.ds(j * bpt + b, 1), :], (B_SEL_LEN, TQ))
                                  for b in range(bpt)], axis=0) > 0.5
        return chosen & (j * TK + krow <= qpos)

    def win_mask(g, j):
        dist = qpos - (j * TK + krow)
        return (dist >= 0) & (dist < B_WINDOW)

    def sel_body(jb, c):
        branch_block(jb * KB_TILES, KB_TILES, 0, 0, sel_mask)
        return c

    gt = gt_ref[...]

    def gate(h, br):
        return gt[3 * h + br:3 * h + br + 1]

    def head_out(h):
        a = acc_ref[h // B_PAIR][:, lanes(h)]
        return a[:HEAD_DIM] / jnp.maximum(a[HEAD_DIM:HEAD_DIM + 1], 1e-30)

    _flash_init(m_ref, acc_ref)
    lax.fori_loop(0, (rq * (i + 1) + KB_TILES - 1) // KB_TILES, sel_body, 0)
    part = [gate(h, 0) * oc[h // B_PAIR][:, lanes(h)] + gate(h, 1) * head_out(h) for h in range(B_HEADS)]

    wt = B_WINDOW // TK + rq
    _flash_init(m_ref, acc_ref)
    branch_block(jnp.maximum(rq * (i + 1) - wt, 0), wt, 2 * LANES, 2 * LANES, win_mask)
    outs = [part[h] + gate(h, 2) * head_out(h) for h in range(B_HEADS)]
    o_ref[...] = jnp.concatenate(outs, axis=0).T.astype(o_ref.dtype)


def _nsa(bq, bgt, kcmp, vcmpt, bk, bvt, bias, nd, bsz, seq):
    TQ = B_TQ
    nq = seq // TQ
    ncp = seq // B_CMP_STRIDE
    ns = seq // B_SEL_LEN
    nslot, w = B_HEADS // B_PAIR, B_PAIR * TQ
    return pl.pallas_call(
        functools.partial(_nsa_kernel, seq, nd),
        grid=(bsz, nq),
        in_specs=[
            pl.BlockSpec((TQ, 512), lambda b, i: (b * nq + i, 0)),
            pl.BlockSpec((32, TQ), lambda b, i: (0, b * nq + i)),
            pl.BlockSpec((ncp, LANES), lambda b, i: (b, 0)),
            pl.BlockSpec((LANES, ncp), lambda b, i: (b, 0)),
            pl.BlockSpec((seq, 512), lambda b, i: (b, 0)),
            pl.BlockSpec((seq // TK, 4 * LANES, TK), lambda b, i: (b, 0, 0)),
            pl.BlockSpec(bias.shape, lambda b, i: (0, 0, 0), pipeline_mode=pl.Buffered(1)),
        ],
        out_specs=pl.BlockSpec((TQ, 512), lambda b, i: (b * nq + i, 0)),
        out_shape=jax.ShapeDtypeStruct((bsz * seq, 512), BF16),
        scratch_shapes=[pltpu.VMEM((B_KV_HEADS, ns, TQ), F32), pltpu.VMEM((B_KV_HEADS, ns, TQ), F32),
                        pltpu.VMEM((nslot, w, HEAD_DIM), BF16),
                        pltpu.VMEM((nslot, 8, w), F32),
                        pltpu.VMEM((nslot, LANES, w), F32)],
        compiler_params=_params(("parallel", "arbitrary")),
        name="nsa",
    )(bq, bgt, kcmp, vcmpt, bk, bvt, bias)


def _dil_kernel(q_ref, kp_ref, kc_ref, vp_ref, vc_ref, bias_ref, o_ref, lse_ref):
    i = pl.program_id(2)

    def rows(ref):
        return ref[...].reshape(-1, ref.shape[-1])

    q = rows(q_ref)
    nt = q.shape[0] // TQ
    k = jnp.concatenate([rows(kp_ref), rows(kc_ref)], axis=0)
    v = jnp.concatenate([rows(vp_ref), rows(vc_ref)], axis=0)
    row = lax.broadcasted_iota(I32, (TQ, 2 * TK), 0)
    col = lax.broadcasted_iota(I32, (TQ, 2 * TK), 1)
    du = row + TK - col
    window = (du >= 0) & (du <= TK)
    units = [(u, hh) for u in range(nt) for hh in range(C_HEADS_PER_GROUP)]

    def head(x, u, n, hh):
        return x[u * TQ:(u + n) * TQ, hh * HEAD_DIM:(hh + 1) * HEAD_DIM]

    scores = [_dot_t(head(q, u, 1, hh), head(k, u, 2, hh)) for u, hh in units]
    es, dens, lses = [], [], []
    for (u, hh), s in zip(units, scores):
        valid = window & ((col >= TK) | (i * nt + u > 0))
        s = jnp.where(valid, s + bias_ref[hh], -jnp.inf)
        m = jnp.max(s, axis=-1, keepdims=True)
        e = jnp.exp(s - m)
        den = jnp.sum(e, axis=-1, keepdims=True)
        es.append(e.astype(BF16))
        dens.append(den)
        lses.append(jnp.broadcast_to(m + jnp.log(den), (TQ, HEAD_DIM)))
    outs = [_dot(e, head(v, u, 2, hh)) / den for (u, hh), e, den in zip(units, es, dens)]

    def assemble(parts):
        tiles = [jnp.concatenate(parts[u * C_HEADS_PER_GROUP:(u + 1) * C_HEADS_PER_GROUP], axis=1)
                 for u in range(nt)]
        return jnp.concatenate(tiles, axis=0)

    o_ref[...] = assemble(outs).reshape(o_ref.shape)
    lse_ref[...] = assemble(lses).reshape(lse_ref.shape)


def _dilated_group(qkv, bias, dil, bsz, seq):
    ln = seq // dil
    nt = min(C_TILES_PER_STEP, ln // TQ)
    gw = C_HEADS_PER_GROUP * HEAD_DIM
    rpc = CLASS_TILE // dil
    tiles_per_seq = seq // CLASS_TILE

    def spec(col, nrows, start):
        if rpc >= nrows:
            def index(b, r, i):
                s = start(i)
                return (b * tiles_per_seq + s // rpc, r, (s % rpc) // nrows, col)
            return pl.BlockSpec((None, None, nrows, gw), index)
        per = nrows // rpc
        assert tiles_per_seq % per == 0
        return pl.BlockSpec((per, None, rpc, gw),
                            lambda b, r, i: (b * (tiles_per_seq // per) + start(i) // nrows, r, 0, col))

    def cur(col):
        return spec(col, nt * TQ, lambda i: i * (nt * TQ))

    def prev(col):
        return spec(col, TQ, lambda i: jnp.maximum(i * nt - 1, 0) * TQ)

    return pl.pallas_call(
        _dil_kernel,
        grid=(bsz, dil, ln // (nt * TQ)),
        in_specs=[cur(0), prev(1), cur(1), prev(2), cur(2),
                  pl.BlockSpec(bias.shape, lambda b, r, i: (0, 0, 0))],
        out_specs=[cur(0)] * 2,
        out_shape=[jax.ShapeDtypeStruct(qkv.shape[:3] + (gw,), F32)] * 2,
        compiler_params=_params(("parallel", "parallel", "arbitrary")),
        name=f"dilated_d{dil}",
    )(qkv, qkv, qkv, qkv, qkv, bias)


def _merge_kernel(x_ref, ya_ref, yb_ref, o0_ref, l0_ref, o1_ref, l1_ref, o2_ref, l2_ref, g_ref,
                  wa_ref, wb_ref, wc_ref, wo_ref, out_ref, nat_ref):
    def natural(k, ref):
        dil, rpc = ref.shape[0], ref.shape[1]
        if dil == 1:
            return ref[0]
        ntile = ref.shape[2] // LANES
        for r in range(dil):
            blk = ref[r]
            for c in range(ntile):
                nat_ref[k * ntile + c, pl.ds(r, rpc, stride=dil), :] = blk[:, c * LANES:(c + 1) * LANES]
        return jnp.concatenate([nat_ref[k * ntile + c] for c in range(ntile)], axis=1)

    o0, l0 = natural(0, o0_ref), natural(1, l0_ref)
    o1, l1 = natural(2, o1_ref), natural(3, l1_ref)
    o2, l2 = natural(4, o2_ref), natural(5, l2_ref)
    mx = jnp.maximum(jnp.maximum(l0, l1), l2)
    e0, e1, e2 = jnp.exp(l0 - mx), jnp.exp(l1 - mx), jnp.exp(l2 - mx)
    yc = (e0 * o0 + e1 * o1 + e2 * o2) / (e0 + e1 + e2)
    ya = _dot(ya_ref[...], wa_ref[...])
    yb = _dot(yb_ref[...], wb_ref[...])
    yc = _dot(yc.astype(BF16), wc_ref[...])
    d = D_MODEL
    z = g_ref[:, 0:d] * ya + g_ref[:, d:2 * d] * yb + g_ref[:, 2 * d:3 * d] * yc
    out_ref[...] = x_ref[...] + _dot(z.astype(BF16), wo_ref[...])


def _merge(x2d, ya, yb, c_outs, mixg, wa, wb, wc, wo, tm):
    m = x2d.shape[0]

    def rows(w):
        return pl.BlockSpec((tm, w), lambda i: (i, 0))

    def full(a):
        return pl.BlockSpec(a.shape, lambda i: (0, 0))

    assert tm == CLASS_TILE
    c_flat = [a for pair in c_outs for a in pair]
    c_specs = [pl.BlockSpec((None,) + a.shape[1:], lambda i: (i, 0, 0, 0)) for a in c_flat]
    gw = C_HEADS_PER_GROUP * HEAD_DIM
    return pl.pallas_call(
        _merge_kernel,
        grid=(m // tm,),
        in_specs=[rows(D_MODEL), rows(512), rows(512)] + c_specs + [rows(3 * D_MODEL)]
                 + [full(wa), full(wb), full(wc), full(wo)],
        out_specs=rows(D_MODEL),
        out_shape=jax.ShapeDtypeStruct((m, D_MODEL), F32),
        scratch_shapes=[pltpu.VMEM((len(c_flat) * gw // LANES, tm, LANES), F32)],
        compiler_params=_params(("parallel",)),
        name="merge",
    )(x2d, ya, yb, *c_flat, mixg, wa, wb, wc, wo)


def _ffn_kernel(x_ref, g_ref, wg_ref, wu_ref, wd_ref, out_ref):
    x = x_ref[...]
    ms = jnp.mean(x * x, axis=-1, keepdims=True)
    h = (x * lax.rsqrt(ms + NORM_EPS) * g_ref[...]).astype(BF16)
    gate = _dot(h, wg_ref[...])
    up = _dot(h, wu_ref[...])
    act = gate / (1.0 + jnp.exp(-gate)) * up
    out_ref[...] = x + _dot(act.astype(BF16), wd_ref[...])


def _ffn(x2d, gain, w_in, w_out, tm):
    m = x2d.shape[0]
    wg = w_in[:, :D_FF].astype(BF16)
    wu = w_in[:, D_FF:].astype(BF16)
    wd = w_out.astype(BF16)

    def full(a):
        return pl.BlockSpec(a.shape, lambda i: (0, 0))

    return pl.pallas_call(
        _ffn_kernel,
        grid=(m // tm,),
        in_specs=[pl.BlockSpec((tm, D_MODEL), lambda i: (i, 0)), pl.BlockSpec((1, D_MODEL), lambda i: (0, 0)),
                  full(wg), full(wu), full(wd)],
        out_specs=pl.BlockSpec((tm, D_MODEL), lambda i: (i, 0)),
        out_shape=jax.ShapeDtypeStruct((m, D_MODEL), F32),
        compiler_params=_params(("parallel",)),
        name="ffn",
    )(x2d, gain.reshape(1, D_MODEL).astype(F32), wg, wu, wd)


def _layer(x2d, bsz, seq, norm1_g, norm2_g, w_in, qk, cmp_pos, cmp_w, w_a, w_b, w_c, w_out, w_ffn_in, w_ffn_out,
           bias_a, bias_b, bias_c):
    pieces_ab, pieces_c, pieces_g = _proj_pieces(w_in, qk)
    aq, ak, avt, iwt, iq, ik, bq, bcmp, bk, bvt, bgt = _proj(x2d, norm1_g, pieces_ab, 512)
    c_qkv = _proj(x2d, norm1_g, pieces_c, CLASS_TILE)
    (mixg,) = _proj(x2d, norm1_g, pieces_g, 512)

    ya = _dsa(iq, iwt, ik, aq, ak, avt, *bias_a, bsz, seq)
    kcmp, vcmpt = _nsa_cmp(bcmp, cmp_pos, cmp_w, qk[3], bsz, seq)
    yb = _nsa(bq, bgt, kcmp, vcmpt, bk, bvt, *bias_b, bsz, seq)
    c_outs = [_dilated_group(c_qkv[g], bias_c[g], dil, bsz, seq) for g, (_, dil) in enumerate(C_GROUPS)]

    x1 = _merge(x2d, ya, yb, c_outs, mixg, w_a.astype(BF16), w_b.astype(BF16), w_c.astype(BF16),
                w_out.astype(BF16), CLASS_TILE)
    return _ffn(x1, norm2_g, w_ffn_in, w_ffn_out, 256)


def kernel(x, norm1_g, norm2_g, w_in, qk_norm_g, nsa_cmp_pos, nsa_cmp_w, w_branch_a, w_branch_b, w_branch_c, w_out, w_ffn_in, w_ffn_out, rel_bias):
    bsz, seq, d = x.shape
    assert d == D_MODEL and seq % (TQ * max(dil for _, dil in C_GROUPS)) == 0 and seq % A_TQ == 0
    assert seq % (KB_TILES * TK) == 0 and seq % B_TQ == 0 and seq >= B_WINDOW + B_TQ
    for win, dil in C_GROUPS:
        assert win == TK * dil
    bias_a = _toeplitz_bias(rel_bias[:, :A_HEADS], seq, A_TQ, keys_on_rows=True, group=A_PAIR, scale=LOG2E)
    bias_b = _toeplitz_bias(rel_bias[:, A_HEADS:A_HEADS + B_HEADS], seq, TK, keys_on_rows=True, scale=LOG2E)
    rel_c = rel_bias[:, A_HEADS + B_HEADS:]
    bias_c = [_dilated_bias(rel_c[:, g * C_HEADS_PER_GROUP:(g + 1) * C_HEADS_PER_GROUP], dil)
              for g, (_, dil) in enumerate(C_GROUPS)]
    x2d = x.reshape(bsz * seq, d)
    for layer in range(norm1_g.shape[0]):
        x2d = _layer(x2d, bsz, seq, norm1_g[layer], norm2_g[layer], w_in[layer], qk_norm_g[layer],
                     nsa_cmp_pos[layer], nsa_cmp_w[layer], w_branch_a[layer], w_branch_b[layer],
                     w_branch_c[layer], w_out[layer], w_ffn_in[layer], w_ffn_out[layer],
                     bias_a, bias_b, bias_c)
    return x2d.reshape(bsz, seq, d)
```

```python
import functools
import math

import numpy as np
import jax
import jax.numpy as jnp
from jax import lax
from jax.experimental import pallas as pl
from jax.experimental.pallas import tpu as pltpu

F32 = jnp.float32
BF16 = jnp.bfloat16
I32 = jnp.int32

D_MODEL = 1024
HEAD_DIM = 64
NORM_EPS = 1e-6
REL_BUCKETS = 32
REL_MAX_DIST = 2048

A_HEADS = 8
A_IDX_HEADS = 4
A_TOPK_MAX = 256
B_HEADS = 8
B_KV_HEADS = 2
B_GROUP = B_HEADS // B_KV_HEADS
B_CMP_LEN = 32
B_CMP_STRIDE = 16
B_SEL_LEN = 64
B_SEL_TOPK_MAX = 16
B_WINDOW = 512
C_GROUPS = ((128, 1), (512, 4), (2048, 16))
C_HEADS_PER_GROUP = 4
C_HEADS = C_HEADS_PER_GROUP * len(C_GROUPS)
D_FF = ((8 * D_MODEL + 3 * 256 - 1) // (3 * 256)) * 256

_O_AQ = 0
_O_AK = _O_AQ + A_HEADS * HEAD_DIM
_O_AV = _O_AK + HEAD_DIM
_O_IQ = _O_AV + HEAD_DIM
_O_IK = _O_IQ + A_IDX_HEADS * HEAD_DIM
_O_IW = _O_IK + HEAD_DIM
_O_BQ = _O_IW + A_IDX_HEADS
_O_BKV = _O_BQ + B_HEADS * HEAD_DIM
_O_BG = _O_BKV + 6 * B_KV_HEADS * HEAD_DIM
_O_CQ = _O_BG + 3 * B_HEADS
_O_CK = _O_CQ + C_HEADS * HEAD_DIM
_O_CV = _O_CK + C_HEADS * HEAD_DIM
_O_MIX = _O_CV + C_HEADS * HEAD_DIM
_O_END = _O_MIX + 3 * D_MODEL

TQ = 128
TK = 128
KB_TILES = 4
A_TQ = 512
A_PAIR = 2
CLASS_TILE = 512
C_TILES_PER_STEP = 4
B_TQ = 256
B_PAIR = 2
LOG2E = 1.4426950408889634
LANES = 128
VMEM_LIMIT = 56 * 1024 * 1024
INT_MIN = -2 ** 31
NEG_INIT = -1e30
SCALE = HEAD_DIM ** -0.5


def _params(sem):
    return pltpu.CompilerParams(dimension_semantics=sem, vmem_limit_bytes=VMEM_LIMIT)


def _dot_t(a, b):
    return lax.dot_general(a, b, (((1,), (1,)), ((), ())), preferred_element_type=F32)


def _dot(a, b):
    return jnp.dot(a, b, preferred_element_type=F32)


def _split_dot_left(a_bf16, b):
    hi = b.astype(BF16)
    lo = (b - hi.astype(F32)).astype(BF16)
    return _dot(a_bf16, hi) + _dot(a_bf16, lo)


def _split_dot(a, b_bf16):
    hi = a.astype(BF16)
    lo = (a - hi.astype(F32)).astype(BF16)
    return _dot(hi, b_bf16) + _dot(lo, b_bf16)


def _bucket_table(n_max):
    n = np.arange(n_max, dtype=np.int64)
    exact = REL_BUCKETS // 2
    nf = np.maximum(n, 1).astype(np.float32)
    large = exact + (np.log(nf / np.float32(exact)) / np.float32(math.log(REL_MAX_DIST / exact))
                     * np.float32(REL_BUCKETS - exact)).astype(np.int32)
    return np.where(n < exact, n, np.minimum(large, REL_BUCKETS - 1)).astype(np.int32)


def _num_bias_tiles(seq, tq):
    bucket = _bucket_table(seq + tq)
    first_sat = int(np.min(np.nonzero(bucket == REL_BUCKETS - 1)[0]))
    assert np.all(bucket[first_sat:] == REL_BUCKETS - 1)
    nd = -(-(first_sat + TK - 1) // TK) + tq // TK
    return min(nd, seq // TK)


def _slot_bias(bias_ref, nheads, pair, rq, nd, i, p, j):
    parts = []
    for u in range(pair):
        for sub in range(rq):
            k = jnp.clip(rq * i + sub - j, 0, nd - 1)
            parts.append(bias_ref[k * nheads + p * pair + u])
    return jnp.concatenate(parts, axis=1)


def _bias_kernel(nh, group, scale, idx_ref, rel_ref, o_ref):
    idx = idx_ref[0]
    c = idx.shape[1]
    acc = [jnp.zeros(idx.shape, F32) for _ in range(nh)]
    for b in range(REL_BUCKETS):
        hit = idx == b
        for h in range(nh):
            acc[h] = jnp.where(hit, rel_ref[b, h] * scale, acc[h])
    for h in range(nh):
        o_ref[h // group, :, (h % group) * c:(h % group + 1) * c] = acc[h]


def _bias_tiles(rel_cols, idx, group=1, scale=1.0):
    n, r, c = idx.shape
    nh = rel_cols.shape[1]
    return pl.pallas_call(
        functools.partial(_bias_kernel, nh, group, scale),
        grid=(n,),
        in_specs=[pl.BlockSpec((1, r, c), lambda k: (k, 0, 0)),
                  pl.BlockSpec(memory_space=pltpu.SMEM)],
        out_specs=pl.BlockSpec((nh // group, r, group * c), lambda k: (k, 0, 0)),
        out_shape=jax.ShapeDtypeStruct((n * nh // group, r, group * c), F32),
        compiler_params=_params(("parallel",)),
        name="bias_tiles",
    )(jnp.asarray(idx, I32), rel_cols.astype(F32))


def _toeplitz_bias(rel_cols, seq, tq=TQ, keys_on_rows=False, group=1, scale=1.0):
    nd = _num_bias_tiles(seq, tq)
    bucket = _bucket_table(seq + tq)
    d = ((np.arange(nd)[:, None, None] - (tq // TK - 1)) * TK
         + np.arange(tq)[None, :, None] - np.arange(TK)[None, None, :])
    idx = bucket[np.clip(d, 0, None)]
    return _bias_tiles(rel_cols, idx.transpose(0, 2, 1) if keys_on_rows else idx, group, scale), nd


def _dilated_bias(rel_cols, dil):
    bucket = _bucket_table(2 * TK * dil + 1)
    du = np.arange(TQ)[:, None] + TK - np.arange(2 * TK)[None, :]
    return _bias_tiles(rel_cols, bucket[np.clip(du, 0, None) * dil][None])


def _proj_kernel(kinds, *refs):
    n = len(kinds)
    x_ref, g_ref, gs_ref = refs[0], refs[1], refs[2]
    w_refs = refs[3:3 + 2 * n:2]
    aux_refs = refs[4:4 + 2 * n:2]
    n_out = sum(2 if isinstance(k, tuple) and k[0] == "trans" else 1 for k in kinds)
    out_refs = iter(refs[3 + 2 * n:3 + 2 * n + n_out])
    scratch_refs = refs[3 + 2 * n + n_out:]
    x = x_ref[...]
    ms = jnp.mean(x * x, axis=-1, keepdims=True)
    h = (x * lax.rsqrt(ms + NORM_EPS) * g_ref[...]).astype(BF16)
    for kind, w_ref, aux_ref in zip(kinds, w_refs, aux_refs):
        if isinstance(kind, tuple) and kind[0] == "trans":
            _, nv, nextra, act = kind
            vt_ref, ex_ref = next(out_refs), next(out_refs)
            yt = _dot_t(w_ref[...], h)
            rows = lax.broadcasted_iota(I32, (nv * LANES, yt.shape[1]), 0)
            vt = (yt[0:nv * LANES] + jnp.where(rows % LANES >= HEAD_DIM, 1.0, 0.0)).astype(vt_ref.dtype)
            for c in range(vt_ref.shape[0]):
                vt_ref[c] = vt[:, c * TK:(c + 1) * TK]
            ex = yt[nv * LANES:nv * LANES + nextra]
            ex_ref[...] = 1.0 / (1.0 + jnp.exp(-ex)) if act == "sigmoid" else ex
            continue
        o_ref = next(out_refs)
        classes = kind[1] if isinstance(kind, tuple) and kind[0] == "classes" else 0
        chunk = kind[1] if isinstance(kind, tuple) and kind[0] == "chunks" else 0
        width = w_ref.shape[1]
        cw = 256 if width % 256 == 0 else LANES
        ys = [_dot(h, w_ref[:, c0:c0 + cw]) for c0 in range(0, width, cw)]
        for c0, y in zip(range(0, width, cw), ys):
            if kind == "norm" or classes:
                gsum = _split_dot(y * y, gs_ref[:cw, :cw])
                r = lax.rsqrt(gsum * (1.0 / HEAD_DIM) + NORM_EPS)
                mask = aux_ref[0:1, c0:c0 + cw]
                fac = mask * (r * aux_ref[1:2, c0:c0 + cw]) + (1.0 - mask)
                y = y * fac + aux_ref[2:3, c0:c0 + cw]
            elif kind == "sigmoid":
                y = 1.0 / (1.0 + jnp.exp(-y))
            if classes or chunk:
                for c in range(0, cw, LANES):
                    scratch_refs[0][(c0 + c) // LANES] = y[:, c:c + LANES]
            else:
                o_ref[:, c0:c0 + cw] = y.astype(o_ref.dtype)
        for r in range(classes):
            rows = pl.ds(r, x.shape[0] // classes, stride=classes)
            o_ref[r] = jnp.concatenate([scratch_refs[0][c, rows, :] for c in range(width // LANES)],
                                       axis=1).astype(o_ref.dtype)
        for r in range(chunk):
            rows = pl.ds(r, x.shape[0] // chunk, stride=chunk)
            for c in range(width // LANES):
                o_ref[:, r * width + c * LANES:r * width + (c + 1) * LANES] = scratch_refs[0][c, rows, :]


def _proj(x2d, gain, pieces, tm):
    m, d = x2d.shape
    kinds = tuple(p[0] for p in pieces)
    gs = (np.arange(256)[:, None] // HEAD_DIM == np.arange(256)[None, :] // HEAD_DIM)
    gs = jnp.asarray(gs, BF16)
    in_specs = [pl.BlockSpec((tm, d), lambda i: (i, 0)),
                pl.BlockSpec((1, d), lambda i: (0, 0)),
                pl.BlockSpec((256, 256), lambda i: (0, 0))]
    args = [x2d, gain.reshape(1, d).astype(F32), gs]
    out_specs, out_shapes, scratch = [], [], []
    for kind, w, aux, dt in pieces:
        in_specs += [pl.BlockSpec(w.shape, lambda i: (0, 0)), pl.BlockSpec(aux.shape, lambda i: (0, 0))]
        args += [w, aux]
        if isinstance(kind, tuple) and kind[0] == "trans":
            _, nv, nextra, _ = kind
            out_specs += [pl.BlockSpec((tm // TK, nv * LANES, TK), lambda i: (i, 0, 0)),
                          pl.BlockSpec((nextra, tm), lambda i: (0, i))]
            out_shapes += [jax.ShapeDtypeStruct((m // TK, nv * LANES, TK), dt),
                           jax.ShapeDtypeStruct((nextra, m), F32)]
            continue
        nw = w.shape[1]
        if isinstance(kind, tuple) and kind[0] == "chunks":
            n = kind[1]
            out_specs.append(pl.BlockSpec((tm // n, n * nw), lambda i: (i, 0)))
            out_shapes.append(jax.ShapeDtypeStruct((m // n, n * nw), dt))
            scratch = [pltpu.VMEM((nw // LANES, tm, LANES), F32)]
            continue
        if isinstance(kind, tuple) and kind[0] == "classes":
            dil = kind[1]
            assert tm == CLASS_TILE
            out_specs.append(pl.BlockSpec((None, dil, tm // dil, nw), lambda i: (i, 0, 0, 0)))
            out_shapes.append(jax.ShapeDtypeStruct((m // tm, dil, tm // dil, nw), dt))
            scratch = [pltpu.VMEM((nw // LANES, tm, LANES), F32)]
            continue
        out_specs.append(pl.BlockSpec((tm, nw), lambda i: (i, 0)))
        out_shapes.append(jax.ShapeDtypeStruct((m, nw), dt))
    return pl.pallas_call(
        functools.partial(_proj_kernel, kinds),
        grid=(m // tm,),
        in_specs=in_specs, out_specs=out_specs, out_shape=out_shapes, scratch_shapes=scratch,
        compiler_params=_params(("parallel",)),
        name="proj",
    )(*args)


def _aux(width, mask=None, gain=None, add=None):
    z = jnp.zeros((width,), F32)
    return jnp.stack([z if mask is None else mask, z if gain is None else gain, z if add is None else add])


def _seg(*parts):
    ref = next(p for p in parts if not isinstance(p, int))
    return jnp.concatenate([jnp.zeros(ref.shape[:-1] + (p,), ref.dtype) if isinstance(p, int) else p
                            for p in parts], axis=-1)


def _proj_pieces(w_in, qk):
    w = w_in.astype(BF16)
    hd = HEAD_DIM
    ones, zeros = jnp.ones((hd,), F32), jnp.zeros((hd,), F32)
    cat = jnp.concatenate

    def cols(a, b):
        return w[:, a:b]

    pieces_ab = [
        ("norm", cols(_O_AQ, _O_AK), _aux(512, jnp.ones((512,), F32), jnp.tile(qk[0], A_HEADS) * (SCALE * LOG2E)),
         BF16),
        ("norm", _seg(cols(_O_AK, _O_AV), hd), _aux(LANES, cat([ones, zeros]), cat([qk[1], zeros])), BF16),
        (("trans", 1, 8, "none"), _seg(cols(_O_AV, _O_IQ), hd, cols(_O_IW, _O_BQ), 16 - A_IDX_HEADS).T,
         _aux(LANES), BF16),
        ("plain", cols(_O_IQ, _O_IK), _aux(256), BF16),
        ("plain", _seg(cols(_O_IK, _O_IW), LANES - hd), _aux(LANES), BF16),
        ("norm", cols(_O_BQ, _O_BKV), _aux(512, jnp.ones((512,), F32), jnp.tile(qk[2], B_HEADS) * (SCALE * LOG2E)),
         BF16),
        (("chunks", B_CMP_STRIDE), cols(_O_BKV, _O_BKV + 256), _aux(256), F32),
    ]
    o = _O_BKV + 256
    ks0, ks1, vs0, vs1, kw0, kw1, vw0, vw1 = [cols(o + i * hd, o + (i + 1) * hd) for i in range(8)]
    pieces_ab += [
        ("norm", _seg(ks0, hd, ks1, hd, kw0, hd, kw1, hd),
         _aux(512, cat([ones, zeros] * 4), cat([qk[3], zeros] * 4)), BF16),
        (("trans", 4, 32, "sigmoid"),
         _seg(vs0, hd, vs1, hd, vw0, hd, vw1, hd, cols(_O_BG, _O_CQ), 32 - 3 * B_HEADS).T, _aux(LANES), BF16),
    ]
    gw = C_HEADS_PER_GROUP * hd
    pieces_c = []
    for g, (_, dil) in enumerate(C_GROUPS):
        wg = cat([cols(o0 + g * gw, o0 + (g + 1) * gw) for o0 in (_O_CQ, _O_CK, _O_CV)], axis=1)
        mask = cat([jnp.ones((2 * gw,), F32), jnp.zeros((gw,), F32)])
        gain = cat([jnp.tile(qk[4], C_HEADS_PER_GROUP) * SCALE, jnp.tile(qk[5], C_HEADS_PER_GROUP),
                    jnp.zeros((gw,), F32)])
        pieces_c.append((("classes", dil), wg, _aux(3 * gw, mask, gain), BF16))
    pieces_g = [("sigmoid", cols(_O_MIX, _O_END), _aux(3 * D_MODEL), BF16)]
    return pieces_ab, pieces_c, pieces_g


def _flash_init(m_ref, acc_ref):
    m_ref[...] = jnp.full(m_ref.shape, NEG_INIT, F32)
    acc_ref[...] = jnp.zeros(acc_ref.shape, F32)


def _flash_block_t(q_ref, slots, group_of, kt, v_aug_t, bias_fn, masks, m_ref, acc_ref):
    slots = list(slots)
    s_all = [_dot_t(kt[group_of(p)], q_ref[p]) for p in slots]
    m_old = [m_ref[p][0:1] for p in slots]
    ps, alphas = [], []
    for k, p in enumerate(slots):
        s, mk = s_all[k], masks[group_of(p)]
        sc = [jnp.where(mk[c], s[c * TK:(c + 1) * TK] + bias_fn(p, c), -jnp.inf).astype(BF16)
              for c in range(len(mk))]
        m_blk = jnp.max(functools.reduce(jnp.maximum, sc).astype(F32), axis=0, keepdims=True)
        m_new = jnp.maximum(m_old[k], m_blk)
        m16 = m_new.astype(BF16)
        ps.append(jnp.concatenate([jnp.exp2(x - m16) for x in sc], axis=0))
        alphas.append(jnp.exp2(m_old[k] - m_new))
        m_ref[p] = jnp.broadcast_to(m_new, m_ref.shape[1:])
    for k, p in enumerate(slots):
        acc_ref[p] = alphas[k] * acc_ref[p] + _dot(v_aug_t[group_of(p)], ps[k])


def _dsa_kernel(topk, nd, pbits, iq_ref, iwt_ref, ik_ref, aq_ref, ak_ref, avt_ref, bias_ref, o_ref,
                keys_ref, half_ref, iqs_ref, qs_ref, m_ref, acc_ref):
    TQ = A_TQ
    i = pl.program_id(1)
    nk = (i + 1) * (TQ // TK)
    nkb = (nk + KB_TILES - 1) // KB_TILES
    kb = KB_TILES * TK
    krow = lax.broadcasted_iota(I32, (TK, TQ), 0)
    qpos = i * TQ + lax.broadcasted_iota(I32, (TK, TQ), 1)
    i16 = jnp.int16
    last = keys_ref.shape[0] * TK - 1

    iwt = iwt_ref[...]
    for h in range(A_IDX_HEADS):
        iqs_ref[h * TQ:(h + 1) * TQ] = iq_ref[:, h * HEAD_DIM:(h + 1) * HEAD_DIM]

    def score_block(jb, c):
        r0 = pl.multiple_of(jb * kb, kb)
        d = _dot_t(ik_ref[pl.ds(r0, kb), 0:HEAD_DIM], iqs_ref[...])
        acc = jnp.zeros((kb, TQ), F32)
        for h in range(A_IDX_HEADS):
            acc = acc + jnp.maximum(d[:, h * TQ:(h + 1) * TQ], 0.0) * iwt[h:h + 1]
        bits = lax.bitcast_convert_type(acc, I32)
        key = jnp.where(bits < 0, bits ^ 0x7FFFFFFF, bits + (last + 1))
        for t in range(KB_TILES):
            j = jb * KB_TILES + t
            kidx = j * TK + krow
            kj = jnp.where(acc[t * TK:(t + 1) * TK] == 0.0, last - kidx, key[t * TK:(t + 1) * TK])
            kj = jnp.where(kidx <= qpos, kj, INT_MIN)
            keys_ref[j] = kj
            half_ref[j] = jnp.right_shift(kj, 16).astype(i16)
        return c

    lax.fori_loop(0, nkb, score_block, 0)

    kf = float(topk)

    nb_max = keys_ref.shape[0] // KB_TILES
    sub = 16

    def search_half(nbits, u0):
        def run(nblk, u_init):
            def step(b, u):
                cand = u | jnp.left_shift(jnp.int32(1), nbits - 1 - b)
                cb = jnp.broadcast_to((cand - 32768).astype(i16), (TK, TQ))
                acc = jnp.zeros((sub, TQ), i16)
                for j in range(nblk * KB_TILES):
                    hit = jnp.where(half_ref[j] >= cb, jnp.ones((), i16), jnp.zeros((), i16))
                    acc = acc + functools.reduce(lambda a, b2: a + b2,
                                                 [hit[r:r + sub] for r in range(0, TK, sub)])
                cnt = jnp.sum(acc.astype(F32), axis=0, keepdims=True)
                return jnp.where(cnt >= kf, cand, u)

            return lax.fori_loop(0, nbits, step, u_init)

        return lax.switch(nkb - 1, [functools.partial(run, n) for n in range(1, nb_max + 1)], u0)

    def for_tiles(fn):
        def body(jb, c):
            for t in range(KB_TILES):
                fn(jb * KB_TILES + t)
            return c
        lax.fori_loop(0, nkb, body, 0)

    zero = jnp.zeros((1, TQ), I32)
    t_hi = search_half(16, zero) - 32768

    def low_tile(j):
        k = keys_ref[j]
        hi = jnp.right_shift(k, 16)
        lo = (k & 0xFFFF) - 32768
        half_ref[j] = jnp.where(hi > t_hi, 32767, jnp.where(hi < t_hi, -32768, lo)).astype(i16)

    for_tiles(low_tile)
    thr = t_hi * 65536 + search_half(16, zero)

    def count_ge(jb, acc):
        for t in range(KB_TILES):
            acc = acc + jnp.where(keys_ref[jb * KB_TILES + t] >= thr, 1.0, 0.0)
        return acc

    n_ge = jnp.sum(lax.fori_loop(0, nkb, count_ge, jnp.zeros((TK, TQ), F32)), axis=0, keepdims=True)

    @pl.when(jnp.max(jnp.where(thr > INT_MIN, n_ge, 0.0)) > kf)
    def _():
        def tie_tile(j):
            k = keys_ref[j]
            rev = last - (j * TK + krow)
            half_ref[j] = jnp.where(k > thr, 32767, jnp.where(k == thr, rev, -32768)).astype(i16)

        for_tiles(tie_tile)
        keep = search_half(pbits, zero + 32768) - 32768

        def demote(j):
            k = keys_ref[j]
            rev = last - (j * TK + krow)
            keys_ref[j] = jnp.where((k == thr) & (rev < keep) & (thr > INT_MIN), k - 1, k)

        for_tiles(demote)

    sel_thr = jnp.maximum(thr, INT_MIN + 1)

    nslot = A_HEADS // A_PAIR
    for h in range(A_HEADS):
        qs_ref[h // A_PAIR, (h % A_PAIR) * TQ:(h % A_PAIR + 1) * TQ] = aq_ref[:, h * HEAD_DIM:(h + 1) * HEAD_DIM]
    _flash_init(m_ref, acc_ref)

    def att_block(jb, c):
        r0 = pl.multiple_of(jb * kb, kb)
        kt = ak_ref[pl.ds(r0, kb), 0:HEAD_DIM]
        j0 = jb * KB_TILES
        masks = []
        for t in range(KB_TILES):
            mk = keys_ref[j0 + t] >= sel_thr
            masks.append(jnp.concatenate([mk] * A_PAIR, axis=1))
        vat = jnp.concatenate([avt_ref[j0 + t] for t in range(KB_TILES)], axis=1)
        _flash_block_t(qs_ref, range(nslot), lambda p: 0, [kt], [vat],
                       lambda p, t: _slot_bias(bias_ref, A_HEADS, A_PAIR, TQ // TK, nd, i, p, j0 + t), [masks],
                       m_ref, acc_ref)
        return c

    lax.fori_loop(0, nkb, att_block, 0)
    outs = []
    for h in range(A_HEADS):
        acc = acc_ref[h // A_PAIR][:, (h % A_PAIR) * TQ:(h % A_PAIR + 1) * TQ]
        outs.append(acc[:HEAD_DIM] / jnp.maximum(acc[HEAD_DIM:HEAD_DIM + 1], 1e-30))
    o_ref[...] = jnp.concatenate(outs, axis=0).T.astype(o_ref.dtype)


def _dsa(iq, iwt, ik, aq, ak, avt, bias, nd, bsz, seq):
    TQ = A_TQ
    nq = seq // TQ
    kb = KB_TILES * TK
    topk = min(A_TOPK_MAX, seq // 4)
    pbits = max(1, (seq - 1).bit_length())
    return pl.pallas_call(
        functools.partial(_dsa_kernel, topk, nd, pbits),
        grid=(bsz, nq),
        in_specs=[
            pl.BlockSpec((TQ, 256), lambda b, i: (b * nq + i, 0)),
            pl.BlockSpec((8, TQ), lambda b, i: (0, b * nq + i)),
            pl.BlockSpec((seq, LANES), lambda b, i: (b, 0)),
            pl.BlockSpec((TQ, 512), lambda b, i: (b * nq + i, 0)),
            pl.BlockSpec((seq, LANES), lambda b, i: (b, 0)),
            pl.BlockSpec((seq // TK, LANES, TK), lambda b, i: (b, 0, 0)),
            pl.BlockSpec(bias.shape, lambda b, i: (0, 0, 0), pipeline_mode=pl.Buffered(1)),
        ],
        out_specs=pl.BlockSpec((TQ, 512), lambda b, i: (b * nq + i, 0)),
        out_shape=jax.ShapeDtypeStruct((bsz * seq, 512), BF16),
        scratch_shapes=[pltpu.VMEM((seq // TK, TK, TQ), I32),
                        pltpu.VMEM((seq // TK, TK, TQ), jnp.int16),
                        pltpu.VMEM((A_IDX_HEADS * TQ, HEAD_DIM), BF16),
                        pltpu.VMEM((A_HEADS // A_PAIR, A_PAIR * TQ, HEAD_DIM), BF16),
                        pltpu.VMEM((A_HEADS // A_PAIR, 8, A_PAIR * TQ), F32),
                        pltpu.VMEM((A_HEADS // A_PAIR, LANES, A_PAIR * TQ), F32)],
        compiler_params=_params(("parallel", "arbitrary")),
        name="dsa",
    )(iq, iwt, ik, aq, ak, avt, bias)


def _nsa_cmp_kernel(x_ref, pos_ref, wlo_ref, whi_ref, gain_ref, k_ref, v_ref):
    x = x_ref[...]
    lo = _dot((x + pos_ref[0:1, :]).astype(BF16), wlo_ref[...])
    hi = _dot((x + pos_ref[1:2, :]).astype(BF16), whi_ref[...])
    nrow = x.shape[0]
    pre = lo + pltpu.roll(hi, nrow - 1, 0)
    ks = []
    for g in range(B_KV_HEADS):
        kg = pre[:, g * HEAD_DIM:(g + 1) * HEAD_DIM]
        ms = jnp.mean(kg * kg, axis=-1, keepdims=True)
        ks.append(kg * lax.rsqrt(ms + NORM_EPS) * gain_ref[...])
    k_ref[...] = jnp.concatenate(ks, axis=1).astype(k_ref.dtype)
    v_ref[...] = pre[:, LANES:2 * LANES].T.astype(v_ref.dtype)


def _nsa_cmp(bcmp, cmp_pos, cmp_w, k_gain, bsz, seq):
    nch = seq // B_CMP_STRIDE
    half = B_CMP_LEN // 2
    width = half * 256
    x = bcmp

    def wmat(l0):
        w = jnp.zeros((half, 4, HEAD_DIM, 4, HEAD_DIM), F32)
        for j in range(4):
            w = w.at[:, j, :, j, :].set(cmp_w[j // 2, l0:l0 + half])
        return w.reshape(width, 256).astype(BF16)

    def prow(l0):
        p = jnp.stack([cmp_pos[0, l0:l0 + half], cmp_pos[0, l0:l0 + half],
                       cmp_pos[1, l0:l0 + half], cmp_pos[1, l0:l0 + half]], axis=1)
        return p.reshape(width)

    pos = jnp.stack([prow(0), prow(half)]).astype(F32)
    return pl.pallas_call(
        _nsa_cmp_kernel,
        grid=(bsz,),
        in_specs=[
            pl.BlockSpec((nch, width), lambda b: (b, 0)),
            pl.BlockSpec((2, width), lambda b: (0, 0)),
            pl.BlockSpec((width, 256), lambda b: (0, 0)),
            pl.BlockSpec((width, 256), lambda b: (0, 0)),
            pl.BlockSpec((1, HEAD_DIM), lambda b: (0, 0)),
        ],
        out_specs=[pl.BlockSpec((nch, LANES), lambda b: (b, 0)), pl.BlockSpec((LANES, nch), lambda b: (b, 0))],
        out_shape=[jax.ShapeDtypeStruct((bsz * nch, LANES), BF16), jax.ShapeDtypeStruct((bsz * LANES, nch), BF16)],
        compiler_params=_params(("parallel",)),
        name="nsa_cmp",
    )(x, pos, wmat(0), wmat(half), k_gain.reshape(1, HEAD_DIM).astype(F32))


def _nsa_kernel(seq, nd, q_ref, gt_ref, kc_ref, vct_ref, k_ref, vt_ref, bias_ref, o_ref,
                imp_ref, sel_ref, qs_ref, m_ref, acc_ref):
    TQ = B_TQ
    rq = TQ // TK
    i = pl.program_id(1)
    ncp = seq // B_CMP_STRIDE
    ns = seq // B_SEL_LEN
    n_top = min(B_SEL_TOPK_MAX, ns)
    nslot = B_HEADS // B_PAIR
    spg = B_GROUP // B_PAIR
    krow = lax.broadcasted_iota(I32, (TK, TQ), 0)
    qpos = i * TQ + lax.broadcasted_iota(I32, (TK, TQ), 1)

    def lanes(h):
        return slice((h % B_PAIR) * TQ, (h % B_PAIR + 1) * TQ)

    def dup(x):
        return jnp.concatenate([x] * B_PAIR, axis=1)

    for h in range(B_HEADS):
        qs_ref[h // B_PAIR, lanes(h)] = q_ref[:, h * HEAD_DIM:(h + 1) * HEAD_DIM]

    n_idx = lax.broadcasted_iota(I32, (ncp, TQ), 0)
    t_c = i * TQ + lax.broadcasted_iota(I32, (ncp, TQ), 1)
    cmask = dup(n_idx * B_CMP_STRIDE + (B_CMP_LEN - 1) <= t_c)
    om = lax.broadcasted_iota(I32, (ns, ncp), 0) * B_SEL_LEN
    on = lax.broadcasted_iota(I32, (ns, ncp), 1) * B_CMP_STRIDE
    ovt = jnp.where((on < om + B_SEL_LEN) & (on + B_CMP_LEN > om), 1.0, 0.0).astype(BF16)
    m_idx = lax.broadcasted_iota(I32, (ns, TQ), 0)
    jt = (i * TQ + lax.broadcasted_iota(I32, (ns, TQ), 1)) // B_SEL_LEN
    forced = (m_idx == 0) | (m_idx == jt) | (m_idx == jt - 1)

    kc = [kc_ref[:, g * HEAD_DIM:(g + 1) * HEAD_DIM] for g in range(B_KV_HEADS)]
    vct = [vct_ref[g * HEAD_DIM:(g + 1) * HEAD_DIM, :] for g in range(B_KV_HEADS)]
    st_all = [_dot_t(kc[p // spg], qs_ref[p]) for p in range(nslot)]
    pts = []
    for p in range(nslot):
        st = jnp.where(cmask, st_all[p], -jnp.inf)
        mx = jnp.max(st, axis=0, keepdims=True)
        mx = jnp.where(mx == -jnp.inf, 0.0, mx)
        e = jnp.exp2(st - mx)
        pts.append(e / jnp.maximum(jnp.sum(e, axis=0, keepdims=True), 1e-30))
    oc = [_dot(vct[p // spg], pts[p].astype(BF16)) for p in range(nslot)]

    for g in range(B_KV_HEADS):
        psum = functools.reduce(lambda a, b: a + b, [pts[h // B_PAIR][:, lanes(h)]
                                                     for h in range(g * B_GROUP, (g + 1) * B_GROUP)])
        imp = _split_dot_left(ovt, psum)
        imp = jnp.where(forced, jnp.inf, jnp.where(m_idx <= jt, imp, -jnp.inf))

        sub = 8
        imp_ref[g] = imp
        grp = [imp[r:r + sub] for r in range(0, ns, sub)]
        rank = [jnp.zeros((sub, TQ), F32) for _ in grp]
        rsub = lax.broadcasted_iota(I32, (sub, TQ), 0)
        for mp in range(ns):
            vp = jnp.broadcast_to(imp_ref[g, mp:mp + 1, :], (sub, TQ))
            for r in range(len(grp)):
                if r * sub + sub - 1 < mp:
                    before = vp > grp[r]
                elif r * sub > mp:
                    before = vp >= grp[r]
                else:
                    before = (vp > grp[r]) | ((vp == grp[r]) & (rsub + r * sub > mp))
                rank[r] = rank[r] + jnp.where(before, 1.0, 0.0)
        rank = jnp.concatenate(rank, axis=0)
        sel_ref[g] = jnp.where((rank < float(n_top)) & (m_idx <= jt), 1.0, 0.0)

    def slot_bias(p, j):
        return _slot_bias(bias_ref, B_HEADS, B_PAIR, rq, nd, i, p, j)

    def branch_block(j0, ntiles, koff, vrow, mask_fn):
        rows = pl.ds(pl.multiple_of(j0 * TK, TK), ntiles * TK)
        kt = [k_ref[rows, koff + g * LANES:koff + g * LANES + HEAD_DIM] for g in range(B_KV_HEADS)]
        vt = [jnp.concatenate([vt_ref[j0 + t, vrow + g * LANES:vrow + (g + 1) * LANES, :] for t in range(ntiles)],
                              axis=1) for g in range(B_KV_HEADS)]
        masks = [[dup(mask_fn(g, j0 + t)) for t in range(ntiles)] for g in range(B_KV_HEADS)]
        _flash_block_t(qs_ref, range(nslot), lambda p: p // spg, kt, vt, lambda p, t: slot_bias(p, j0 + t),
                       masks, m_ref, acc_ref)

    bpt = TK // B_SEL_LEN

    def sel_mask(g, j):
        chosen = jnp.concatenate([jnp.broadcast_to(sel_ref[g, pl.ds(j * bpt + b, 1), :], (B_SEL_LEN, TQ))
                                  for b in range(bpt)], axis=0) > 0.5
        return chosen & (j * TK + krow <= qpos)

    def win_mask(g, j):
        dist = qpos - (j * TK + krow)
        return (dist >= 0) & (dist < B_WINDOW)

    def sel_body(jb, c):
        branch_block(jb * KB_TILES, KB_TILES, 0, 0, sel_mask)
        return c

    gt = gt_ref[...]

    def gate(h, br):
        return gt[3 * h + br:3 * h + br + 1]

    def head_out(h):
        a = acc_ref[h // B_PAIR][:, lanes(h)]
        return a[:HEAD_DIM] / jnp.maximum(a[HEAD_DIM:HEAD_DIM + 1], 1e-30)

    _flash_init(m_ref, acc_ref)
    lax.fori_loop(0, (rq * (i + 1) + KB_TILES - 1) // KB_TILES, sel_body, 0)
    part = [gate(h, 0) * oc[h // B_PAIR][:, lanes(h)] + gate(h, 1) * head_out(h) for h in range(B_HEADS)]

    wt = B_WINDOW // TK + rq
    _flash_init(m_ref, acc_ref)
    branch_block(jnp.maximum(rq * (i + 1) - wt, 0), wt, 2 * LANES, 2 * LANES, win_mask)
    outs = [part[h] + gate(h, 2) * head_out(h) for h in range(B_HEADS)]
    o_ref[...] = jnp.concatenate(outs, axis=0).T.astype(o_ref.dtype)


def _nsa(bq, bgt, kcmp, vcmpt, bk, bvt, bias, nd, bsz, seq):
    TQ = B_TQ
    nq = seq // TQ
    ncp = seq // B_CMP_STRIDE
    ns = seq // B_SEL_LEN
    nslot, w = B_HEADS // B_PAIR, B_PAIR * TQ
    return pl.pallas_call(
        functools.partial(_nsa_kernel, seq, nd),
        grid=(bsz, nq),
        in_specs=[
            pl.BlockSpec((TQ, 512), lambda b, i: (b * nq + i, 0)),
            pl.BlockSpec((32, TQ), lambda b, i: (0, b * nq + i)),
            pl.BlockSpec((ncp, LANES), lambda b, i: (b, 0)),
            pl.BlockSpec((LANES, ncp), lambda b, i: (b, 0)),
            pl.BlockSpec((seq, 512), lambda b, i: (b, 0)),
            pl.BlockSpec((seq // TK, 4 * LANES, TK), lambda b, i: (b, 0, 0)),
            pl.BlockSpec(bias.shape, lambda b, i: (0, 0, 0), pipeline_mode=pl.Buffered(1)),
        ],
        out_specs=pl.BlockSpec((TQ, 512), lambda b, i: (b * nq + i, 0)),
        out_shape=jax.ShapeDtypeStruct((bsz * seq, 512), BF16),
        scratch_shapes=[pltpu.VMEM((B_KV_HEADS, ns, TQ), F32), pltpu.VMEM((B_KV_HEADS, ns, TQ), F32),
                        pltpu.VMEM((nslot, w, HEAD_DIM), BF16),
                        pltpu.VMEM((nslot, 8, w), F32),
                        pltpu.VMEM((nslot, LANES, w), F32)],
        compiler_params=_params(("parallel", "arbitrary")),
        name="nsa",
    )(bq, bgt, kcmp, vcmpt, bk, bvt, bias)


def _dil_kernel(q_ref, kp_ref, kc_ref, vp_ref, vc_ref, bias_ref, o_ref, lse_ref):
    i = pl.program_id(2)

    def rows(ref):
        return ref[...].reshape(-1, ref.shape[-1])

    q = rows(q_ref)
    nt = q.shape[0] // TQ
    k = jnp.concatenate([rows(kp_ref), rows(kc_ref)], axis=0)
    v = jnp.concatenate([rows(vp_ref), rows(vc_ref)], axis=0)
    row = lax.broadcasted_iota(I32, (TQ, 2 * TK), 0)
    col = lax.broadcasted_iota(I32, (TQ, 2 * TK), 1)
    du = row + TK - col
    window = (du >= 0) & (du <= TK)
    units = [(u, hh) for u in range(nt) for hh in range(C_HEADS_PER_GROUP)]

    def head(x, u, n, hh):
        return x[u * TQ:(u + n) * TQ, hh * HEAD_DIM:(hh + 1) * HEAD_DIM]

    scores = [_dot_t(head(q, u, 1, hh), head(k, u, 2, hh)) for u, hh in units]
    es, dens, lses = [], [], []
    for (u, hh), s in zip(units, scores):
        valid = window & ((col >= TK) | (i * nt + u > 0))
        s = jnp.where(valid, s + bias_ref[hh], -jnp.inf)
        m = jnp.max(s, axis=-1, keepdims=True)
        e = jnp.exp(s - m)
        den = jnp.sum(e, axis=-1, keepdims=True)
        es.append(e.astype(BF16))
        dens.append(den)
        lses.append(jnp.broadcast_to(m + jnp.log(den), (TQ, HEAD_DIM)))
    outs = [_dot(e, head(v, u, 2, hh)) / den for (u, hh), e, den in zip(units, es, dens)]

    def assemble(parts):
        tiles = [jnp.concatenate(parts[u * C_HEADS_PER_GROUP:(u + 1) * C_HEADS_PER_GROUP], axis=1)
                 for u in range(nt)]
        return jnp.concatenate(tiles, axis=0)

    o_ref[...] = assemble(outs).reshape(o_ref.shape)
    lse_ref[...] = assemble(lses).reshape(lse_ref.shape)


def _dilated_group(qkv, bias, dil, bsz, seq):
    ln = seq // dil
    nt = min(C_TILES_PER_STEP, ln // TQ)
    gw = C_HEADS_PER_GROUP * HEAD_DIM
    rpc = CLASS_TILE // dil
    tiles_per_seq = seq // CLASS_TILE

    def spec(col, nrows, start):
        if rpc >= nrows:
            def index(b, r, i):
                s = start(i)
                return (b * tiles_per_seq + s // rpc, r, (s % rpc) // nrows, col)
            return pl.BlockSpec((None, None, nrows, gw), index)
        per = nrows // rpc
        assert tiles_per_seq % per == 0
        return pl.BlockSpec((per, None, rpc, gw),
                            lambda b, r, i: (b * (tiles_per_seq // per) + start(i) // nrows, r, 0, col))

    def cur(col):
        return spec(col, nt * TQ, lambda i: i * (nt * TQ))

    def prev(col):
        return spec(col, TQ, lambda i: jnp.maximum(i * nt - 1, 0) * TQ)

    return pl.pallas_call(
        _dil_kernel,
        grid=(bsz, dil, ln // (nt * TQ)),
        in_specs=[cur(0), prev(1), cur(1), prev(2), cur(2),
                  pl.BlockSpec(bias.shape, lambda b, r, i: (0, 0, 0))],
        out_specs=[cur(0)] * 2,
        out_shape=[jax.ShapeDtypeStruct(qkv.shape[:3] + (gw,), F32)] * 2,
        compiler_params=_params(("parallel", "parallel", "arbitrary")),
        name=f"dilated_d{dil}",
    )(qkv, qkv, qkv, qkv, qkv, bias)


def _merge_kernel(x_ref, ya_ref, yb_ref, o0_ref, l0_ref, o1_ref, l1_ref, o2_ref, l2_ref, g_ref,
                  wa_ref, wb_ref, wc_ref, wo_ref, out_ref, nat_ref):
    def natural(k, ref):
        dil, rpc = ref.shape[0], ref.shape[1]
        if dil == 1:
            return ref[0]
        ntile = ref.shape[2] // LANES
        for r in range(dil):
            blk = ref[r]
            for c in range(ntile):
                nat_ref[k * ntile + c, pl.ds(r, rpc, stride=dil), :] = blk[:, c * LANES:(c + 1) * LANES]
        return jnp.concatenate([nat_ref[k * ntile + c] for c in range(ntile)], axis=1)

    o0, l0 = natural(0, o0_ref), natural(1, l0_ref)
    o1, l1 = natural(2, o1_ref), natural(3, l1_ref)
    o2, l2 = natural(4, o2_ref), natural(5, l2_ref)
    mx = jnp.maximum(jnp.maximum(l0, l1), l2)
    e0, e1, e2 = jnp.exp(l0 - mx), jnp.exp(l1 - mx), jnp.exp(l2 - mx)
    yc = (e0 * o0 + e1 * o1 + e2 * o2) / (e0 + e1 + e2)
    ya = _dot(ya_ref[...], wa_ref[...])
    yb = _dot(yb_ref[...], wb_ref[...])
    yc = _dot(yc.astype(BF16), wc_ref[...])
    d = D_MODEL
    z = g_ref[:, 0:d] * ya + g_ref[:, d:2 * d] * yb + g_ref[:, 2 * d:3 * d] * yc
    out_ref[...] = x_ref[...] + _dot(z.astype(BF16), wo_ref[...])


def _merge(x2d, ya, yb, c_outs, mixg, wa, wb, wc, wo, tm):
    m = x2d.shape[0]

    def rows(w):
        return pl.BlockSpec((tm, w), lambda i: (i, 0))

    def full(a):
        return pl.BlockSpec(a.shape, lambda i: (0, 0))

    assert tm == CLASS_TILE
    c_flat = [a for pair in c_outs for a in pair]
    c_specs = [pl.BlockSpec((None,) + a.shape[1:], lambda i: (i, 0, 0, 0)) for a in c_flat]
    gw = C_HEADS_PER_GROUP * HEAD_DIM
    return pl.pallas_call(
        _merge_kernel,
        grid=(m // tm,),
        in_specs=[rows(D_MODEL), rows(512), rows(512)] + c_specs + [rows(3 * D_MODEL)]
                 + [full(wa), full(wb), full(wc), full(wo)],
        out_specs=rows(D_MODEL),
        out_shape=jax.ShapeDtypeStruct((m, D_MODEL), F32),
        scratch_shapes=[pltpu.VMEM((len(c_flat) * gw // LANES, tm, LANES), F32)],
        compiler_params=_params(("parallel",)),
        name="merge",
    )(x2d, ya, yb, *c_flat, mixg, wa, wb, wc, wo)


def _ffn_kernel(x_ref, g_ref, wg_ref, wu_ref, wd_ref, out_ref):
    x = x_ref[...]
    ms = jnp.mean(x * x, axis=-1, keepdims=True)
    h = (x * lax.rsqrt(ms + NORM_EPS) * g_ref[...]).astype(BF16)
    gate = _dot(h, wg_ref[...])
    up = _dot(h, wu_ref[...])
    act = gate / (1.0 + jnp.exp(-gate)) * up
    out_ref[...] = x + _dot(act.astype(BF16), wd_ref[...])


def _ffn(x2d, gain, w_in, w_out, tm):
    m = x2d.shape[0]
    wg = w_in[:, :D_FF].astype(BF16)
    wu = w_in[:, D_FF:].astype(BF16)
    wd = w_out.astype(BF16)

    def full(a):
        return pl.BlockSpec(a.shape, lambda i: (0, 0))

    return pl.pallas_call(
        _ffn_kernel,
        grid=(m // tm,),
        in_specs=[pl.BlockSpec((tm, D_MODEL), lambda i: (i, 0)), pl.BlockSpec((1, D_MODEL), lambda i: (0, 0)),
                  full(wg), full(wu), full(wd)],
        out_specs=pl.BlockSpec((tm, D_MODEL), lambda i: (i, 0)),
        out_shape=jax.ShapeDtypeStruct((m, D_MODEL), F32),
        compiler_params=_params(("parallel",)),
        name="ffn",
    )(x2d, gain.reshape(1, D_MODEL).astype(F32), wg, wu, wd)


def _layer(x2d, bsz, seq, norm1_g, norm2_g, w_in, qk, cmp_pos, cmp_w, w_a, w_b, w_c, w_out, w_ffn_in, w_ffn_out,
           bias_a, bias_b, bias_c):
    pieces_ab, pieces_c, pieces_g = _proj_pieces(w_in, qk)
    aq, ak, avt, iwt, iq, ik, bq, bcmp, bk, bvt, bgt = _proj(x2d, norm1_g, pieces_ab, 512)
    c_qkv = _proj(x2d, norm1_g, pieces_c, CLASS_TILE)
    (mixg,) = _proj(x2d, norm1_g, pieces_g, 512)

    ya = _dsa(iq, iwt, ik, aq, ak, avt, *bias_a, bsz, seq)
    kcmp, vcmpt = _nsa_cmp(bcmp, cmp_pos, cmp_w, qk[3], bsz, seq)
    yb = _nsa(bq, bgt, kcmp, vcmpt, bk, bvt, *bias_b, bsz, seq)
    c_outs = [_dilated_group(c_qkv[g], bias_c[g], dil, bsz, seq) for g, (_, dil) in enumerate(C_GROUPS)]

    x1 = _merge(x2d, ya, yb, c_outs, mixg, w_a.astype(BF16), w_b.astype(BF16), w_c.astype(BF16),
                w_out.astype(BF16), CLASS_TILE)
    return _ffn(x1, norm2_g, w_ffn_in, w_ffn_out, 256)


def kernel(x, norm1_g, norm2_g, w_in, qk_norm_g, nsa_cmp_pos, nsa_cmp_w, w_branch_a, w_branch_b, w_branch_c, w_out, w_ffn_in, w_ffn_out, rel_bias):
    bsz, seq, d = x.shape
    assert d == D_MODEL and seq % (TQ * max(dil for _, dil in C_GROUPS)) == 0 and seq % A_TQ == 0
    assert seq % (KB_TILES * TK) == 0 and seq % B_TQ == 0 and seq >= B_WINDOW + B_TQ
    for win, dil in C_GROUPS:
        assert win == TK * dil
    bias_a = _toeplitz_bias(rel_bias[:, :A_HEADS], seq, TK, keys_on_rows=True, scale=LOG2E)
    bias_b = _toeplitz_bias(rel_bias[:, A_HEADS:A_HEADS + B_HEADS], seq, TK, keys_on_rows=True, scale=LOG2E)
    rel_c = rel_bias[:, A_HEADS + B_HEADS:]
    bias_c = [_dilated_bias(rel_c[:, g * C_HEADS_PER_GROUP:(g + 1) * C_HEADS_PER_GROUP], dil)
              for g, (_, dil) in enumerate(C_GROUPS)]
    x2d = x.reshape(bsz * seq, d)
    for layer in range(norm1_g.shape[0]):
        x2d = _layer(x2d, bsz, seq, norm1_g[layer], norm2_g[layer], w_in[layer], qk_norm_g[layer],
                     nsa_cmp_pos[layer], nsa_cmp_w[layer], w_branch_a[layer], w_branch_b[layer],
                     w_branch_c[layer], w_out[layer], w_ffn_in[layer], w_ffn_out[layer],
                     bias_a, bias_b, bias_c)
    return x2d.reshape(bsz, seq, d)
```

```python
import functools
import math

import numpy as np
import jax
import jax.numpy as jnp
from jax import lax
from jax.experimental import pallas as pl
from jax.experimental.pallas import tpu as pltpu

F32 = jnp.float32
BF16 = jnp.bfloat16
I32 = jnp.int32

D_MODEL = 1024
HEAD_DIM = 64
NORM_EPS = 1e-6
REL_BUCKETS = 32
REL_MAX_DIST = 2048

A_HEADS = 8
A_IDX_HEADS = 4
A_TOPK_MAX = 256
B_HEADS = 8
B_KV_HEADS = 2
B_GROUP = B_HEADS // B_KV_HEADS
B_CMP_LEN = 32
B_CMP_STRIDE = 16
B_SEL_LEN = 64
B_SEL_TOPK_MAX = 16
B_WINDOW = 512
C_GROUPS = ((128, 1), (512, 4), (2048, 16))
C_HEADS_PER_GROUP = 4
C_HEADS = C_HEADS_PER_GROUP * len(C_GROUPS)
D_FF = ((8 * D_MODEL + 3 * 256 - 1) // (3 * 256)) * 256

_O_AQ = 0
_O_AK = _O_AQ + A_HEADS * HEAD_DIM
_O_AV = _O_AK + HEAD_DIM
_O_IQ = _O_AV + HEAD_DIM
_O_IK = _O_IQ + A_IDX_HEADS * HEAD_DIM
_O_IW = _O_IK + HEAD_DIM
_O_BQ = _O_IW + A_IDX_HEADS
_O_BKV = _O_BQ + B_HEADS * HEAD_DIM
_O_BG = _O_BKV + 6 * B_KV_HEADS * HEAD_DIM
_O_CQ = _O_BG + 3 * B_HEADS
_O_CK = _O_CQ + C_HEADS * HEAD_DIM
_O_CV = _O_CK + C_HEADS * HEAD_DIM
_O_MIX = _O_CV + C_HEADS * HEAD_DIM
_O_END = _O_MIX + 3 * D_MODEL

TQ = 128
TK = 128
KB_TILES = 4
A_TQ = 512
A_PAIR = 2
CLASS_TILE = 512
C_TILES_PER_STEP = 4
B_TQ = 256
B_PAIR = 2
LOG2E = 1.4426950408889634
LANES = 128
VMEM_LIMIT = 56 * 1024 * 1024
INT_MIN = -2 ** 31
NEG_INIT = -1e30
SCALE = HEAD_DIM ** -0.5


def _params(sem):
    return pltpu.CompilerParams(dimension_semantics=sem, vmem_limit_bytes=VMEM_LIMIT)


def _dot_t(a, b):
    return lax.dot_general(a, b, (((1,), (1,)), ((), ())), preferred_element_type=F32)


def _dot(a, b):
    return jnp.dot(a, b, preferred_element_type=F32)


def _split_dot_left(a_bf16, b):
    hi = b.astype(BF16)
    lo = (b - hi.astype(F32)).astype(BF16)
    return _dot(a_bf16, hi) + _dot(a_bf16, lo)


def _split_dot(a, b_bf16):
    hi = a.astype(BF16)
    lo = (a - hi.astype(F32)).astype(BF16)
    return _dot(hi, b_bf16) + _dot(lo, b_bf16)


def _bucket_table(n_max):
    n = np.arange(n_max, dtype=np.int64)
    exact = REL_BUCKETS // 2
    nf = np.maximum(n, 1).astype(np.float32)
    large = exact + (np.log(nf / np.float32(exact)) / np.float32(math.log(REL_MAX_DIST / exact))
                     * np.float32(REL_BUCKETS - exact)).astype(np.int32)
    return np.where(n < exact, n, np.minimum(large, REL_BUCKETS - 1)).astype(np.int32)


def _num_bias_tiles(seq, tq):
    bucket = _bucket_table(seq + tq)
    first_sat = int(np.min(np.nonzero(bucket == REL_BUCKETS - 1)[0]))
    assert np.all(bucket[first_sat:] == REL_BUCKETS - 1)
    nd = -(-(first_sat + TK - 1) // TK) + tq // TK
    return min(nd, seq // TK)


def _slot_bias(bias_ref, nheads, pair, rq, nd, i, p, j):
    parts = []
    for u in range(pair):
        for sub in range(rq):
            k = jnp.clip(rq * i + sub - j, 0, nd - 1)
            parts.append(bias_ref[k * nheads + p * pair + u])
    return jnp.concatenate(parts, axis=1)


def _bias_kernel(nh, group, scale, idx_ref, rel_ref, o_ref):
    idx = idx_ref[0]
    c = idx.shape[1]
    acc = [jnp.zeros(idx.shape, F32) for _ in range(nh)]
    for b in range(REL_BUCKETS):
        hit = idx == b
        for h in range(nh):
            acc[h] = jnp.where(hit, rel_ref[b, h] * scale, acc[h])
    for h in range(nh):
        o_ref[h // group, :, (h % group) * c:(h % group + 1) * c] = acc[h]


def _bias_tiles(rel_cols, idx, group=1, scale=1.0):
    n, r, c = idx.shape
    nh = rel_cols.shape[1]
    return pl.pallas_call(
        functools.partial(_bias_kernel, nh, group, scale),
        grid=(n,),
        in_specs=[pl.BlockSpec((1, r, c), lambda k: (k, 0, 0)),
                  pl.BlockSpec(memory_space=pltpu.SMEM)],
        out_specs=pl.BlockSpec((nh // group, r, group * c), lambda k: (k, 0, 0)),
        out_shape=jax.ShapeDtypeStruct((n * nh // group, r, group * c), F32),
        compiler_params=_params(("parallel",)),
        name="bias_tiles",
    )(jnp.asarray(idx, I32), rel_cols.astype(F32))


def _toeplitz_bias(rel_cols, seq, tq=TQ, keys_on_rows=False, group=1, scale=1.0):
    nd = _num_bias_tiles(seq, tq)
    bucket = _bucket_table(seq + tq)
    d = ((np.arange(nd)[:, None, None] - (tq // TK - 1)) * TK
         + np.arange(tq)[None, :, None] - np.arange(TK)[None, None, :])
    idx = bucket[np.clip(d, 0, None)]
    return _bias_tiles(rel_cols, idx.transpose(0, 2, 1) if keys_on_rows else idx, group, scale), nd


def _dilated_bias(rel_cols, dil):
    bucket = _bucket_table(2 * TK * dil + 1)
    du = np.arange(TQ)[None, :] + TK - np.arange(2 * TK)[:, None]
    return _bias_tiles(rel_cols, bucket[np.clip(du, 0, None) * dil][None])


def _proj_kernel(kinds, *refs):
    n = len(kinds)
    x_ref, g_ref, gs_ref = refs[0], refs[1], refs[2]
    w_refs = refs[3:3 + 2 * n:2]
    aux_refs = refs[4:4 + 2 * n:2]
    n_out = sum(2 if isinstance(k, tuple) and k[0] == "trans" else 1 for k in kinds)
    out_refs = iter(refs[3 + 2 * n:3 + 2 * n + n_out])
    scratch_refs = refs[3 + 2 * n + n_out:]
    x = x_ref[...]
    ms = jnp.mean(x * x, axis=-1, keepdims=True)
    h = (x * lax.rsqrt(ms + NORM_EPS) * g_ref[...]).astype(BF16)
    for kind, w_ref, aux_ref in zip(kinds, w_refs, aux_refs):
        if isinstance(kind, tuple) and kind[0] == "trans":
            _, nv, nextra, act = kind
            vt_ref, ex_ref = next(out_refs), next(out_refs)
            yt = _dot_t(w_ref[...], h)
            rows = lax.broadcasted_iota(I32, (nv * LANES, yt.shape[1]), 0)
            vt = (yt[0:nv * LANES] + jnp.where(rows % LANES >= HEAD_DIM, 1.0, 0.0)).astype(vt_ref.dtype)
            for c in range(vt_ref.shape[0]):
                vt_ref[c] = vt[:, c * TK:(c + 1) * TK]
            ex = yt[nv * LANES:nv * LANES + nextra]
            ex_ref[...] = 1.0 / (1.0 + jnp.exp(-ex)) if act == "sigmoid" else ex
            continue
        o_ref = next(out_refs)
        classes = kind[1] if isinstance(kind, tuple) and kind[0] == "classes" else 0
        chunk = kind[1] if isinstance(kind, tuple) and kind[0] == "chunks" else 0
        width = w_ref.shape[1]
        cw = 256 if width % 256 == 0 else LANES
        ys = [_dot(h, w_ref[:, c0:c0 + cw]) for c0 in range(0, width, cw)]
        for c0, y in zip(range(0, width, cw), ys):
            if kind == "norm" or classes:
                gsum = _split_dot(y * y, gs_ref[:cw, :cw])
                r = lax.rsqrt(gsum * (1.0 / HEAD_DIM) + NORM_EPS)
                mask = aux_ref[0:1, c0:c0 + cw]
                fac = mask * (r * aux_ref[1:2, c0:c0 + cw]) + (1.0 - mask)
                y = y * fac + aux_ref[2:3, c0:c0 + cw]
            elif kind == "sigmoid":
                y = 1.0 / (1.0 + jnp.exp(-y))
            if classes or chunk:
                for c in range(0, cw, LANES):
                    scratch_refs[0][(c0 + c) // LANES] = y[:, c:c + LANES]
            else:
                o_ref[:, c0:c0 + cw] = y.astype(o_ref.dtype)
        for r in range(classes):
            rows = pl.ds(r, x.shape[0] // classes, stride=classes)
            pw = o_ref.shape[-1] // LANES
            for part in range(o_ref.shape[1]):
                o_ref[r, part] = jnp.concatenate([scratch_refs[0][part * pw + c, rows, :] for c in range(pw)],
                                                 axis=1).astype(o_ref.dtype)
        for r in range(chunk):
            rows = pl.ds(r, x.shape[0] // chunk, stride=chunk)
            for c in range(width // LANES):
                o_ref[:, r * width + c * LANES:r * width + (c + 1) * LANES] = scratch_refs[0][c, rows, :]


def _proj(x2d, gain, pieces, tm):
    m, d = x2d.shape
    kinds = tuple(p[0] for p in pieces)
    gs = (np.arange(256)[:, None] // HEAD_DIM == np.arange(256)[None, :] // HEAD_DIM)
    gs = jnp.asarray(gs, BF16)
    in_specs = [pl.BlockSpec((tm, d), lambda i: (i, 0)),
                pl.BlockSpec((1, d), lambda i: (0, 0)),
                pl.BlockSpec((256, 256), lambda i: (0, 0))]
    args = [x2d, gain.reshape(1, d).astype(F32), gs]
    out_specs, out_shapes, scratch = [], [], []
    for kind, w, aux, dt in pieces:
        in_specs += [pl.BlockSpec(w.shape, lambda i: (0, 0)), pl.BlockSpec(aux.shape, lambda i: (0, 0))]
        args += [w, aux]
        if isinstance(kind, tuple) and kind[0] == "trans":
            _, nv, nextra, _ = kind
            out_specs += [pl.BlockSpec((tm // TK, nv * LANES, TK), lambda i: (i, 0, 0)),
                          pl.BlockSpec((nextra, tm), lambda i: (0, i))]
            out_shapes += [jax.ShapeDtypeStruct((m // TK, nv * LANES, TK), dt),
                           jax.ShapeDtypeStruct((nextra, m), F32)]
            continue
        nw = w.shape[1]
        if isinstance(kind, tuple) and kind[0] == "chunks":
            n = kind[1]
            out_specs.append(pl.BlockSpec((tm // n, n * nw), lambda i: (i, 0)))
            out_shapes.append(jax.ShapeDtypeStruct((m // n, n * nw), dt))
            scratch = [pltpu.VMEM((nw // LANES, tm, LANES), F32)]
            continue
        if isinstance(kind, tuple) and kind[0] == "classes":
            dil = kind[1]
            assert tm == CLASS_TILE
            gw = C_HEADS_PER_GROUP * HEAD_DIM
            shape = (dil, nw // gw, tm // dil, gw)
            out_specs.append(pl.BlockSpec((None,) + shape, lambda i: (i, 0, 0, 0, 0)))
            out_shapes.append(jax.ShapeDtypeStruct((m // tm,) + shape, dt))
            scratch = [pltpu.VMEM((nw // LANES, tm, LANES), F32)]
            continue
        out_specs.append(pl.BlockSpec((tm, nw), lambda i: (i, 0)))
        out_shapes.append(jax.ShapeDtypeStruct((m, nw), dt))
    return pl.pallas_call(
        functools.partial(_proj_kernel, kinds),
        grid=(m // tm,),
        in_specs=in_specs, out_specs=out_specs, out_shape=out_shapes, scratch_shapes=scratch,
        compiler_params=_params(("parallel",)),
        name="proj",
    )(*args)


def _aux(width, mask=None, gain=None, add=None):
    z = jnp.zeros((width,), F32)
    return jnp.stack([z if mask is None else mask, z if gain is None else gain, z if add is None else add])


def _seg(*parts):
    ref = next(p for p in parts if not isinstance(p, int))
    return jnp.concatenate([jnp.zeros(ref.shape[:-1] + (p,), ref.dtype) if isinstance(p, int) else p
                            for p in parts], axis=-1)


def _proj_pieces(w_in, qk):
    w = w_in.astype(BF16)
    hd = HEAD_DIM
    ones, zeros = jnp.ones((hd,), F32), jnp.zeros((hd,), F32)
    cat = jnp.concatenate

    def cols(a, b):
        return w[:, a:b]

    pieces_ab = [
        ("norm", cols(_O_AQ, _O_AK), _aux(512, jnp.ones((512,), F32), jnp.tile(qk[0], A_HEADS) * (SCALE * LOG2E)),
         BF16),
        ("norm", _seg(cols(_O_AK, _O_AV), hd), _aux(LANES, cat([ones, zeros]), cat([qk[1], zeros])), BF16),
        (("trans", 1, 8, "none"), _seg(cols(_O_AV, _O_IQ), hd, cols(_O_IW, _O_BQ), 16 - A_IDX_HEADS).T,
         _aux(LANES), BF16),
        ("plain", cols(_O_IQ, _O_IK), _aux(256), BF16),
        ("plain", _seg(cols(_O_IK, _O_IW), LANES - hd), _aux(LANES), BF16),
        ("norm", cols(_O_BQ, _O_BKV), _aux(512, jnp.ones((512,), F32), jnp.tile(qk[2], B_HEADS) * (SCALE * LOG2E)),
         BF16),
        (("chunks", B_CMP_STRIDE), cols(_O_BKV, _O_BKV + 256), _aux(256), F32),
    ]
    o = _O_BKV + 256
    ks0, ks1, vs0, vs1, kw0, kw1, vw0, vw1 = [cols(o + i * hd, o + (i + 1) * hd) for i in range(8)]
    pieces_ab += [
        ("norm", _seg(ks0, hd, ks1, hd, kw0, hd, kw1, hd),
         _aux(512, cat([ones, zeros] * 4), cat([qk[3], zeros] * 4)), BF16),
        (("trans", 4, 32, "sigmoid"),
         _seg(vs0, hd, vs1, hd, vw0, hd, vw1, hd, cols(_O_BG, _O_CQ), 32 - 3 * B_HEADS).T, _aux(LANES), BF16),
    ]
    gw = C_HEADS_PER_GROUP * hd
    pieces_c = []
    for g, (_, dil) in enumerate(C_GROUPS):
        wg = cat([cols(o0 + g * gw, o0 + (g + 1) * gw) for o0 in (_O_CQ, _O_CK, _O_CV)], axis=1)
        mask = cat([jnp.ones((2 * gw,), F32), jnp.zeros((gw,), F32)])
        gain = cat([jnp.tile(qk[4], C_HEADS_PER_GROUP) * SCALE, jnp.tile(qk[5], C_HEADS_PER_GROUP),
                    jnp.zeros((gw,), F32)])
        pieces_c.append((("classes", dil), wg, _aux(3 * gw, mask, gain), BF16))
    pieces_g = [("sigmoid", cols(_O_MIX, _O_END), _aux(3 * D_MODEL), BF16)]
    return pieces_ab, pieces_c, pieces_g


def _flash_init(m_ref, acc_ref):
    m_ref[...] = jnp.full(m_ref.shape, NEG_INIT, F32)
    acc_ref[...] = jnp.zeros(acc_ref.shape, F32)


def _flash_block_t(q_ref, slots, group_of, kt, v_aug_t, bias_fn, masks, m_ref, acc_ref):
    slots = list(slots)
    s_all = [_dot_t(kt[group_of(p)], q_ref[p]) for p in slots]
    m_old = [m_ref[p][0:1] for p in slots]
    ps, alphas = [], []
    for k, p in enumerate(slots):
        s, mk = s_all[k], masks[group_of(p)]
        sc = [jnp.where(mk[c], s[c * TK:(c + 1) * TK] + bias_fn(p, c), -jnp.inf).astype(BF16)
              for c in range(len(mk))]
        m_blk = jnp.max(functools.reduce(jnp.maximum, sc).astype(F32), axis=0, keepdims=True)
        m_new = jnp.maximum(m_old[k], m_blk)
        m16 = m_new.astype(BF16)
        ps.append(jnp.concatenate([jnp.exp2(x - m16) for x in sc], axis=0))
        alphas.append(jnp.exp2(m_old[k] - m_new))
        m_ref[p] = jnp.broadcast_to(m_new, m_ref.shape[1:])
    for k, p in enumerate(slots):
        acc_ref[p] = alphas[k] * acc_ref[p] + _dot(v_aug_t[group_of(p)], ps[k])


def _dsa_kernel(topk, nd, pbits, iq_ref, iwt_ref, ik_ref, aq_ref, ak_ref, avt_ref, bias_ref, o_ref,
                keys_ref, half_ref, iqs_ref, qs_ref, m_ref, acc_ref):
    TQ = A_TQ
    i = pl.program_id(1)
    nk = (i + 1) * (TQ // TK)
    nkb = (nk + KB_TILES - 1) // KB_TILES
    kb = KB_TILES * TK
    krow = lax.broadcasted_iota(I32, (TK, TQ), 0)
    qpos = i * TQ + lax.broadcasted_iota(I32, (TK, TQ), 1)
    i16 = jnp.int16
    last = keys_ref.shape[0] * TK - 1

    iwt = iwt_ref[...]
    for h in range(A_IDX_HEADS):
        iqs_ref[h * TQ:(h + 1) * TQ] = iq_ref[:, h * HEAD_DIM:(h + 1) * HEAD_DIM]

    def score_block(jb, c):
        r0 = pl.multiple_of(jb * kb, kb)
        d = _dot_t(ik_ref[pl.ds(r0, kb), 0:HEAD_DIM], iqs_ref[...])
        acc = jnp.zeros((kb, TQ), F32)
        for h in range(A_IDX_HEADS):
            acc = acc + jnp.maximum(d[:, h * TQ:(h + 1) * TQ], 0.0) * iwt[h:h + 1]
        bits = lax.bitcast_convert_type(acc, I32)
        key = jnp.where(bits < 0, bits ^ 0x7FFFFFFF, bits + (last + 1))
        for t in range(KB_TILES):
            j = jb * KB_TILES + t
            kidx = j * TK + krow
            kj = jnp.where(acc[t * TK:(t + 1) * TK] == 0.0, last - kidx, key[t * TK:(t + 1) * TK])
            kj = jnp.where(kidx <= qpos, kj, INT_MIN)
            keys_ref[j] = kj
            half_ref[j] = jnp.right_shift(kj, 16).astype(i16)
        return c

    lax.fori_loop(0, nkb, score_block, 0)

    kf = float(topk)

    nb_max = keys_ref.shape[0] // KB_TILES
    sub = 16

    def search_half(nbits, u0):
        def run(nblk, u_init):
            def step(b, u):
                cand = u | jnp.left_shift(jnp.int32(1), nbits - 1 - b)
                cb = jnp.broadcast_to((cand - 32768).astype(i16), (TK, TQ))
                acc = jnp.zeros((sub, TQ), i16)
                for j in range(nblk * KB_TILES):
                    hit = jnp.where(half_ref[j] >= cb, jnp.ones((), i16), jnp.zeros((), i16))
                    acc = acc + functools.reduce(lambda a, b2: a + b2,
                                                 [hit[r:r + sub] for r in range(0, TK, sub)])
                cnt = jnp.sum(acc.astype(F32), axis=0, keepdims=True)
                return jnp.where(cnt >= kf, cand, u)

            return lax.fori_loop(0, nbits, step, u_init)

        return lax.switch(nkb - 1, [functools.partial(run, n) for n in range(1, nb_max + 1)], u0)

    def for_tiles(fn):
        def body(jb, c):
            for t in range(KB_TILES):
                fn(jb * KB_TILES + t)
            return c
        lax.fori_loop(0, nkb, body, 0)

    zero = jnp.zeros((1, TQ), I32)
    t_hi = search_half(16, zero) - 32768

    def low_tile(j):
        k = keys_ref[j]
        hi = jnp.right_shift(k, 16)
        lo = (k & 0xFFFF) - 32768
        half_ref[j] = jnp.where(hi > t_hi, 32767, jnp.where(hi < t_hi, -32768, lo)).astype(i16)

    for_tiles(low_tile)
    thr = t_hi * 65536 + search_half(16, zero)

    def count_ge(jb, acc):
        for t in range(KB_TILES):
            acc = acc + jnp.where(keys_ref[jb * KB_TILES + t] >= thr, 1.0, 0.0)
        return acc

    n_ge = jnp.sum(lax.fori_loop(0, nkb, count_ge, jnp.zeros((TK, TQ), F32)), axis=0, keepdims=True)

    @pl.when(jnp.max(jnp.where(thr > INT_MIN, n_ge, 0.0)) > kf)
    def _():
        def tie_tile(j):
            k = keys_ref[j]
            rev = last - (j * TK + krow)
            half_ref[j] = jnp.where(k > thr, 32767, jnp.where(k == thr, rev, -32768)).astype(i16)

        for_tiles(tie_tile)
        keep = search_half(pbits, zero + 32768) - 32768

        def demote(j):
            k = keys_ref[j]
            rev = last - (j * TK + krow)
            keys_ref[j] = jnp.where((k == thr) & (rev < keep) & (thr > INT_MIN), k - 1, k)

        for_tiles(demote)

    sel_thr = jnp.maximum(thr, INT_MIN + 1)

    nslot = A_HEADS // A_PAIR
    for h in range(A_HEADS):
        qs_ref[h // A_PAIR, (h % A_PAIR) * TQ:(h % A_PAIR + 1) * TQ] = aq_ref[:, h * HEAD_DIM:(h + 1) * HEAD_DIM]
    _flash_init(m_ref, acc_ref)

    def att_block(jb, c):
        r0 = pl.multiple_of(jb * kb, kb)
        kt = ak_ref[pl.ds(r0, kb), 0:HEAD_DIM]
        j0 = jb * KB_TILES
        masks = []
        for t in range(KB_TILES):
            mk = keys_ref[j0 + t] >= sel_thr
            masks.append(jnp.concatenate([mk] * A_PAIR, axis=1))
        vat = jnp.concatenate([avt_ref[j0 + t] for t in range(KB_TILES)], axis=1)
        _flash_block_t(qs_ref, range(nslot), lambda p: 0, [kt], [vat],
                       lambda p, t: _slot_bias(bias_ref, A_HEADS, A_PAIR, TQ // TK, nd, i, p, j0 + t), [masks],
                       m_ref, acc_ref)
        return c

    lax.fori_loop(0, nkb, att_block, 0)
    outs = []
    for h in range(A_HEADS):
        acc = acc_ref[h // A_PAIR][:, (h % A_PAIR) * TQ:(h % A_PAIR + 1) * TQ]
        outs.append(acc[:HEAD_DIM] / jnp.maximum(acc[HEAD_DIM:HEAD_DIM + 1], 1e-30))
    o_ref[...] = jnp.concatenate(outs, axis=0).T.astype(o_ref.dtype)


def _dsa(iq, iwt, ik, aq, ak, avt, bias, nd, bsz, seq):
    TQ = A_TQ
    nq = seq // TQ
    kb = KB_TILES * TK
    topk = min(A_TOPK_MAX, seq // 4)
    pbits = max(1, (seq - 1).bit_length())
    return pl.pallas_call(
        functools.partial(_dsa_kernel, topk, nd, pbits),
        grid=(bsz, nq),
        in_specs=[
            pl.BlockSpec((TQ, 256), lambda b, i: (b * nq + i, 0)),
            pl.BlockSpec((8, TQ), lambda b, i: (0, b * nq + i)),
            pl.BlockSpec((seq, LANES), lambda b, i: (b, 0)),
            pl.BlockSpec((TQ, 512), lambda b, i: (b * nq + i, 0)),
            pl.BlockSpec((seq, LANES), lambda b, i: (b, 0)),
            pl.BlockSpec((seq // TK, LANES, TK), lambda b, i: (b, 0, 0)),
            pl.BlockSpec(bias.shape, lambda b, i: (0, 0, 0), pipeline_mode=pl.Buffered(1)),
        ],
        out_specs=pl.BlockSpec((TQ, 512), lambda b, i: (b * nq + i, 0)),
        out_shape=jax.ShapeDtypeStruct((bsz * seq, 512), BF16),
        scratch_shapes=[pltpu.VMEM((seq // TK, TK, TQ), I32),
                        pltpu.VMEM((seq // TK, TK, TQ), jnp.int16),
                        pltpu.VMEM((A_IDX_HEADS * TQ, HEAD_DIM), BF16),
                        pltpu.VMEM((A_HEADS // A_PAIR, A_PAIR * TQ, HEAD_DIM), BF16),
                        pltpu.VMEM((A_HEADS // A_PAIR, 8, A_PAIR * TQ), F32),
                        pltpu.VMEM((A_HEADS // A_PAIR, LANES, A_PAIR * TQ), F32)],
        compiler_params=_params(("parallel", "arbitrary")),
        name="dsa",
    )(iq, iwt, ik, aq, ak, avt, bias)


def _nsa_cmp_kernel(x_ref, pos_ref, wlo_ref, whi_ref, gain_ref, k_ref, v_ref):
    x = x_ref[...]
    lo = _dot((x + pos_ref[0:1, :]).astype(BF16), wlo_ref[...])
    hi = _dot((x + pos_ref[1:2, :]).astype(BF16), whi_ref[...])
    nrow = x.shape[0]
    pre = lo + pltpu.roll(hi, nrow - 1, 0)
    ks = []
    for g in range(B_KV_HEADS):
        kg = pre[:, g * HEAD_DIM:(g + 1) * HEAD_DIM]
        ms = jnp.mean(kg * kg, axis=-1, keepdims=True)
        ks.append(kg * lax.rsqrt(ms + NORM_EPS) * gain_ref[...])
    k_ref[...] = jnp.concatenate(ks, axis=1).astype(k_ref.dtype)
    v_ref[...] = pre[:, LANES:2 * LANES].T.astype(v_ref.dtype)


def _nsa_cmp(bcmp, cmp_pos, cmp_w, k_gain, bsz, seq):
    nch = seq // B_CMP_STRIDE
    half = B_CMP_LEN // 2
    width = half * 256
    x = bcmp

    def wmat(l0):
        w = jnp.zeros((half, 4, HEAD_DIM, 4, HEAD_DIM), F32)
        for j in range(4):
            w = w.at[:, j, :, j, :].set(cmp_w[j // 2, l0:l0 + half])
        return w.reshape(width, 256).astype(BF16)

    def prow(l0):
        p = jnp.stack([cmp_pos[0, l0:l0 + half], cmp_pos[0, l0:l0 + half],
                       cmp_pos[1, l0:l0 + half], cmp_pos[1, l0:l0 + half]], axis=1)
        return p.reshape(width)

    pos = jnp.stack([prow(0), prow(half)]).astype(F32)
    return pl.pallas_call(
        _nsa_cmp_kernel,
        grid=(bsz,),
        in_specs=[
            pl.BlockSpec((nch, width), lambda b: (b, 0)),
            pl.BlockSpec((2, width), lambda b: (0, 0)),
            pl.BlockSpec((width, 256), lambda b: (0, 0)),
            pl.BlockSpec((width, 256), lambda b: (0, 0)),
            pl.BlockSpec((1, HEAD_DIM), lambda b: (0, 0)),
        ],
        out_specs=[pl.BlockSpec((nch, LANES), lambda b: (b, 0)), pl.BlockSpec((LANES, nch), lambda b: (b, 0))],
        out_shape=[jax.ShapeDtypeStruct((bsz * nch, LANES), BF16), jax.ShapeDtypeStruct((bsz * LANES, nch), BF16)],
        compiler_params=_params(("parallel",)),
        name="nsa_cmp",
    )(x, pos, wmat(0), wmat(half), k_gain.reshape(1, HEAD_DIM).astype(F32))


def _nsa_kernel(seq, nd, q_ref, gt_ref, kc_ref, vct_ref, k_ref, vt_ref, bias_ref, o_ref,
                imp_ref, sel_ref, qs_ref, m_ref, acc_ref):
    TQ = B_TQ
    rq = TQ // TK
    i = pl.program_id(1)
    ncp = seq // B_CMP_STRIDE
    ns = seq // B_SEL_LEN
    n_top = min(B_SEL_TOPK_MAX, ns)
    nslot = B_HEADS // B_PAIR
    spg = B_GROUP // B_PAIR
    krow = lax.broadcasted_iota(I32, (TK, TQ), 0)
    qpos = i * TQ + lax.broadcasted_iota(I32, (TK, TQ), 1)

    def lanes(h):
        return slice((h % B_PAIR) * TQ, (h % B_PAIR + 1) * TQ)

    def dup(x):
        return jnp.concatenate([x] * B_PAIR, axis=1)

    for h in range(B_HEADS):
        qs_ref[h // B_PAIR, lanes(h)] = q_ref[:, h * HEAD_DIM:(h + 1) * HEAD_DIM]

    n_idx = lax.broadcasted_iota(I32, (ncp, TQ), 0)
    t_c = i * TQ + lax.broadcasted_iota(I32, (ncp, TQ), 1)
    cmask = dup(n_idx * B_CMP_STRIDE + (B_CMP_LEN - 1) <= t_c)
    om = lax.broadcasted_iota(I32, (ns, ncp), 0) * B_SEL_LEN
    on = lax.broadcasted_iota(I32, (ns, ncp), 1) * B_CMP_STRIDE
    ovt = jnp.where((on < om + B_SEL_LEN) & (on + B_CMP_LEN > om), 1.0, 0.0).astype(BF16)
    m_idx = lax.broadcasted_iota(I32, (ns, TQ), 0)
    jt = (i * TQ + lax.broadcasted_iota(I32, (ns, TQ), 1)) // B_SEL_LEN
    forced = (m_idx == 0) | (m_idx == jt) | (m_idx == jt - 1)

    kc = [kc_ref[:, g * HEAD_DIM:(g + 1) * HEAD_DIM] for g in range(B_KV_HEADS)]
    vct = [vct_ref[g * HEAD_DIM:(g + 1) * HEAD_DIM, :] for g in range(B_KV_HEADS)]
    st_all = [_dot_t(kc[p // spg], qs_ref[p]) for p in range(nslot)]
    pts = []
    for p in range(nslot):
        st = jnp.where(cmask, st_all[p], -jnp.inf)
        mx = jnp.max(st, axis=0, keepdims=True)
        mx = jnp.where(mx == -jnp.inf, 0.0, mx)
        e = jnp.exp2(st - mx)
        pts.append(e / jnp.maximum(jnp.sum(e, axis=0, keepdims=True), 1e-30))
    oc = [_dot(vct[p // spg], pts[p].astype(BF16)) for p in range(nslot)]

    for g in range(B_KV_HEADS):
        psum = functools.reduce(lambda a, b: a + b, [pts[h // B_PAIR][:, lanes(h)]
                                                     for h in range(g * B_GROUP, (g + 1) * B_GROUP)])
        imp = _split_dot_left(ovt, psum)
        imp = jnp.where(forced, jnp.inf, jnp.where(m_idx <= jt, imp, -jnp.inf))

        sub = 8
        imp_ref[g] = imp
        grp = [imp[r:r + sub] for r in range(0, ns, sub)]
        rank = [jnp.zeros((sub, TQ), F32) for _ in grp]
        rsub = lax.broadcasted_iota(I32, (sub, TQ), 0)
        for mp in range(ns):
            vp = jnp.broadcast_to(imp_ref[g, mp:mp + 1, :], (sub, TQ))
            for r in range(len(grp)):
                if r * sub + sub - 1 < mp:
                    before = vp > grp[r]
                elif r * sub > mp:
                    before = vp >= grp[r]
                else:
                    before = (vp > grp[r]) | ((vp == grp[r]) & (rsub + r * sub > mp))
                rank[r] = rank[r] + jnp.where(before, 1.0, 0.0)
        rank = jnp.concatenate(rank, axis=0)
        sel_ref[g] = jnp.where((rank < float(n_top)) & (m_idx <= jt), 1.0, 0.0)

    def slot_bias(p, j):
        return _slot_bias(bias_ref, B_HEADS, B_PAIR, rq, nd, i, p, j)

    def branch_block(j0, ntiles, koff, vrow, mask_fn):
        rows = pl.ds(pl.multiple_of(j0 * TK, TK), ntiles * TK)
        kt = [k_ref[rows, koff + g * LANES:koff + g * LANES + HEAD_DIM] for g in range(B_KV_HEADS)]
        vt = [jnp.concatenate([vt_ref[j0 + t, vrow + g * LANES:vrow + (g + 1) * LANES, :] for t in range(ntiles)],
                              axis=1) for g in range(B_KV_HEADS)]
        masks = [[dup(mask_fn(g, j0 + t)) for t in range(ntiles)] for g in range(B_KV_HEADS)]
        _flash_block_t(qs_ref, range(nslot), lambda p: p // spg, kt, vt, lambda p, t: slot_bias(p, j0 + t),
                       masks, m_ref, acc_ref)

    bpt = TK // B_SEL_LEN

    def sel_mask(g, j):
        chosen = jnp.concatenate([jnp.broadcast_to(sel_ref[g, pl.ds(j * bpt + b, 1), :], (B_SEL_LEN, TQ))
                                  for b in range(bpt)], axis=0) > 0.5
        return chosen & (j * TK + krow <= qpos)

    def win_mask(g, j):
        dist = qpos - (j * TK + krow)
        return (dist >= 0) & (dist < B_WINDOW)

    def sel_body(jb, c):
        branch_block(jb * KB_TILES, KB_TILES, 0, 0, sel_mask)
        return c

    gt = gt_ref[...]

    def gate(h, br):
        return gt[3 * h + br:3 * h + br + 1]

    def head_out(h):
        a = acc_ref[h // B_PAIR][:, lanes(h)]
        return a[:HEAD_DIM] / jnp.maximum(a[HEAD_DIM:HEAD_DIM + 1], 1e-30)

    _flash_init(m_ref, acc_ref)
    lax.fori_loop(0, (rq * (i + 1) + KB_TILES - 1) // KB_TILES, sel_body, 0)
    part = [gate(h, 0) * oc[h // B_PAIR][:, lanes(h)] + gate(h, 1) * head_out(h) for h in range(B_HEADS)]

    wt = B_WINDOW // TK + rq
    _flash_init(m_ref, acc_ref)
    branch_block(jnp.maximum(rq * (i + 1) - wt, 0), wt, 2 * LANES, 2 * LANES, win_mask)
    outs = [part[h] + gate(h, 2) * head_out(h) for h in range(B_HEADS)]
    o_ref[...] = jnp.concatenate(outs, axis=0).T.astype(o_ref.dtype)


def _nsa(bq, bgt, kcmp, vcmpt, bk, bvt, bias, nd, bsz, seq):
    TQ = B_TQ
    nq = seq // TQ
    ncp = seq // B_CMP_STRIDE
    ns = seq // B_SEL_LEN
    nslot, w = B_HEADS // B_PAIR, B_PAIR * TQ
    return pl.pallas_call(
        functools.partial(_nsa_kernel, seq, nd),
        grid=(bsz, nq),
        in_specs=[
            pl.BlockSpec((TQ, 512), lambda b, i: (b * nq + i, 0)),
            pl.BlockSpec((32, TQ), lambda b, i: (0, b * nq + i)),
            pl.BlockSpec((ncp, LANES), lambda b, i: (b, 0)),
            pl.BlockSpec((LANES, ncp), lambda b, i: (b, 0)),
            pl.BlockSpec((seq, 512), lambda b, i: (b, 0)),
            pl.BlockSpec((seq // TK, 4 * LANES, TK), lambda b, i: (b, 0, 0)),
            pl.BlockSpec(bias.shape, lambda b, i: (0, 0, 0), pipeline_mode=pl.Buffered(1)),
        ],
        out_specs=pl.BlockSpec((TQ, 512), lambda b, i: (b * nq + i, 0)),
        out_shape=jax.ShapeDtypeStruct((bsz * seq, 512), BF16),
        scratch_shapes=[pltpu.VMEM((B_KV_HEADS, ns, TQ), F32), pltpu.VMEM((B_KV_HEADS, ns, TQ), F32),
                        pltpu.VMEM((nslot, w, HEAD_DIM), BF16),
                        pltpu.VMEM((nslot, 8, w), F32),
                        pltpu.VMEM((nslot, LANES, w), F32)],
        compiler_params=_params(("parallel", "arbitrary")),
        name="nsa",
    )(bq, bgt, kcmp, vcmpt, bk, bvt, bias)


def _dil_kernel(q_ref, kp_ref, kc_ref, vp_ref, vc_ref, bias_ref, o_ref, lse_ref):
    i = pl.program_id(2)

    def rows(ref):
        return ref[...].reshape(-1, ref.shape[-1])

    q = rows(q_ref)
    nt = q.shape[0] // TQ
    k = jnp.concatenate([rows(kp_ref), rows(kc_ref)], axis=0)
    v = jnp.concatenate([rows(vp_ref), rows(vc_ref)], axis=0)
    krow = lax.broadcasted_iota(I32, (2 * TK, TQ), 0)
    qcol = lax.broadcasted_iota(I32, (2 * TK, TQ), 1)
    du = qcol + TK - krow
    window = (du >= 0) & (du <= TK)
    units = [(u, hh) for u in range(nt) for hh in range(C_HEADS_PER_GROUP)]

    def head(x, u, n, hh):
        return x[u * TQ:(u + n) * TQ, hh * HEAD_DIM:(hh + 1) * HEAD_DIM]

    scores = [_dot_t(head(k, u, 2, hh), head(q, u, 1, hh)) for u, hh in units]
    ps, lses = [], []
    for (u, hh), s in zip(units, scores):
        valid = window & ((krow >= TK) | (i * nt + u > 0))
        s = jnp.where(valid, s + bias_ref[hh], -jnp.inf)
        m = jnp.max(s, axis=0, keepdims=True)
        e = jnp.exp(s - m)
        den = jnp.sum(e, axis=0, keepdims=True)
        ps.append((e / den).T.astype(BF16))
        lses.append(jnp.broadcast_to(m + jnp.log(den), (HEAD_DIM, TQ)))
    outs = [_dot(p, head(v, u, 2, hh)) for (u, hh), p in zip(units, ps)]

    def tile_parts(parts, u):
        return parts[u * C_HEADS_PER_GROUP:(u + 1) * C_HEADS_PER_GROUP]

    o_all = jnp.concatenate([jnp.concatenate(tile_parts(outs, u), axis=1) for u in range(nt)], axis=0)
    l_all = jnp.concatenate([jnp.concatenate(tile_parts(lses, u), axis=0).T for u in range(nt)], axis=0)
    o_ref[...] = o_all.reshape(o_ref.shape)
    lse_ref[...] = l_all.reshape(lse_ref.shape)


def _dilated_group(qkv, bias, dil, bsz, seq):
    ln = seq // dil
    nt = min(C_TILES_PER_STEP, ln // TQ)
    gw = C_HEADS_PER_GROUP * HEAD_DIM
    rpc = CLASS_TILE // dil
    tiles_per_seq = seq // CLASS_TILE

    def spec(part, nrows, start):
        mid = () if part is None else (None,)
        sel = () if part is None else (part,)
        if rpc >= nrows:
            def index(b, r, i):
                s = start(i)
                return (b * tiles_per_seq + s // rpc, r) + sel + ((s % rpc) // nrows, 0)
            return pl.BlockSpec((None, None) + mid + (nrows, gw), index)
        per = nrows // rpc
        assert tiles_per_seq % per == 0
        return pl.BlockSpec((per, None) + mid + (rpc, gw),
                            lambda b, r, i: (b * (tiles_per_seq // per) + start(i) // nrows, r) + sel + (0, 0))

    def cur(part):
        return spec(part, nt * TQ, lambda i: i * (nt * TQ))

    def prev(part):
        return spec(part, TQ, lambda i: jnp.maximum(i * nt - 1, 0) * TQ)

    return pl.pallas_call(
        _dil_kernel,
        grid=(bsz, dil, ln // (nt * TQ)),
        in_specs=[cur(0), prev(1), cur(1), prev(2), cur(2),
                  pl.BlockSpec(bias.shape, lambda b, r, i: (0, 0, 0))],
        out_specs=[cur(None)] * 2,
        out_shape=[jax.ShapeDtypeStruct(qkv.shape[:2] + (rpc, gw), F32)] * 2,
        compiler_params=_params(("parallel", "parallel", "arbitrary")),
        name=f"dilated_d{dil}",
    )(qkv, qkv, qkv, qkv, qkv, bias)


def _merge_kernel(x_ref, ya_ref, yb_ref, o0_ref, l0_ref, o1_ref, l1_ref, o2_ref, l2_ref, g_ref,
                  wa_ref, wb_ref, wc_ref, wo_ref, out_ref, nat_ref):
    def natural(k, ref):
        dil, rpc = ref.shape[0], ref.shape[1]
        if dil == 1:
            return ref[0]
        ntile = ref.shape[2] // LANES
        for r in range(dil):
            blk = ref[r]
            for c in range(ntile):
                nat_ref[k * ntile + c, pl.ds(r, rpc, stride=dil), :] = blk[:, c * LANES:(c + 1) * LANES]
        return jnp.concatenate([nat_ref[k * ntile + c] for c in range(ntile)], axis=1)

    o0, l0 = natural(0, o0_ref), natural(1, l0_ref)
    o1, l1 = natural(2, o1_ref), natural(3, l1_ref)
    o2, l2 = natural(4, o2_ref), natural(5, l2_ref)
    mx = jnp.maximum(jnp.maximum(l0, l1), l2)
    e0, e1, e2 = jnp.exp(l0 - mx), jnp.exp(l1 - mx), jnp.exp(l2 - mx)
    yc = (e0 * o0 + e1 * o1 + e2 * o2) / (e0 + e1 + e2)
    ya = _dot(ya_ref[...], wa_ref[...])
    yb = _dot(yb_ref[...], wb_ref[...])
    yc = _dot(yc.astype(BF16), wc_ref[...])
    d = D_MODEL
    z = g_ref[:, 0:d] * ya + g_ref[:, d:2 * d] * yb + g_ref[:, 2 * d:3 * d] * yc
    out_ref[...] = x_ref[...] + _dot(z.astype(BF16), wo_ref[...])


def _merge(x2d, ya, yb, c_outs, mixg, wa, wb, wc, wo, tm):
    m = x2d.shape[0]

    def rows(w):
        return pl.BlockSpec((tm, w), lambda i: (i, 0))

    def full(a):
        return pl.BlockSpec(a.shape, lambda i: (0, 0))

    assert tm == CLASS_TILE
    c_flat = [a for pair in c_outs for a in pair]
    c_specs = [pl.BlockSpec((None,) + a.shape[1:], lambda i: (i, 0, 0, 0)) for a in c_flat]
    gw = C_HEADS_PER_GROUP * HEAD_DIM
    return pl.pallas_call(
        _merge_kernel,
        grid=(m // tm,),
        in_specs=[rows(D_MODEL), rows(512), rows(512)] + c_specs + [rows(3 * D_MODEL)]
                 + [full(wa), full(wb), full(wc), full(wo)],
        out_specs=rows(D_MODEL),
        out_shape=jax.ShapeDtypeStruct((m, D_MODEL), F32),
        scratch_shapes=[pltpu.VMEM((len(c_flat) * gw // LANES, tm, LANES), F32)],
        compiler_params=_params(("parallel",)),
        name="merge",
    )(x2d, ya, yb, *c_flat, mixg, wa, wb, wc, wo)


def _ffn_kernel(x_ref, g_ref, wg_ref, wu_ref, wd_ref, out_ref):
    x = x_ref[...]
    ms = jnp.mean(x * x, axis=-1, keepdims=True)
    h = (x * lax.rsqrt(ms + NORM_EPS) * g_ref[...]).astype(BF16)
    gate = _dot(h, wg_ref[...])
    up = _dot(h, wu_ref[...])
    act = gate / (1.0 + jnp.exp(-gate)) * up
    out_ref[...] = x + _dot(act.astype(BF16), wd_ref[...])


def _ffn(x2d, gain, w_in, w_out, tm):
    m = x2d.shape[0]
    wg = w_in[:, :D_FF].astype(BF16)
    wu = w_in[:, D_FF:].astype(BF16)
    wd = w_out.astype(BF16)

    def full(a):
        return pl.BlockSpec(a.shape, lambda i: (0, 0))

    return pl.pallas_call(
        _ffn_kernel,
        grid=(m // tm,),
        in_specs=[pl.BlockSpec((tm, D_MODEL), lambda i: (i, 0)), pl.BlockSpec((1, D_MODEL), lambda i: (0, 0)),
                  full(wg), full(wu), full(wd)],
        out_specs=pl.BlockSpec((tm, D_MODEL), lambda i: (i, 0)),
        out_shape=jax.ShapeDtypeStruct((m, D_MODEL), F32),
        compiler_params=_params(("parallel",)),
        name="ffn",
    )(x2d, gain.reshape(1, D_MODEL).astype(F32), wg, wu, wd)


def _layer(x2d, bsz, seq, norm1_g, norm2_g, w_in, qk, cmp_pos, cmp_w, w_a, w_b, w_c, w_out, w_ffn_in, w_ffn_out,
           bias_a, bias_b, bias_c):
    pieces_ab, pieces_c, pieces_g = _proj_pieces(w_in, qk)
    aq, ak, avt, iwt, iq, ik, bq, bcmp, bk, bvt, bgt = _proj(x2d, norm1_g, pieces_ab, 512)
    c_qkv = _proj(x2d, norm1_g, pieces_c, CLASS_TILE)
    (mixg,) = _proj(x2d, norm1_g, pieces_g, 512)

    ya = _dsa(iq, iwt, ik, aq, ak, avt, *bias_a, bsz, seq)
    kcmp, vcmpt = _nsa_cmp(bcmp, cmp_pos, cmp_w, qk[3], bsz, seq)
    yb = _nsa(bq, bgt, kcmp, vcmpt, bk, bvt, *bias_b, bsz, seq)
    c_outs = [_dilated_group(c_qkv[g], bias_c[g], dil, bsz, seq) for g, (_, dil) in enumerate(C_GROUPS)]

    x1 = _merge(x2d, ya, yb, c_outs, mixg, w_a.astype(BF16), w_b.astype(BF16), w_c.astype(BF16),
                w_out.astype(BF16), CLASS_TILE)
    return _ffn(x1, norm2_g, w_ffn_in, w_ffn_out, 256)


def kernel(x, norm1_g, norm2_g, w_in, qk_norm_g, nsa_cmp_pos, nsa_cmp_w, w_branch_a, w_branch_b, w_branch_c, w_out, w_ffn_in, w_ffn_out, rel_bias):
    bsz, seq, d = x.shape
    assert d == D_MODEL and seq % (TQ * max(dil for _, dil in C_GROUPS)) == 0 and seq % A_TQ == 0
    assert seq % (KB_TILES * TK) == 0 and seq % B_TQ == 0 and seq >= B_WINDOW + B_TQ
    for win, dil in C_GROUPS:
        assert win == TK * dil
    bias_a = _toeplitz_bias(rel_bias[:, :A_HEADS], seq, TK, keys_on_rows=True, scale=LOG2E)
    bias_b = _toeplitz_bias(rel_bias[:, A_HEADS:A_HEADS + B_HEADS], seq, TK, keys_on_rows=True, scale=LOG2E)
    rel_c = rel_bias[:, A_HEADS + B_HEADS:]
    bias_c = [_dilated_bias(rel_c[:, g * C_HEADS_PER_GROUP:(g + 1) * C_HEADS_PER_GROUP], dil)
              for g, (_, dil) in enumerate(C_GROUPS)]
    x2d = x.reshape(bsz * seq, d)
    for layer in range(norm1_g.shape[0]):
        x2d = _layer(x2d, bsz, seq, norm1_g[layer], norm2_g[layer], w_in[layer], qk_norm_g[layer],
                     nsa_cmp_pos[layer], nsa_cmp_w[layer], w_branch_a[layer], w_branch_b[layer],
                     w_branch_c[layer], w_out[layer], w_ffn_in[layer], w_ffn_out[layer],
                     bias_a, bias_b, bias_c)
    return x2d.reshape(bsz, seq, d)
```
